```python
import math
import jax, jax.numpy as jnp
from jax import lax
import numpy as np

D_MODEL = 1024
BATCH = 16
SEQ = 2048
DEPTH = 2
DEC_BATCH = 128
DEC_SEQ = 8
PAST_LEN = 16384
PAGE_SIZE = 128

N_EVEN = (DEPTH + 1) // 2
N_ODD = DEPTH // 2
HEAD_DIM = 64
NORM_EPS = 1e-6

RW_HEADS = 8
RW_WIDTH = RW_HEADS * HEAD_DIM
RW_DECAY_LORA = 64
RW_A_LORA = 64
RW_G_LORA = 128
RW_SPLITS = (RW_WIDTH, RW_DECAY_LORA, RW_WIDTH, RW_WIDTH, RW_A_LORA, RW_G_LORA)
RW_PROJ = sum(RW_SPLITS)
RW_LNX_EPS = 64e-5

MB_HEADS = 8
MB_KV_HEADS = 4
MB_GROUP = MB_HEADS // MB_KV_HEADS
MB_BLOCK = 256
MB_TOPK = 3
MB_QBLOCK = 128

MLA_HEADS = 8
MLA_NOPE = 64
MLA_ROPE = 32
MLA_V = 64
MLA_Q_LORA = 256
MLA_KV_LORA = 128
MLA_QBLOCK = 128
ROPE_THETA = 10000.0

GD_HEADS = 8
GD_DK = 64
GD_DV = 64
GD_CONV = 4
GD_QKV = GD_HEADS * (2 * GD_DK + GD_DV)
GD_CHUNK = 64

P_EVEN = RW_PROJ + (MB_HEADS + 2 * MB_KV_HEADS) * HEAD_DIM
MIX_EVEN = RW_WIDTH + MB_HEADS * HEAD_DIM
ODD_SPLITS = (MLA_Q_LORA, MLA_KV_LORA, MLA_ROPE, GD_QKV, GD_HEADS * GD_DV, GD_HEADS, GD_HEADS)
P_ODD = sum(ODD_SPLITS)
MIX_ODD = MLA_HEADS * MLA_V + GD_HEADS * GD_DV

D_FF = 2816
N_EXPERTS = 8
TOP_K = 2
D_FF_EXPERT = 1408
MOE_ROW_BLOCK = 256

kernel_name = 'hybrid_rwkv7_moba_mla_gdn_decode_step'


def split_cols(p, widths):
    cuts = [int(c) for c in np.cumsum(widths)[:-1]]
    return jnp.split(p, cuts, axis=-1)


def rmsnorm(x, g):
    xf = x.astype(jnp.float32)
    y = xf * lax.rsqrt(jnp.mean(xf * xf, axis=-1, keepdims=True) + NORM_EPS)
    return (y * g.astype(jnp.float32)).astype(x.dtype)


def l2norm(x):
    xf = x.astype(jnp.float32)
    return xf * lax.rsqrt(jnp.sum(xf * xf, axis=-1, keepdims=True) + 1e-12)


def adaln(c, w, b):
    m = (jax.nn.silu(c) @ w + b)[:, None, :]
    return jnp.split(m, 6, axis=-1)


def modulate(x, g, shift, scale):
    return rmsnorm(x, g) * (1.0 + scale) + shift


def masked_softmax(s, mask):
    return jax.nn.softmax(jnp.where(mask, s, -jnp.inf), axis=-1)


def alibi_slopes(n):
    return jnp.asarray([2.0 ** (-8.0 * (h + 1) / n) for h in range(n)], dtype=jnp.float32)


def rope_tables(pos):
    half = MLA_ROPE // 2
    inv = ROPE_THETA ** (-jnp.arange(half, dtype=jnp.float32) / half)
    ang = pos.astype(jnp.float32)[:, None] * inv[None, :]
    return jnp.cos(ang), jnp.sin(ang)


def apply_rope(x, cos, sin):
    x1, x2 = jnp.split(x.astype(jnp.float32), 2, axis=-1)
    return jnp.concatenate([x1 * cos - x2 * sin, x1 * sin + x2 * cos], axis=-1).astype(x.dtype)


def swiglu(h, w_gate, w_up, w_down):
    return (jax.nn.silu(h @ w_gate) * (h @ w_up)) @ w_down


def rwkv7_mixer(p, prev, S0, mu, w0, w2, a0, a2, g2, k_k, k_a, r_k, lnx_g, lnx_b):
    B, T, _ = p.shape
    f32 = jnp.float32
    p_prev = jnp.concatenate([prev[:, None, :].astype(p.dtype), p[:, :-1]], axis=1)
    pm = p + (p_prev - p) * mu
    r, wl, k, v, al, gl = split_cols(pm, RW_SPLITS)
    w_log = -jax.nn.softplus(-(w0 + jnp.tanh(wl) @ w2)) - 0.5
    decay = jnp.exp(-jnp.exp(w_log.astype(f32)))
    a = jax.nn.sigmoid(a0 + al @ a2)
    g = jax.nn.sigmoid(gl) @ g2
    heads = lambda t: t.reshape(B, T, RW_HEADS, HEAD_DIM).astype(f32)
    kk = l2norm(heads(k * k_k))
    k = k * (1.0 + (a - 1.0) * k_a)
    r_h, k_h, v_h, w_h, a_h = heads(r), heads(k), heads(v), heads(decay), heads(a)
    aa, bb = -kk, kk * a_h

    def step(S, inp):
        r_t, w_t, k_t, v_t, a_t, b_t = inp
        sa = jnp.einsum('bhvk,bhk->bhv', S, a_t)
        S = S * w_t[:, :, None, :] + sa[..., None] * b_t[:, :, None, :] + v_t[..., None] * k_t[:, :, None, :]
        return S, jnp.einsum('bhvk,bhk->bhv', S, r_t)

    tm = lambda t: jnp.moveaxis(t, 1, 0)
    S, y = lax.scan(step, S0.astype(f32), tuple(tm(t) for t in (r_h, w_h, k_h, v_h, aa, bb)))
    y = jnp.moveaxis(y, 0, 1)
    mean = jnp.mean(y, axis=-1, keepdims=True)
    var = jnp.mean(jnp.square(y - mean), axis=-1, keepdims=True)
    y = ((y - mean) * lax.rsqrt(var + RW_LNX_EPS)).reshape(B, T, RW_WIDTH) * lnx_g + lnx_b
    bonus = jnp.sum(r_h * k_h * r_k.astype(f32), axis=-1, keepdims=True) * v_h
    y = (y + bonus.reshape(B, T, RW_WIDTH)) * g
    return y.astype(p.dtype), S.astype(S0.dtype), p[:, -1]


def moba_prompt(q, k, v):
    B, T = q.shape[:2]
    f32 = jnp.float32
    nb_full = T // MB_BLOCK
    nb_all = -(-T // MB_BLOCK)
    n_cand = max(nb_full, MB_TOPK)
    pad = nb_all * MB_BLOCK - T
    kvh = jnp.arange(MB_HEADS) // MB_GROUP
    slopes = alibi_slopes(MB_HEADS)
    scale = HEAD_DIM ** -0.5
    kb = jnp.pad(k, ((0, 0), (0, pad), (0, 0), (0, 0))).reshape(B, nb_all, MB_BLOCK, MB_KV_HEADS, HEAD_DIM)
    vb = jnp.pad(v, ((0, 0), (0, pad), (0, 0), (0, 0))).reshape(B, nb_all, MB_BLOCK, MB_KV_HEADS, HEAD_DIM)
    kmean = kb[:, :nb_full].astype(f32).mean(axis=2)
    kmean = jnp.pad(kmean, ((0, 0), (0, n_cand - nb_full), (0, 0), (0, 0)))
    gate = jnp.einsum('bthd,bjhd->bhtj', q.astype(f32), kmean[:, :, kvh])
    qblk = jnp.arange(T) // MB_BLOCK
    valid = jnp.arange(n_cand)[None, :] < qblk[:, None]
    top_s, top_i = lax.top_k(jnp.where(valid, gate, -jnp.inf), MB_TOPK)
    top_ok = top_s > -jnp.inf
    sel = jnp.concatenate([jnp.where(top_ok, top_i, 0),
                           jnp.broadcast_to(qblk[None, None, :, None], (B, MB_HEADS, T, 1)).astype(top_i.dtype)], axis=-1)
    ok = jnp.concatenate([top_ok, jnp.ones((B, MB_HEADS, T, 1), bool)], axis=-1)
    kbt = kb.transpose(0, 3, 1, 2, 4)
    vbt = vb.transpose(0, 3, 1, 2, 4)
    qbt = q.transpose(0, 2, 1, 3)

    def per_seq(args):
        q_s, kb_s, vb_s, sel_s, ok_s = args

        def per_block(i):
            t0 = i * MB_QBLOCK
            qc = lax.dynamic_slice_in_dim(q_s, t0, MB_QBLOCK, axis=1)
            sc = lax.dynamic_slice_in_dim(sel_s, t0, MB_QBLOCK, axis=1)
            oc = lax.dynamic_slice_in_dim(ok_s, t0, MB_QBLOCK, axis=1)
            kg = kb_s[kvh[:, None, None], sc]
            vg = vb_s[kvh[:, None, None], sc]
            kpos = sc[..., None] * MB_BLOCK + jnp.arange(MB_BLOCK)
            dist = (t0 + jnp.arange(MB_QBLOCK))[None, :, None, None] - kpos
            s = jnp.einsum('hqd,hqjkd->hqjk', qc, kg).astype(f32) * scale - slopes[:, None, None, None] * dist.astype(f32)
            mask = oc[..., None] & (dist >= 0)
            pr = masked_softmax(s.reshape(MB_HEADS, MB_QBLOCK, -1), mask.reshape(MB_HEADS, MB_QBLOCK, -1))
            return jnp.einsum('hqjk,hqjkd->qhd', pr.reshape(s.shape).astype(vg.dtype), vg)

        out = lax.map(per_block, jnp.arange(T // MB_QBLOCK))
        return out.reshape(T, MB_HEADS, HEAD_DIM)

    return lax.map(per_seq, (qbt, kbt, vbt, sel, ok))


def moba_sample(q, k_new, v_new, pool_k, pool_v, layer, page_table):
    S = q.shape[1]
    f32 = jnp.float32
    n_pages = page_table.shape[1]
    past = n_pages * PAGE_SIZE
    ppb = MB_BLOCK // PAGE_SIZE
    nb_full = past // MB_BLOCK
    n_cand = max(nb_full, MB_TOPK)
    own_start = nb_full * MB_BLOCK
    kvh = jnp.arange(MB_HEADS) // MB_GROUP
    slopes = alibi_slopes(MB_HEADS)
    scale = HEAD_DIM ** -0.5
    qpos = past + jnp.arange(S)
    valid = jnp.arange(n_cand) < nb_full
    kpos_own = jnp.concatenate([own_start + jnp.arange(past - own_start), qpos])
    dist_own = qpos[:, None] - kpos_own[None, :]

    def per_seq(args):
        q_s, kn_s, vn_s, pt_s = args
        kpool, vpool = pool_k[layer], pool_v[layer]
        k_past = kpool[pt_s].reshape(past, MB_KV_HEADS, HEAD_DIM).astype(q_s.dtype)
        kmean = k_past[:own_start].astype(f32).reshape(nb_full, MB_BLOCK, MB_KV_HEADS, HEAD_DIM).mean(axis=1)
        kmean = jnp.pad(kmean, ((0, n_cand - nb_full), (0, 0), (0, 0)))
        gate = jnp.einsum('shd,jhd->hsj', q_s.astype(f32), kmean[:, kvh])
        top_s, top_i = lax.top_k(jnp.where(valid, gate, -jnp.inf), MB_TOPK)
        top_ok = top_s > -jnp.inf
        top_i = jnp.where(top_ok, top_i, 0)
        pages = pt_s[jnp.minimum(top_i[..., None] * ppb + jnp.arange(ppb), n_pages - 1)]
        hidx = kvh[:, None, None, None]
        kg = kpool[pages, :, hidx].reshape(MB_HEADS, S, MB_TOPK, MB_BLOCK, HEAD_DIM).astype(q_s.dtype)
        vg = vpool[pages, :, hidx].reshape(MB_HEADS, S, MB_TOPK, MB_BLOCK, HEAD_DIM).astype(vn_s.dtype)
        kpos = top_i[..., None] * MB_BLOCK + jnp.arange(MB_BLOCK)
        dist = (qpos[None, :, None, None] - kpos).astype(f32)
        s_top = jnp.einsum('shd,hsjkd->hsjk', q_s, kg).astype(f32) * scale - slopes[:, None, None, None] * dist
        s_top = jnp.where(top_ok[..., None], s_top, -jnp.inf).reshape(MB_HEADS, S, MB_TOPK * MB_BLOCK)
        k_own = jnp.concatenate([k_past[own_start:], kn_s], axis=0)[:, kvh]
        v_past_own = vpool[pt_s[nb_full * ppb:]].reshape(-1, MB_KV_HEADS, HEAD_DIM).astype(vn_s.dtype)
        v_own = jnp.concatenate([v_past_own, vn_s], axis=0)[:, kvh]
        s_own = jnp.einsum('shd,khd->hsk', q_s, k_own).astype(f32) * scale - slopes[:, None, None] * dist_own.astype(f32)
        s_own = jnp.where(dist_own >= 0, s_own, -jnp.inf)
        pr = jax.nn.softmax(jnp.concatenate([s_top, s_own], axis=-1), axis=-1)
        p_top = pr[..., :MB_TOPK * MB_BLOCK].reshape(MB_HEADS, S, MB_TOPK, MB_BLOCK).astype(vg.dtype)
        p_own = pr[..., MB_TOPK * MB_BLOCK:].astype(v_own.dtype)
        return jnp.einsum('hsjk,hsjkd->shd', p_top, vg) + jnp.einsum('hsk,khd->shd', p_own, v_own)

    return lax.map(per_seq, (q, k_new, v_new, page_table))


def even_mixer(h, rw_prev, rw_S0, moba_fn, w_in, w_out, rw_params):
    B, T, _ = h.shape
    p = h @ w_in
    y_rw, S_rw, shift = rwkv7_mixer(p[..., :RW_PROJ], rw_prev, rw_S0, *rw_params)
    q, k, v = split_cols(p[..., RW_PROJ:], (MB_HEADS * HEAD_DIM, MB_KV_HEADS * HEAD_DIM, MB_KV_HEADS * HEAD_DIM))
    q = q.reshape(B, T, MB_HEADS, HEAD_DIM)
    k = k.reshape(B, T, MB_KV_HEADS, HEAD_DIM)
    v = v.reshape(B, T, MB_KV_HEADS, HEAD_DIM)
    y_mb = moba_fn(q, k, v).reshape(B, T, MB_HEADS * HEAD_DIM).astype(y_rw.dtype)
    y = jnp.concatenate([y_rw, y_mb], axis=-1) @ w_out
    return y, k, v, S_rw, shift


def mla_qkv(cq, ckv, kr, cos, sin, q_norm_g, w_q_up, kv_norm_g, w_uk):
    B, T = cq.shape[:2]
    qf = (rmsnorm(cq, q_norm_g) @ w_q_up).reshape(B, T, MLA_HEADS, MLA_NOPE + MLA_ROPE)
    q_nope, q_rope = qf[..., :MLA_NOPE], qf[..., MLA_NOPE:]
    q_rope = apply_rope(q_rope, cos[:, None, :], sin[:, None, :])
    q_lat = jnp.einsum('bthn,chn->bthc', q_nope, w_uk)
    ckv_n = rmsnorm(ckv, kv_norm_g)
    k_rope = apply_rope(kr, cos, sin)
    return q_lat, q_rope, ckv_n, k_rope


def mla_prompt(q_lat, q_rope, ckv, k_rope):
    B, T = q_lat.shape[:2]
    scale = (MLA_NOPE + MLA_ROPE) ** -0.5
    kpos = jnp.arange(T)

    def per_block(i):
        t0 = i * MLA_QBLOCK
        ql = lax.dynamic_slice_in_dim(q_lat, t0, MLA_QBLOCK, axis=1)
        qr = lax.dynamic_slice_in_dim(q_rope, t0, MLA_QBLOCK, axis=1)
        s = (jnp.einsum('bqhc,bkc->bhqk', ql, ckv) + jnp.einsum('bqhr,bkr->bhqk', qr, k_rope)).astype(jnp.float32) * scale
        mask = (t0 + jnp.arange(MLA_QBLOCK))[:, None] >= kpos[None, :]
        pr = masked_softmax(s, mask)
        return jnp.einsum('bhqk,bkc->bqhc', pr.astype(ckv.dtype), ckv)

    o = lax.map(per_block, jnp.arange(T // MLA_QBLOCK))
    return jnp.moveaxis(o, 0, 1).reshape(B, T, MLA_HEADS, MLA_KV_LORA)


def mla_sample(q_lat, q_rope, ckv_new, kr_new, pool_ckv, pool_kr, layer, page_table):
    S = q_lat.shape[1]
    past = page_table.shape[1] * PAGE_SIZE
    scale = (MLA_NOPE + MLA_ROPE) ** -0.5
    kpos = jnp.arange(past + S)
    qpos = past + jnp.arange(S)
    mask = kpos[None, None, :] <= qpos[None, :, None]

    def per_seq(args):
        ql, qr, cn, kn, pt = args
        c_all = jnp.concatenate([pool_ckv[layer][pt].reshape(past, MLA_KV_LORA).astype(cn.dtype), cn], axis=0)
        r_all = jnp.concatenate([pool_kr[layer][pt].reshape(past, MLA_ROPE).astype(kn.dtype), kn], axis=0)
        s = (jnp.einsum('shc,kc->hsk', ql, c_all) + jnp.einsum('shr,kr->hsk', qr, r_all)).astype(jnp.float32) * scale
        pr = masked_softmax(s, mask)
        return jnp.einsum('hsk,kc->shc', pr.astype(c_all.dtype), c_all)

    return lax.map(per_seq, (q_lat, q_rope, ckv_new, kr_new, page_table))


def gated_delta_chunked(q, k, v, g, beta, S0):
    B, T, H, Dk = q.shape
    Dv = v.shape[-1]
    C = math.gcd(T, GD_CHUNK)
    N = T // C

    def chunks(t):
        t = t.reshape((B, N, C, H) + t.shape[3:])
        return jnp.moveaxis(jnp.moveaxis(t, 1, 0), 3, 2)

    q, k, v, g, beta = (chunks(t) for t in (q, k, v, g, beta))
    gc = jnp.cumsum(g, axis=-1)
    incl = jnp.tril(jnp.ones((C, C), bool))
    strict = jnp.tril(jnp.ones((C, C), bool), -1)
    decay = jnp.exp(jnp.where(incl, gc[..., :, None] - gc[..., None, :], -jnp.inf))
    kb = k * beta[..., None]
    A = jnp.where(strict, jnp.einsum('...id,...jd->...ij', kb, k) * decay, 0.0)
    rhs = jnp.concatenate([v * beta[..., None], kb * jnp.exp(gc)[..., None]], axis=-1)
    X = lax.linalg.triangular_solve(jnp.eye(C, dtype=A.dtype) + A, rhs, left_side=True, lower=True, unit_diagonal=True)
    u, w = X[..., :Dv], X[..., Dv:]
    qk = jnp.where(incl, jnp.einsum('...id,...jd->...ij', q, k) * decay, 0.0)

    def step(S, xs):
        q_c, k_c, u_c, w_c, gc_c, qk_c = xs
        v_new = u_c - jnp.einsum('bhck,bhkv->bhcv', w_c, S)
        o = jnp.einsum('bhck,bhkv->bhcv', q_c * jnp.exp(gc_c)[..., None], S) + jnp.einsum('bhij,bhjv->bhiv', qk_c, v_new)
        g_last = gc_c[..., -1:]
        S = S * jnp.exp(g_last)[..., None] + jnp.einsum('bhck,bhcv->bhkv', k_c * jnp.exp(g_last - gc_c)[..., None], v_new)
        return S, o

    S, o = lax.scan(step, S0, (q, k, u, w, gc, qk))
    o = jnp.moveaxis(jnp.moveaxis(o, 2, 3), 0, 1).reshape(B, T, H, Dv)
    return o, S


def gdn_mixer(qkv, z, a_raw, b_raw, conv_buf, S0, conv_w, a_log, dt_bias, norm_g):
    B, T, _ = qkv.shape
    f32 = jnp.float32
    xx = jnp.concatenate([conv_buf.astype(qkv.dtype), qkv], axis=1)
    y = lax.conv_general_dilated(xx, conv_w[:, None, :].astype(xx.dtype), window_strides=(1,), padding='VALID',
                                 dimension_numbers=('NWC', 'WIO', 'NWC'), feature_group_count=GD_QKV)
    y = jax.nn.silu(y)
    q, k, v = split_cols(y, (GD_HEADS * GD_DK, GD_HEADS * GD_DK, GD_HEADS * GD_DV))
    q = l2norm(q.reshape(B, T, GD_HEADS, GD_DK)) * (GD_DK ** -0.5)
    k = l2norm(k.reshape(B, T, GD_HEADS, GD_DK))
    v = v.reshape(B, T, GD_HEADS, GD_DV).astype(f32)
    beta = jax.nn.sigmoid(b_raw.astype(f32))
    g = -jnp.exp(a_log.astype(f32)) * jax.nn.softplus(a_raw.astype(f32) + dt_bias.astype(f32))
    o, S = gated_delta_chunked(q, k, v, g, beta, S0.astype(f32))
    o = rmsnorm(o, norm_g) * jax.nn.silu(z.reshape(B, T, GD_HEADS, GD_DV).astype(f32))
    return o.reshape(B, T, GD_HEADS * GD_DV).astype(qkv.dtype), S.astype(S0.dtype), xx[:, -(GD_CONV - 1):]


def odd_mixer(h, cos, sin, gdn_buf, gdn_S0, mla_fn, w_in, w_out, mla_params, gdn_params):
    B, T, _ = h.shape
    p = h @ w_in
    cq, ckv, kr, qkv, z, a_raw, b_raw = split_cols(p, ODD_SPLITS)
    q_norm, w_q_up, kv_norm, w_uk, w_uv = mla_params
    q_lat, q_rope, ckv_n, k_rope = mla_qkv(cq, ckv, kr, cos, sin, q_norm, w_q_up, kv_norm, w_uk)
    o_lat = mla_fn(q_lat, q_rope, ckv_n, k_rope)
    y_mla = jnp.einsum('bthc,chv->bthv', o_lat, w_uv).reshape(B, T, MLA_HEADS * MLA_V)
    y_gdn, S, buf = gdn_mixer(qkv, z, a_raw, b_raw, gdn_buf, gdn_S0, *gdn_params)
    y = jnp.concatenate([y_mla, y_gdn.astype(y_mla.dtype)], axis=-1) @ w_out
    return y, ckv_n, k_rope, S, buf


def moe_swiglu(h, w_router, b_router, w_gate, w_up, w_down):
    shp = h.shape
    D = shp[-1]
    f32 = jnp.float32
    x = h.reshape(-1, D)
    N = x.shape[0]
    R = MOE_ROW_BLOCK
    logits = (x @ w_router).astype(f32) + b_router.astype(f32)
    top_v, top_e = lax.top_k(logits, TOP_K)
    top_w = jax.nn.softmax(top_v, axis=-1)
    NK = N * TOP_K
    flat_e = top_e.reshape(NK)
    flat_w = top_w.reshape(NK)
    flat_tok = jnp.arange(NK, dtype=jnp.int32) // TOP_K
    order = jnp.argsort(flat_e)
    e_s, w_s, tok_s = flat_e[order], flat_w[order], flat_tok[order]
    counts = jnp.bincount(flat_e, length=N_EXPERTS)
    padded = (counts + R - 1) // R * R
    start = jnp.cumsum(counts) - counts
    pstart = jnp.cumsum(padded) - padded
    dest = pstart[e_s] + jnp.arange(NK) - start[e_s]
    n_blocks = -(-NK // R) + N_EXPERTS
    row_tok = jnp.full((n_blocks * R,), N, jnp.int32).at[dest].set(tok_s)
    row_w = jnp.zeros((n_blocks * R,), f32).at[dest].set(w_s)
    block_e = jnp.minimum(jnp.searchsorted(jnp.cumsum(padded), jnp.arange(n_blocks) * R, side='right'), N_EXPERTS - 1)
    x_pad = jnp.concatenate([x, jnp.zeros((1, D), x.dtype)], axis=0)

    def per_block(args):
        rows, e = args
        xb = x_pad[rows]
        return swiglu(xb, w_gate[e], w_up[e], w_down[e])

    yb = lax.map(per_block, (row_tok.reshape(n_blocks, R), block_e)).reshape(-1, D)
    y = jnp.zeros((N + 1, D), f32).at[row_tok].add(yb.astype(f32) * row_w[:, None])
    return y[:N].reshape(shp).astype(h.dtype)


def setup_inputs(seed: int = 0) -> dict:
    f32 = jnp.float32
    keys = list(jax.random.split(jax.random.key(seed), 64))

    def nrm(shape, scale=1.0):
        return jax.random.normal(keys.pop(), shape, f32) * scale

    def gain(shape):
        return 1.0 + 0.05 * jax.random.normal(keys.pop(), shape, f32)

    def unif(shape, lo, hi):
        return jax.random.uniform(keys.pop(), shape, f32, lo, hi)

    n_pages = PAST_LEN // PAGE_SIZE
    n_used = DEC_BATCH * n_pages
    n_pool = n_used + max(1, n_used // 4)
    page_table = jax.random.permutation(keys.pop(), n_pool)[:n_used].reshape(DEC_BATCH, n_pages).astype(jnp.int32)
    dt = jnp.exp(unif((N_ODD, GD_HEADS), math.log(1e-3), math.log(1e-1)))
    inv_d = D_MODEL ** -0.5
    return {
        'x_prompt': nrm((BATCH, SEQ, D_MODEL)),
        'x_sample': nrm((DEC_BATCH, DEC_SEQ, D_MODEL)),
        'cache_moba_k': nrm((N_EVEN, n_pool, PAGE_SIZE, MB_KV_HEADS, HEAD_DIM)),
        'cache_moba_v': nrm((N_EVEN, n_pool, PAGE_SIZE, MB_KV_HEADS, HEAD_DIM)),
        'cache_mla_ckv': nrm((N_ODD, n_pool, PAGE_SIZE, MLA_KV_LORA)),
        'cache_mla_krope': nrm((N_ODD, n_pool, PAGE_SIZE, MLA_ROPE)),
        'state_rwkv_wkv': nrm((N_EVEN, DEC_BATCH, RW_HEADS, HEAD_DIM, HEAD_DIM), 0.3),
        'state_rwkv_shift': nrm((N_EVEN, DEC_BATCH, RW_PROJ)),
        'state_gdn': nrm((N_ODD, DEC_BATCH, GD_HEADS, GD_DK, GD_DV), 0.3),
        'state_gdn_conv': nrm((N_ODD, DEC_BATCH, GD_CONV - 1, GD_QKV)),
        'page_table': page_table,
        'c_prompt': nrm((BATCH, D_MODEL)),
        'c_sample': nrm((DEC_BATCH, D_MODEL)),
        'w_ada': nrm((DEPTH, D_MODEL, 6 * D_MODEL), 0.5 * inv_d),
        'b_ada': nrm((DEPTH, 6 * D_MODEL), 0.02),
        'g_mix_pre': gain((DEPTH, D_MODEL)),
        'g_mix_post': gain((DEPTH, D_MODEL)),
        'g_ff_pre': gain((DEPTH, D_MODEL)),
        'g_ff_post': gain((DEPTH, D_MODEL)),
        'w_in_even': nrm((N_EVEN, D_MODEL, P_EVEN), inv_d),
        'w_out_even': nrm((N_EVEN, MIX_EVEN, D_MODEL), MIX_EVEN ** -0.5),
        'rw_mu': unif((N_EVEN, RW_PROJ), 0.0, 1.0),
        'rw_w0': nrm((N_EVEN, RW_WIDTH), 0.5),
        'rw_w2': nrm((N_EVEN, RW_DECAY_LORA, RW_WIDTH), RW_DECAY_LORA ** -0.5),
        'rw_a0': nrm((N_EVEN, RW_WIDTH), 0.5),
        'rw_a2': nrm((N_EVEN, RW_A_LORA, RW_WIDTH), RW_A_LORA ** -0.5),
        'rw_g2': nrm((N_EVEN, RW_G_LORA, RW_WIDTH), RW_G_LORA ** -0.5),
        'rw_kk': 0.85 + nrm((N_EVEN, RW_WIDTH), 0.05),
        'rw_ka': gain((N_EVEN, RW_WIDTH)),
        'rw_rk': nrm((N_EVEN, RW_HEADS, HEAD_DIM), 0.1),
        'rw_lnx_g': gain((N_EVEN, RW_WIDTH)),
        'rw_lnx_b': nrm((N_EVEN, RW_WIDTH), 0.02),
        'ffn_w_gate': nrm((N_EVEN, D_MODEL, D_FF), inv_d),
        'ffn_w_up': nrm((N_EVEN, D_MODEL, D_FF), inv_d),
        'ffn_w_down': nrm((N_EVEN, D_FF, D_MODEL), D_FF ** -0.5),
        'w_in_odd': nrm((N_ODD, D_MODEL, P_ODD), inv_d),
        'w_out_odd': nrm((N_ODD, MIX_ODD, D_MODEL), MIX_ODD ** -0.5),
        'mla_q_norm': gain((N_ODD, MLA_Q_LORA)),
        'mla_w_q_up': nrm((N_ODD, MLA_Q_LORA, MLA_HEADS * (MLA_NOPE + MLA_ROPE)), MLA_Q_LORA ** -0.5),
        'mla_kv_norm': gain((N_ODD, MLA_KV_LORA)),
        'mla_w_uk': nrm((N_ODD, MLA_KV_LORA, MLA_HEADS, MLA_NOPE), MLA_KV_LORA ** -0.5),
        'mla_w_uv': nrm((N_ODD, MLA_KV_LORA, MLA_HEADS, MLA_V), MLA_KV_LORA ** -0.5),
        'gdn_conv_w': nrm((N_ODD, GD_CONV, GD_QKV), 0.5),
        'gdn_a_log': jnp.log(unif((N_ODD, GD_HEADS), 1.0, 16.0)),
        'gdn_dt_bias': dt + jnp.log(-jnp.expm1(-dt)),
        'gdn_norm_g': gain((N_ODD, GD_DV)),
        'moe_w_router': nrm((N_ODD, D_MODEL, N_EXPERTS), inv_d),
        'moe_b_router': nrm((N_ODD, N_EXPERTS), 0.01),
        'moe_w_gate': nrm((N_ODD, N_EXPERTS, D_MODEL, D_FF_EXPERT), inv_d),
        'moe_w_up': nrm((N_ODD, N_EXPERTS, D_MODEL, D_FF_EXPERT), inv_d),
        'moe_w_down': nrm((N_ODD, N_EXPERTS, D_FF_EXPERT, D_MODEL), D_FF_EXPERT ** -0.5),
    }


def reference(x_prompt, x_sample, cache_moba_k, cache_moba_v, cache_mla_ckv, cache_mla_krope,
              state_rwkv_wkv, state_rwkv_shift, state_gdn, state_gdn_conv, page_table, c_prompt, c_sample,
              w_ada, b_ada, g_mix_pre, g_mix_post, g_ff_pre, g_ff_post,
              w_in_even, w_out_even, rw_mu, rw_w0, rw_w2, rw_a0, rw_a2, rw_g2, rw_kk, rw_ka, rw_rk,
              rw_lnx_g, rw_lnx_b, ffn_w_gate, ffn_w_up, ffn_w_down,
              w_in_odd, w_out_odd, mla_q_norm, mla_w_q_up, mla_kv_norm, mla_w_uk, mla_w_uv,
              gdn_conv_w, gdn_a_log, gdn_dt_bias, gdn_norm_g,
              moe_w_router, moe_b_router, moe_w_gate, moe_w_up, moe_w_down):
    Bp, Tp, _ = x_prompt.shape
    Bs, Ts, _ = x_sample.shape
    past_len = page_table.shape[1] * PAGE_SIZE
    cos_p, sin_p = rope_tables(jnp.arange(Tp))
    cos_s, sin_s = rope_tables(past_len + jnp.arange(Ts))
    xp, xs = x_prompt, x_sample
    names = ('mk', 'mv', 'wkv', 'shift', 'ckv', 'krope', 'gdn', 'conv')
    newp = {n: [] for n in names}
    news = {n: [] for n in names}
    for layer in range(DEPTH):
        i = layer // 2
        sh1p, sc1p, ga1p, sh2p, sc2p, ga2p = adaln(c_prompt, w_ada[layer], b_ada[layer])
        sh1s, sc1s, ga1s, sh2s, sc2s, ga2s = adaln(c_sample, w_ada[layer], b_ada[layer])
        hp = modulate(xp, g_mix_pre[layer], sh1p, sc1p)
        hs = modulate(xs, g_mix_pre[layer], sh1s, sc1s)
        if layer % 2 == 0:
            rw_params = (rw_mu[i], rw_w0[i], rw_w2[i], rw_a0[i], rw_a2[i], rw_g2[i], rw_kk[i], rw_ka[i],
                         rw_rk[i], rw_lnx_g[i], rw_lnx_b[i])
            mp, kp_, vp_, Sp, shp = even_mixer(
                hp, jnp.zeros((Bp, RW_PROJ), hp.dtype),
                jnp.zeros((Bp, RW_HEADS, HEAD_DIM, HEAD_DIM), state_rwkv_wkv.dtype),
                moba_prompt, w_in_even[i], w_out_even[i], rw_params)
            ms, ks_, vs_, Ss, shs = even_mixer(
                hs, state_rwkv_shift[i], state_rwkv_wkv[i],
                lambda q, k, v: moba_sample(q, k, v, cache_moba_k, cache_moba_v, i, page_table),
                w_in_even[i], w_out_even[i], rw_params)
            for d, vals in ((newp, (kp_, vp_, Sp, shp)), (news, (ks_, vs_, Ss, shs))):
                d['mk'].append(vals[0]); d['mv'].append(vals[1]); d['wkv'].append(vals[2]); d['shift'].append(vals[3])
        else:
            mla_params = (mla_q_norm[i], mla_w_q_up[i], mla_kv_norm[i], mla_w_uk[i], mla_w_uv[i])
            gdn_params = (gdn_conv_w[i], gdn_a_log[i], gdn_dt_bias[i], gdn_norm_g[i])
            mp, cp_, rp_, Sp, bp_ = odd_mixer(
                hp, cos_p, sin_p, jnp.zeros((Bp, GD_CONV - 1, GD_QKV), hp.dtype),
                jnp.zeros((Bp, GD_HEADS, GD_DK, GD_DV), state_gdn.dtype),
                mla_prompt, w_in_odd[i], w_out_odd[i], mla_params, gdn_params)
            ms, cs_, rs_, Ss, bs_ = odd_mixer(
                hs, cos_s, sin_s, state_gdn_conv[i], state_gdn[i],
                lambda a, b, c, d: mla_sample(a, b, c, d, cache_mla_ckv, cache_mla_krope, i, page_table),
                w_in_odd[i], w_out_odd[i], mla_params, gdn_params)
            for d, vals in ((newp, (cp_, rp_, Sp, bp_)), (news, (cs_, rs_, Ss, bs_))):
                d['ckv'].append(vals[0]); d['krope'].append(vals[1]); d['gdn'].append(vals[2]); d['conv'].append(vals[3])
        xp = xp + ga1p * rmsnorm(mp, g_mix_post[layer])
        xs = xs + ga1s * rmsnorm(ms, g_mix_post[layer])
        hp = modulate(xp, g_ff_pre[layer], sh2p, sc2p)
        hs = modulate(xs, g_ff_pre[layer], sh2s, sc2s)
        if layer % 2 == 0:
            fp = swiglu(hp, ffn_w_gate[i], ffn_w_up[i], ffn_w_down[i])
            fs = swiglu(hs, ffn_w_gate[i], ffn_w_up[i], ffn_w_down[i])
        else:
            fp = moe_swiglu(hp, moe_w_router[i], moe_b_router[i], moe_w_gate[i], moe_w_up[i], moe_w_down[i])
            fs = moe_swiglu(hs, moe_w_router[i], moe_b_router[i], moe_w_gate[i], moe_w_up[i], moe_w_down[i])
        xp = xp + ga2p * rmsnorm(fp, g_ff_post[layer])
        xs = xs + ga2s * rmsnorm(fs, g_ff_post[layer])
    p_moba_k = jnp.stack(newp['mk'])
    p_moba_v = jnp.stack(newp['mv'])
    p_rwkv_wkv = jnp.stack(newp['wkv'])
    p_rwkv_shift = jnp.stack(newp['shift'])
    p_mla_ckv = jnp.stack(newp['ckv'])
    p_mla_krope = jnp.stack(newp['krope'])
    p_gdn = jnp.stack(newp['gdn'])
    p_gdn_conv = jnp.stack(newp['conv'])
    s_moba_k = jnp.stack(news['mk'])
    s_moba_v = jnp.stack(news['mv'])
    s_rwkv_wkv = jnp.stack(news['wkv'])
    s_rwkv_shift = jnp.stack(news['shift'])
    s_mla_ckv = jnp.stack(news['ckv'])
    s_mla_krope = jnp.stack(news['krope'])
    s_gdn = jnp.stack(news['gdn'])
    s_gdn_conv = jnp.stack(news['conv'])
    return (xp, xs, p_moba_k, p_moba_v, p_rwkv_wkv, p_rwkv_shift, p_mla_ckv, p_mla_krope, p_gdn, p_gdn_conv,
            s_moba_k, s_moba_v, s_rwkv_wkv, s_rwkv_shift, s_mla_ckv, s_mla_krope, s_gdn, s_gdn_conv)
```

```python
import functools
import math

import jax
import jax.numpy as jnp
import numpy as np
from jax import lax
from jax.experimental import pallas as pl
from jax.experimental.pallas import tpu as pltpu

F32 = jnp.float32
BF16 = jnp.bfloat16

NORM_EPS = 1e-6
HEAD_DIM = 64
PAGE_SIZE = 128
RW_HEADS = 8
RW_WIDTH = RW_HEADS * HEAD_DIM
RW_SPLITS = (RW_WIDTH, 64, RW_WIDTH, RW_WIDTH, 64, 128)
RW_PROJ = sum(RW_SPLITS)
RW_LNX_EPS = 64e-5
MB_HEADS = 8
MB_KV_HEADS = 4
MB_GROUP = MB_HEADS // MB_KV_HEADS
MB_BLOCK = 256
MB_TOPK = 3
MLA_HEADS = 8
MLA_NOPE = 64
MLA_ROPE = 32
MLA_V = 64
MLA_Q_LORA = 256
MLA_KV_LORA = 128
ROPE_THETA = 10000.0
GD_HEADS = 8
GD_DK = 64
GD_DV = 64
GD_CONV = 4
GD_QKV = GD_HEADS * (2 * GD_DK + GD_DV)
ODD_SPLITS = (MLA_Q_LORA, MLA_KV_LORA, MLA_ROPE, GD_QKV, GD_HEADS * GD_DV, GD_HEADS, GD_HEADS)
N_EXPERTS = 8

LANES = 128
VMEM_LIMIT = 56 * 1024 * 1024


def _cparams(sem):
    return pltpu.CompilerParams(dimension_semantics=sem, vmem_limit_bytes=VMEM_LIMIT)


def _silu(x):
    return x * jax.nn.sigmoid(x)


def _rms(x, eps=NORM_EPS):
    return x * lax.rsqrt(jnp.mean(x * x, axis=-1, keepdims=True) + eps)


def _adaln_body(c_ref, w_ref, b_ref, o_ref):
    c = c_ref[...]
    o_ref[...] = jnp.dot(_silu(c).astype(BF16), w_ref[...], preferred_element_type=F32) + b_ref[...]


def adaln_call(c, w, b, tn=1536):
    R, D = c.shape
    N = w.shape[1]
    return pl.pallas_call(
        _adaln_body,
        grid=(N // tn,),
        in_specs=[pl.BlockSpec((R, D), lambda j: (0, 0)),
                  pl.BlockSpec((D, tn), lambda j: (0, j)),
                  pl.BlockSpec((1, tn), lambda j: (0, j))],
        out_specs=pl.BlockSpec((R, tn), lambda j: (0, j)),
        out_shape=jax.ShapeDtypeStruct((R, N), F32),
        compiler_params=_cparams(("arbitrary",)),
        name="adaln",
    )(c, w, b)


def _mod_spec(mod, tm):
    if mod.shape[1] == 1:
        return pl.BlockSpec((None, 1, mod.shape[2]), lambda g, i, *_: (g, 0, 0))
    return pl.BlockSpec((None, tm, mod.shape[2]), lambda g, i, *_: (g, i, 0))


def _modmm_body(x_ref, sh_ref, sc_ref, g_ref, w_ref, *o_refs, splits):
    h = _rms(x_ref[...]) * g_ref[...] * (1.0 + sc_ref[...]) + sh_ref[...]
    y = jnp.dot(h.astype(BF16), w_ref[...], preferred_element_type=F32)
    off = 0
    for o_ref, n in zip(o_refs, splits):
        o_ref[...] = y[:, off:off + n].astype(o_ref.dtype)
        off += n


def modmm_call(x, shift, scale, gain, w, splits, tm=512):
    G, R, D = x.shape
    tm = min(tm, R)
    N = w.shape[1]
    assert sum(splits) == N and all(s % LANES == 0 for s in splits)
    return pl.pallas_call(
        functools.partial(_modmm_body, splits=splits),
        grid=(G, R // tm),
        in_specs=[pl.BlockSpec((None, tm, D), lambda g, i: (g, i, 0)),
                  _mod_spec(shift, tm), _mod_spec(scale, tm),
                  pl.BlockSpec((1, D), lambda g, i: (0, 0)),
                  pl.BlockSpec((D, N), lambda g, i: (0, 0))],
        out_specs=[pl.BlockSpec((None, tm, n), lambda g, i: (g, i, 0)) for n in splits],
        out_shape=[jax.ShapeDtypeStruct((G, R, n), F32) for n in splits],
        compiler_params=_cparams(("arbitrary", "arbitrary")),
        name="modmm",
    )(x, shift, scale, gain, w)


def _outproj_body(ya_ref, yb_ref, wa_ref, wb_ref, x_ref, ga_ref, gp_ref, o_ref):
    y = jnp.dot(ya_ref[...].astype(BF16), wa_ref[...], preferred_element_type=F32)
    y = y + jnp.dot(yb_ref[...].astype(BF16), wb_ref[...], preferred_element_type=F32)
    o_ref[...] = x_ref[...] + ga_ref[...] * (_rms(y) * gp_ref[...])


def outproj_call(ya, yb, wa, wb, x, gate, gpost, tm=512):
    G, R, D = x.shape
    tm = min(tm, R)
    Ka, Kb = ya.shape[2], yb.shape[2]
    return pl.pallas_call(
        _outproj_body,
        grid=(G, R // tm),
        in_specs=[pl.BlockSpec((None, tm, Ka), lambda g, i: (g, i, 0)),
                  pl.BlockSpec((None, tm, Kb), lambda g, i: (g, i, 0)),
                  pl.BlockSpec((Ka, D), lambda g, i: (0, 0)),
                  pl.BlockSpec((Kb, D), lambda g, i: (0, 0)),
                  pl.BlockSpec((None, tm, D), lambda g, i: (g, i, 0)),
                  _mod_spec(gate, tm),
                  pl.BlockSpec((1, D), lambda g, i: (0, 0))],
        out_specs=pl.BlockSpec((None, tm, D), lambda g, i: (g, i, 0)),
        out_shape=jax.ShapeDtypeStruct((G, R, D), F32),
        compiler_params=_cparams(("arbitrary", "arbitrary")),
        name="outproj",
    )(ya, yb, wa, wb, x, gate, gpost)


def _ffn_body(x_ref, sh_ref, sc_ref, ga_ref, gpre_ref, gpost_ref, wr_ref, br_ref,
              wg_ref, wu_ref, wd_ref, o_ref, h_ref, acc_ref, rw_ref, *, moe):
    j = pl.program_id(2)

    @pl.when(j == 0)
    def _():
        h = _rms(x_ref[...]) * gpre_ref[...] * (1.0 + sc_ref[...]) + sh_ref[...]
        h_ref[...] = h.astype(BF16)
        acc_ref[...] = jnp.zeros_like(acc_ref)
        if moe:
            logits = jnp.dot(h, wr_ref[...], preferred_element_type=F32,
                             precision=lax.Precision.HIGHEST) + br_ref[...]
            lane = lax.broadcasted_iota(jnp.int32, logits.shape, 1)
            neg = jnp.float32(-jnp.inf)
            logits = jnp.where(lane < N_EXPERTS, logits, neg)
            m1 = jnp.max(logits, axis=-1, keepdims=True)
            i1 = jnp.min(jnp.where(logits == m1, lane, LANES), axis=-1, keepdims=True)
            rest = jnp.where(lane == i1, neg, logits)
            m2 = jnp.max(rest, axis=-1, keepdims=True)
            i2 = jnp.min(jnp.where(rest == m2, lane, LANES), axis=-1, keepdims=True)
            e2 = jnp.exp(m2 - m1)
            w1 = 1.0 / (1.0 + e2)
            w2 = e2 / (1.0 + e2)
            rw_ref[...] = jnp.where(lane == i1, w1, 0.0) + jnp.where(lane == i2, w2, 0.0)

    hb = h_ref[...]
    g = jnp.dot(hb, wg_ref[...], preferred_element_type=F32)
    u = jnp.dot(hb, wu_ref[...], preferred_element_type=F32)
    f = jnp.dot((_silu(g) * u).astype(BF16), wd_ref[...], preferred_element_type=F32)
    if moe:
        lane = lax.broadcasted_iota(jnp.int32, rw_ref.shape, 1)
        we = jnp.sum(jnp.where(lane == j, rw_ref[...], 0.0), axis=-1, keepdims=True)
        f = f * we
    acc_ref[...] += f

    @pl.when(j == pl.num_programs(2) - 1)
    def _():
        o_ref[...] = x_ref[...] + ga_ref[...] * (_rms(acc_ref[...]) * gpost_ref[...])


def ffn_call(x, shift, scale, gate, gpre, gpost, wg, wu, wd, w_router=None, b_router=None, tm=512):
    G, R, D = x.shape
    tm = min(tm, R)
    J, _, Fd = wg.shape
    moe = w_router is not None
    if not moe:
        w_router = jnp.zeros((D, LANES), F32)
        b_router = jnp.zeros((1, LANES), F32)
    return pl.pallas_call(
        functools.partial(_ffn_body, moe=moe),
        grid=(G, R // tm, J),
        in_specs=[pl.BlockSpec((None, tm, D), lambda g, i, j: (g, i, 0)),
                  _mod_spec(shift, tm), _mod_spec(scale, tm), _mod_spec(gate, tm),
                  pl.BlockSpec((1, D), lambda g, i, j: (0, 0)),
                  pl.BlockSpec((1, D), lambda g, i, j: (0, 0)),
                  pl.BlockSpec((D, LANES), lambda g, i, j: (0, 0)),
                  pl.BlockSpec((1, LANES), lambda g, i, j: (0, 0)),
                  pl.BlockSpec((None, D, Fd), lambda g, i, j: (j, 0, 0)),
                  pl.BlockSpec((None, D, Fd), lambda g, i, j: (j, 0, 0)),
                  pl.BlockSpec((None, Fd, D), lambda g, i, j: (j, 0, 0))],
        out_specs=pl.BlockSpec((None, tm, D), lambda g, i, j: (g, i, 0)),
        out_shape=jax.ShapeDtypeStruct((G, R, D), F32),
        scratch_shapes=[pltpu.VMEM((tm, D), BF16), pltpu.VMEM((tm, D), F32),
                        pltpu.VMEM((tm, LANES), F32)],
        compiler_params=_cparams(("arbitrary", "arbitrary", "arbitrary")),
        name="moe" if moe else "ffn",
    )(x, shift, scale, gate, gpre, gpost, w_router, b_router, wg, wu, wd)


def _scan_body(r_ref, w_ref, k_ref, v_ref, a_ref, b_ref, an_ref, s0_ref, y_ref, s_ref, sa_ref,
               *, tt, dk, norm):
    i = pl.program_id(1)

    @pl.when(i == 0)
    def _():
        s_ref[...] = s0_ref[...]
        sa = jnp.zeros(sa_ref.shape, F32)
        for k in range(dk):
            sa = sa + s0_ref[k] * a_ref[0, k:k + 1, :]
        sa_ref[...] = sa

    def step(t, sa):
        v_t = v_ref[t]
        tn = jnp.minimum(t + 1, tt - 1)
        last = t == tt - 1
        y = jnp.zeros(sa.shape, F32)
        sa_n = jnp.zeros(sa.shape, F32)
        for k in range(dk):
            new = (s_ref[k] * w_ref[t, k:k + 1, :] + sa * b_ref[t, k:k + 1, :]
                   + v_t * k_ref[t, k:k + 1, :])
            s_ref[k] = new
            y = y + new * r_ref[t, k:k + 1, :]
            a_next = jnp.where(last, an_ref[0, k:k + 1, :], a_ref[tn, k:k + 1, :])
            sa_n = sa_n + new * a_next
        if norm == "layer":
            mu = jnp.mean(y, axis=0, keepdims=True)
            d = y - mu
            y = d * lax.rsqrt(jnp.mean(d * d, axis=0, keepdims=True) + RW_LNX_EPS)
        else:
            y = y * lax.rsqrt(jnp.mean(y * y, axis=0, keepdims=True) + NORM_EPS)
        y_ref[t] = y
        return sa_n

    sa_ref[...] = lax.fori_loop(0, tt, step, sa_ref[...])


def scan_call(r, w, k, v, a, b, s0, norm, tt=32):
    G, T, dk, L = r.shape
    dv = v.shape[2]
    tt = min(tt, T)
    nt = T // tt
    tspec = lambda d: pl.BlockSpec((None, tt, d, L), lambda g, i: (g, i, 0, 0))
    nxt = pl.BlockSpec((None, 1, dk, L), lambda g, i: (g, jnp.minimum((i + 1) * tt, T - 1), 0, 0))
    sspec = pl.BlockSpec((None, dk, dv, L), lambda g, i: (g, 0, 0, 0))
    return pl.pallas_call(
        functools.partial(_scan_body, tt=tt, dk=dk, norm=norm),
        grid=(G, nt),
        in_specs=[tspec(dk), tspec(dk), tspec(dk), tspec(dv), tspec(dk), tspec(dk), nxt, sspec],
        out_specs=[tspec(dv), sspec],
        out_shape=[jax.ShapeDtypeStruct((G, T, dv, L), F32),
                   jax.ShapeDtypeStruct((G, dk, dv, L), F32)],
        scratch_shapes=[pltpu.VMEM((dv, L), F32)],
        compiler_params=_cparams(("arbitrary", "arbitrary")),
        name="scan",
    )(r, w, k, v, a, b, a, s0)


def _to_scan(x, lanes_bh):
    B, T, H, C = x.shape
    if lanes_bh:
        return x.transpose(1, 3, 0, 2).reshape(1, T, C, B * H)
    return x.transpose(2, 1, 3, 0)


def _from_scan(y, B, H, lanes_bh):
    G, T, C, L = y.shape
    if lanes_bh:
        return y.reshape(T, C, B, H).transpose(2, 0, 3, 1)
    return y.transpose(3, 1, 0, 2)


def _state_to_scan(s, lanes_bh):
    B, H, dk, dv = s.shape
    if lanes_bh:
        return s.transpose(2, 3, 0, 1).reshape(1, dk, dv, B * H)
    return s.transpose(1, 2, 3, 0)


def _state_from_scan(s, B, H, lanes_bh):
    G, dk, dv, L = s.shape
    if lanes_bh:
        return s.reshape(dk, dv, B, H).transpose(2, 3, 0, 1)
    return s.transpose(3, 0, 1, 2)


def _l2norm(x):
    return x * lax.rsqrt(jnp.sum(x * x, axis=-1, keepdims=True) + 1e-12)


def _alibi_slopes(n):
    return jnp.asarray([2.0 ** (-8.0 * (h + 1) / n) for h in range(n)], dtype=F32)


def _rope_tables(pos):
    half = MLA_ROPE // 2
    inv = ROPE_THETA ** (-jnp.arange(half, dtype=F32) / half)
    ang = pos.astype(F32)[:, None] * inv[None, :]
    return jnp.cos(ang), jnp.sin(ang)


def _apply_rope(x, cos, sin):
    x1, x2 = jnp.split(x, 2, axis=-1)
    return jnp.concatenate([x1 * cos - x2 * sin, x1 * sin + x2 * cos], axis=-1)


def _rwkv_mix(p, prev, S0, prm, lanes_bh):
    mu, w0, w2, a0, a2, g2, k_k, k_a, r_k, lnx_g, lnx_b = prm
    B, T, _ = p.shape
    H = RW_HEADS
    p_prev = jnp.concatenate([prev[:, None, :], p[:, :-1]], axis=1)
    pm = p + (p_prev - p) * mu
    cuts = [int(c) for c in np.cumsum(RW_SPLITS)[:-1]]
    r, wl, k, v, al, gl = jnp.split(pm, cuts, axis=-1)
    w_log = -jax.nn.softplus(-(w0 + jnp.tanh(wl) @ w2)) - 0.5
    decay = jnp.exp(-jnp.exp(w_log))
    a = jax.nn.sigmoid(a0 + al @ a2)
    g = jax.nn.sigmoid(gl) @ g2
    heads = lambda t: t.reshape(B, T, H, HEAD_DIM)
    kk = _l2norm(heads(k * k_k))
    k = k * (1.0 + (a - 1.0) * k_a)
    r_h, k_h, v_h, w_h, a_h = heads(r), heads(k), heads(v), heads(decay), heads(a)
    aa, bb = -kk, kk * a_h
    ts = lambda t: _to_scan(t, lanes_bh)
    s0 = _state_to_scan(jnp.swapaxes(S0, -1, -2), lanes_bh)
    y, S = scan_call(ts(r_h), ts(w_h), ts(k_h), ts(v_h), ts(aa), ts(bb), s0, "layer")
    y = _from_scan(y, B, H, lanes_bh).reshape(B, T, RW_WIDTH) * lnx_g + lnx_b
    S = jnp.swapaxes(_state_from_scan(S, B, H, lanes_bh), -1, -2)
    bonus = jnp.sum(r_h * k_h * r_k, axis=-1, keepdims=True) * v_h
    y = (y + bonus.reshape(B, T, RW_WIDTH)) * g
    return y, S, p[:, -1]


def _moba_prompt(q, k, v):
    B, T = q.shape[:2]
    nb = T // MB_BLOCK
    slopes = _alibi_slopes(MB_HEADS)
    scale = HEAD_DIM ** -0.5
    kmean = k.reshape(B, nb, MB_BLOCK, MB_KV_HEADS, HEAD_DIM).mean(axis=2)
    qg = q.reshape(B, T, MB_KV_HEADS, MB_GROUP, HEAD_DIM)
    gate = jnp.einsum('btgud,bjgd->bgutj', qg, kmean)
    qblk = jnp.arange(T) // MB_BLOCK
    valid = jnp.arange(nb)[None, :] < qblk[:, None]
    top_s, top_i = lax.top_k(jnp.where(valid, gate, -jnp.inf), MB_TOPK)
    top_ok = top_s > -jnp.inf
    sel = jnp.any(top_ok[..., None] & (top_i[..., None] == jnp.arange(nb)), axis=-2)
    sel = sel | (jnp.arange(nb)[None, :] == qblk[:, None])
    s = jnp.einsum('btgud,bsgd->bguts', qg, k) * scale
    dist = jnp.arange(T)[:, None] - jnp.arange(T)[None, :]
    s = s - slopes.reshape(MB_KV_HEADS, MB_GROUP, 1, 1) * dist.astype(F32)
    mask = jnp.repeat(sel, MB_BLOCK, axis=-1) & (dist >= 0)
    pr = jax.nn.softmax(jnp.where(mask, s, -jnp.inf), axis=-1)
    o = jnp.einsum('bguts,bsgd->btgud', pr, v)
    return o.reshape(B, T, MB_HEADS * HEAD_DIM)


def _moba_sample(q, k_new, v_new, pool_k, pool_v, page_table):
    B, S = q.shape[:2]
    n_pages = page_table.shape[1]
    past = n_pages * PAGE_SIZE
    nb = past // MB_BLOCK
    assert past % MB_BLOCK == 0 and nb >= MB_TOPK
    slopes = _alibi_slopes(MB_HEADS).reshape(MB_KV_HEADS, MB_GROUP, 1, 1)
    scale = HEAD_DIM ** -0.5
    k_past = pool_k[page_table].reshape(B, past, MB_KV_HEADS, HEAD_DIM)
    v_past = pool_v[page_table].reshape(B, past, MB_KV_HEADS, HEAD_DIM)
    kmean = k_past.reshape(B, nb, MB_BLOCK, MB_KV_HEADS, HEAD_DIM).mean(axis=2)
    qg = q.reshape(B, S, MB_KV_HEADS, MB_GROUP, HEAD_DIM)
    gate = jnp.einsum('bsgud,bjgd->bgusj', qg, kmean)
    top_s, top_i = lax.top_k(gate, MB_TOPK)
    sel = jnp.any(top_i[..., None] == jnp.arange(nb), axis=-2)
    qpos = past + jnp.arange(S)
    s_past = jnp.einsum('bsgud,bkgd->bgusk', qg, k_past) * scale
    s_past = s_past - slopes * (qpos[:, None] - jnp.arange(past)[None, :]).astype(F32)
    s_past = jnp.where(jnp.repeat(sel, MB_BLOCK, axis=-1), s_past, -jnp.inf)
    d_own = qpos[:, None] - qpos[None, :]
    s_own = jnp.einsum('bsgud,bkgd->bgusk', qg, k_new) * scale - slopes * d_own.astype(F32)
    s_own = jnp.where(d_own >= 0, s_own, -jnp.inf)
    pr = jax.nn.softmax(jnp.concatenate([s_past, s_own], axis=-1), axis=-1)
    o = (jnp.einsum('bgusk,bkgd->bsgud', pr[..., :past], v_past)
         + jnp.einsum('bgusk,bkgd->bsgud', pr[..., past:], v_new))
    return o.reshape(B, S, MB_HEADS * HEAD_DIM)


def _mla_attn(q_lat, q_rope, c_all, r_all, qpos):
    scale = (MLA_NOPE + MLA_ROPE) ** -0.5
    K = c_all.shape[1]
    s = (jnp.einsum('bqhc,bkc->bhqk', q_lat, c_all) + jnp.einsum('bqhr,bkr->bhqk', q_rope, r_all)) * scale
    mask = jnp.arange(K)[None, :] <= qpos[:, None]
    pr = jax.nn.softmax(jnp.where(mask, s, -jnp.inf), axis=-1)
    return jnp.einsum('bhqk,bkc->bqhc', pr, c_all)


def _gdn_mix(qkv, z, a_raw, b_raw, conv_buf, S0, prm, lanes_bh):
    conv_w, a_log, dt_bias, norm_g = prm
    B, T, _ = qkv.shape
    H = GD_HEADS
    xx = jnp.concatenate([conv_buf, qkv], axis=1)
    y = sum(xx[:, j:j + T] * conv_w[j] for j in range(GD_CONV))
    y = _silu(y)
    q, k, v = jnp.split(y, [H * GD_DK, 2 * H * GD_DK], axis=-1)
    q = _l2norm(q.reshape(B, T, H, GD_DK)) * (GD_DK ** -0.5)
    k = _l2norm(k.reshape(B, T, H, GD_DK))
    v = v.reshape(B, T, H, GD_DV)
    beta = jax.nn.sigmoid(b_raw)[..., None]
    alpha = jnp.exp(-jnp.exp(a_log) * jax.nn.softplus(a_raw + dt_bias))[..., None]
    ts = lambda t: _to_scan(t, lanes_bh)
    w = jnp.broadcast_to(alpha, k.shape)
    y, S = scan_call(ts(q), ts(w), ts(k), ts(beta * v), ts(k), ts(-beta * alpha * k),
                     _state_to_scan(S0, lanes_bh), "rms")
    o = _from_scan(y, B, H, lanes_bh) * norm_g * _silu(z.reshape(B, T, H, GD_DV))
    S = _state_from_scan(S, B, H, lanes_bh)
    return o.reshape(B, T, H * GD_DV), S, xx[:, -(GD_CONV - 1):]


def _bdiag(w):
    H, a, b = w.shape
    eye = jnp.eye(H, dtype=w.dtype)
    return (eye[:, None, :, None] * w[:, :, None, :]).reshape(H * a, H * b)


def kernel(x_prompt, x_sample, cache_moba_k, cache_moba_v, cache_mla_ckv, cache_mla_krope, state_rwkv_wkv, state_rwkv_shift, state_gdn, state_gdn_conv, page_table, c_prompt, c_sample, w_ada, b_ada, g_mix_pre, g_mix_post, g_ff_pre, g_ff_post, w_in_even, w_out_even, rw_mu, rw_w0, rw_w2, rw_a0, rw_a2, rw_g2, rw_kk, rw_ka, rw_rk, rw_lnx_g, rw_lnx_b, ffn_w_gate, ffn_w_up, ffn_w_down, w_in_odd, w_out_odd, mla_q_norm, mla_w_q_up, mla_kv_norm, mla_w_uk, mla_w_uv, gdn_conv_w, gdn_a_log, gdn_dt_bias, gdn_norm_g, moe_w_router, moe_b_router, moe_w_gate, moe_w_up, moe_w_down):
    Bp, Tp, D = x_prompt.shape
    Bs, Ts, _ = x_sample.shape
    past = page_table.shape[1] * PAGE_SIZE
    Ms = Bs * Ts
    row = lambda v: v.reshape(1, -1)

    xp = x_prompt
    xs = x_sample.reshape(1, Ms, D)
    c_all = jnp.concatenate([c_prompt, c_sample], axis=0)

    outs_p, outs_s = {}, {}
    for layer in range(w_ada.shape[0]):
        i = layer // 2
        mod = adaln_call(c_all, w_ada[layer].astype(BF16), row(b_ada[layer]))
        mp = [m.reshape(Bp, 1, D) for m in jnp.split(mod[:Bp], 6, axis=-1)]
        ms = [jnp.broadcast_to(m[:, None, :], (Bs, Ts, D)).reshape(1, Ms, D)
              for m in jnp.split(mod[Bp:], 6, axis=-1)]
        gpre, gpost = row(g_mix_pre[layer]), row(g_mix_post[layer])
        if layer % 2 == 0:
            w_in = w_in_even[i].astype(BF16)
            splits = (RW_PROJ, MB_HEADS * HEAD_DIM, MB_KV_HEADS * HEAD_DIM, MB_KV_HEADS * HEAD_DIM)
            rw_prm = (rw_mu[i], rw_w0[i], rw_w2[i], rw_a0[i], rw_a2[i], rw_g2[i], rw_kk[i], rw_ka[i],
                      rw_rk[i], rw_lnx_g[i], rw_lnx_b[i])
            w_out = w_out_even[i].astype(BF16)
            p_rw, q, k, v = modmm_call(xp, mp[0], mp[1], gpre, w_in, splits)
            y_rw, S_p, shift_p = _rwkv_mix(p_rw, jnp.zeros((Bp, RW_PROJ), F32),
                                           jnp.zeros((Bp, RW_HEADS, HEAD_DIM, HEAD_DIM), F32), rw_prm, True)
            k4 = k.reshape(Bp, Tp, MB_KV_HEADS, HEAD_DIM)
            v4 = v.reshape(Bp, Tp, MB_KV_HEADS, HEAD_DIM)
            y_mb = _moba_prompt(q.reshape(Bp, Tp, MB_HEADS, HEAD_DIM), k4, v4)
            xp = outproj_call(y_rw, y_mb, w_out[:RW_WIDTH], w_out[RW_WIDTH:], xp, mp[2], gpost)
            outs_p.update(mk=k4[None], mv=v4[None], wkv=S_p[None], shift=shift_p[None])
            p_rw, q, k, v = modmm_call(xs, ms[0], ms[1], gpre, w_in, splits)
            y_rw, S_s, shift_s = _rwkv_mix(p_rw.reshape(Bs, Ts, RW_PROJ), state_rwkv_shift[i],
                                           state_rwkv_wkv[i], rw_prm, False)
            k4 = k.reshape(Bs, Ts, MB_KV_HEADS, HEAD_DIM)
            v4 = v.reshape(Bs, Ts, MB_KV_HEADS, HEAD_DIM)
            y_mb = _moba_sample(q.reshape(Bs, Ts, MB_HEADS, HEAD_DIM), k4, v4,
                                cache_moba_k[i], cache_moba_v[i], page_table)
            xs = outproj_call(y_rw.reshape(1, Ms, RW_WIDTH), y_mb.reshape(1, Ms, -1),
                              w_out[:RW_WIDTH], w_out[RW_WIDTH:], xs, ms[2], gpost)
            outs_s.update(mk=k4[None], mv=v4[None], wkv=S_s[None], shift=shift_s[None])
        else:
            w_in = w_in_odd[i]
            cuts = [int(c) for c in np.cumsum(ODD_SPLITS)[:-1]]
            w_cq, w_ckv, w_kr, w_qkv, w_z, w_a, w_b = jnp.split(w_in, cuts, axis=1)
            padc = lambda w, n: jnp.pad(w, ((0, 0), (0, n - w.shape[1])))
            w_in_p = jnp.concatenate([w_cq, w_ckv, padc(w_kr, LANES), w_qkv, w_z,
                                      padc(jnp.concatenate([w_a, w_b], axis=1), LANES)], axis=1).astype(BF16)
            splits = (MLA_Q_LORA, MLA_KV_LORA, LANES, GD_QKV, GD_HEADS * GD_DV, LANES)
            w_out = w_out_odd[i].astype(BF16)
            gd_prm = (gdn_conv_w[i], gdn_a_log[i], gdn_dt_bias[i], gdn_norm_g[i])
            w_uv_bd = _bdiag(mla_w_uv[i].transpose(1, 0, 2))

            def odd(x, m, B, T, pos, conv_buf, S0, lanes_bh, mla_kv):
                cq, ckv, kr, qkv, z, ab = modmm_call(x, m[0], m[1], gpre, w_in_p, splits)
                cq, ckv, qkv, z = (t.reshape(B, T, -1) for t in (cq, ckv, qkv, z))
                kr = kr.reshape(B, T, LANES)[..., :MLA_ROPE]
                ab = ab.reshape(B, T, LANES)
                a_raw, b_raw = ab[..., :GD_HEADS], ab[..., GD_HEADS:2 * GD_HEADS]
                cos, sin = _rope_tables(pos)
                qf = (_rms(cq) * mla_q_norm[i]) @ mla_w_q_up[i]
                qf = qf.reshape(B, T, MLA_HEADS, MLA_NOPE + MLA_ROPE)
                q_rope = _apply_rope(qf[..., MLA_NOPE:], cos[:, None, :], sin[:, None, :])
                q_lat = jnp.einsum('bthn,chn->bthc', qf[..., :MLA_NOPE], mla_w_uk[i])
                ckv_n = _rms(ckv) * mla_kv_norm[i]
                k_rope = _apply_rope(kr, cos, sin)
                c_keys, r_keys = mla_kv(ckv_n, k_rope)
                o_lat = _mla_attn(q_lat, q_rope, c_keys, r_keys, pos)
                y_gdn, S, buf = _gdn_mix(qkv, z, a_raw, b_raw, conv_buf, S0, gd_prm, lanes_bh)
                return o_lat.reshape(B, T, -1), y_gdn, ckv_n, k_rope, S, buf

            o_lat, y_gdn, ckv_n, k_rope, S_p, buf_p = odd(
                xp, mp, Bp, Tp, jnp.arange(Tp), jnp.zeros((Bp, GD_CONV - 1, GD_QKV), F32),
                jnp.zeros((Bp, GD_HEADS, GD_DK, GD_DV), F32), True, lambda c, r: (c, r))
            y_mla = (o_lat @ w_uv_bd)
            xp = outproj_call(y_mla, y_gdn, w_out[:MLA_HEADS * MLA_V], w_out[MLA_HEADS * MLA_V:], xp, mp[2], gpost)
            outs_p.update(ckv=ckv_n[None], krope=k_rope[None], gdn=S_p[None], conv=buf_p[None])
            pc = cache_mla_ckv[i][page_table].reshape(Bs, past, MLA_KV_LORA)
            pr_ = cache_mla_krope[i][page_table].reshape(Bs, past, MLA_ROPE)
            o_lat, y_gdn, ckv_n, k_rope, S_s, buf_s = odd(
                xs, ms, Bs, Ts, past + jnp.arange(Ts), state_gdn_conv[i], state_gdn[i], False,
                lambda c, r: (jnp.concatenate([pc, c], axis=1), jnp.concatenate([pr_, r], axis=1)))
            y_mla = (o_lat @ w_uv_bd)
            xs = outproj_call(y_mla.reshape(1, Ms, -1), y_gdn.reshape(1, Ms, -1),
                              w_out[:MLA_HEADS * MLA_V], w_out[MLA_HEADS * MLA_V:], xs, ms[2], gpost)
            outs_s.update(ckv=ckv_n[None], krope=k_rope[None], gdn=S_s[None], conv=buf_s[None])

        gpre, gpost = row(g_ff_pre[layer]), row(g_ff_post[layer])
        if layer % 2 == 0:
            J = 2
            Fd = ffn_w_gate.shape[2] // J
            wg = ffn_w_gate[i].astype(BF16).reshape(D, J, Fd).transpose(1, 0, 2)
            wu = ffn_w_up[i].astype(BF16).reshape(D, J, Fd).transpose(1, 0, 2)
            wd = ffn_w_down[i].astype(BF16).reshape(J, Fd, D)
            xp = ffn_call(xp, mp[3], mp[4], mp[5], gpre, gpost, wg, wu, wd)
            xs = ffn_call(xs, ms[3], ms[4], ms[5], gpre, gpost, wg, wu, wd)
        else:
            wr = jnp.pad(moe_w_router[i], ((0, 0), (0, LANES - N_EXPERTS)))
            br = jnp.pad(row(moe_b_router[i]), ((0, 0), (0, LANES - N_EXPERTS)))
            wg, wu, wd = (w[i].astype(BF16) for w in (moe_w_gate, moe_w_up, moe_w_down))
            xp = ffn_call(xp, mp[3], mp[4], mp[5], gpre, gpost, wg, wu, wd, wr, br)
            xs = ffn_call(xs, ms[3], ms[4], ms[5], gpre, gpost, wg, wu, wd, wr, br)

    names = ('mk', 'mv', 'wkv', 'shift', 'ckv', 'krope', 'gdn', 'conv')
    return ((xp, xs.reshape(Bs, Ts, D)) + tuple(outs_p[n] for n in names)
            + tuple(outs_s[n] for n in names))
```

```python
import functools
import math

import jax
import jax.numpy as jnp
import numpy as np
from jax import lax
from jax.experimental import pallas as pl
from jax.experimental.pallas import tpu as pltpu

F32 = jnp.float32
BF16 = jnp.bfloat16

NORM_EPS = 1e-6
HEAD_DIM = 64
PAGE_SIZE = 128
RW_HEADS = 8
RW_WIDTH = RW_HEADS * HEAD_DIM
RW_SPLITS = (RW_WIDTH, 64, RW_WIDTH, RW_WIDTH, 64, 128)
RW_PROJ = sum(RW_SPLITS)
RW_LNX_EPS = 64e-5
MB_HEADS = 8
MB_KV_HEADS = 4
MB_GROUP = MB_HEADS // MB_KV_HEADS
MB_BLOCK = 256
MB_TOPK = 3
MLA_HEADS = 8
MLA_NOPE = 64
MLA_ROPE = 32
MLA_V = 64
MLA_Q_LORA = 256
MLA_KV_LORA = 128
ROPE_THETA = 10000.0
GD_HEADS = 8
GD_DK = 64
GD_DV = 64
GD_CONV = 4
GD_QKV = GD_HEADS * (2 * GD_DK + GD_DV)
ODD_SPLITS = (MLA_Q_LORA, MLA_KV_LORA, MLA_ROPE, GD_QKV, GD_HEADS * GD_DV, GD_HEADS, GD_HEADS)
N_EXPERTS = 8

LANES = 128
VMEM_LIMIT = 56 * 1024 * 1024


def _cparams(sem):
    return pltpu.CompilerParams(dimension_semantics=sem, vmem_limit_bytes=VMEM_LIMIT)


def _silu(x):
    return x * jax.nn.sigmoid(x)


def _rms(x, eps=NORM_EPS):
    return x * lax.rsqrt(jnp.mean(x * x, axis=-1, keepdims=True) + eps)


def _adaln_body(c_ref, w_ref, b_ref, o_ref):
    c = c_ref[...]
    o_ref[...] = jnp.dot(_silu(c).astype(BF16), w_ref[...], preferred_element_type=F32) + b_ref[...]


def adaln_call(c, w, b, tn=1536):
    R, D = c.shape
    N = w.shape[1]
    return pl.pallas_call(
        _adaln_body,
        grid=(N // tn,),
        in_specs=[pl.BlockSpec((R, D), lambda j: (0, 0)),
                  pl.BlockSpec((D, tn), lambda j: (0, j)),
                  pl.BlockSpec((1, tn), lambda j: (0, j))],
        out_specs=pl.BlockSpec((R, tn), lambda j: (0, j)),
        out_shape=jax.ShapeDtypeStruct((R, N), F32),
        compiler_params=_cparams(("arbitrary",)),
        name="adaln",
    )(c, w, b)


def _mod_spec(mod, tm):
    if mod.shape[1] == 1:
        return pl.BlockSpec((None, 1, mod.shape[2]), lambda g, i, *_: (g, 0, 0))
    return pl.BlockSpec((None, tm, mod.shape[2]), lambda g, i, *_: (g, i, 0))


def _modmm_body(x_ref, sh_ref, sc_ref, g_ref, w_ref, *o_refs, splits):
    h = _rms(x_ref[...]) * g_ref[...] * (1.0 + sc_ref[...]) + sh_ref[...]
    y = jnp.dot(h.astype(BF16), w_ref[...], preferred_element_type=F32)
    off = 0
    for o_ref, n in zip(o_refs, splits):
        o_ref[...] = y[:, off:off + n].astype(o_ref.dtype)
        off += n


def modmm_call(x, shift, scale, gain, w, splits, tm=512):
    G, R, D = x.shape
    tm = min(tm, R)
    N = w.shape[1]
    assert sum(splits) == N and all(s % LANES == 0 for s in splits)
    return pl.pallas_call(
        functools.partial(_modmm_body, splits=splits),
        grid=(G, R // tm),
        in_specs=[pl.BlockSpec((None, tm, D), lambda g, i: (g, i, 0)),
                  _mod_spec(shift, tm), _mod_spec(scale, tm),
                  pl.BlockSpec((1, D), lambda g, i: (0, 0)),
                  pl.BlockSpec((D, N), lambda g, i: (0, 0))],
        out_specs=[pl.BlockSpec((None, tm, n), lambda g, i: (g, i, 0)) for n in splits],
        out_shape=[jax.ShapeDtypeStruct((G, R, n), F32) for n in splits],
        compiler_params=_cparams(("arbitrary", "arbitrary")),
        name="modmm",
    )(x, shift, scale, gain, w)


def _outproj_body(ya_ref, yb_ref, wpre_ref, wa_ref, wb_ref, x_ref, ga_ref, gp_ref, o_ref, *, pre):
    ya = ya_ref[...].astype(BF16)
    if pre:
        ya = jnp.dot(ya, wpre_ref[...], preferred_element_type=F32).astype(BF16)
    y = jnp.dot(ya, wa_ref[...], preferred_element_type=F32)
    y = y + jnp.dot(yb_ref[...].astype(BF16), wb_ref[...], preferred_element_type=F32)
    o_ref[...] = x_ref[...] + ga_ref[...] * (_rms(y) * gp_ref[...])


def outproj_call(ya, yb, wa, wb, x, gate, gpost, wpre=None, tm=512):
    G, R, D = x.shape
    tm = min(tm, R)
    Ka, Kb = ya.shape[2], yb.shape[2]
    pre = wpre is not None
    if not pre:
        wpre = jnp.zeros((8, LANES), BF16)
    return pl.pallas_call(
        functools.partial(_outproj_body, pre=pre),
        grid=(G, R // tm),
        in_specs=[pl.BlockSpec((None, tm, Ka), lambda g, i: (g, i, 0)),
                  pl.BlockSpec((None, tm, Kb), lambda g, i: (g, i, 0)),
                  pl.BlockSpec(wpre.shape, lambda g, i: (0, 0)),
                  pl.BlockSpec(wa.shape, lambda g, i: (0, 0)),
                  pl.BlockSpec(wb.shape, lambda g, i: (0, 0)),
                  pl.BlockSpec((None, tm, D), lambda g, i: (g, i, 0)),
                  _mod_spec(gate, tm),
                  pl.BlockSpec((1, D), lambda g, i: (0, 0))],
        out_specs=pl.BlockSpec((None, tm, D), lambda g, i: (g, i, 0)),
        out_shape=jax.ShapeDtypeStruct((G, R, D), F32),
        compiler_params=_cparams(("arbitrary", "arbitrary")),
        name="outproj",
    )(ya, yb, wpre, wa, wb, x, gate, gpost)


def _mla_prep_body(cq_ref, ckv_ref, kr_ref, cos_ref, sin_ref, qg_ref, kg_ref, wqn_ref, wqr_ref, wuk_ref,
                   qcat_ref, kcat_ref, ckvn_ref, krope_ref):
    cosf, sinf = cos_ref[...], sin_ref[...]
    lane = lax.broadcasted_iota(jnp.int32, cosf.shape, 1)
    half = MLA_ROPE // 2

    def rope(x):
        rot = jnp.where(lane < half, pltpu.roll(x, LANES - half, 1), pltpu.roll(x, half, 1))
        return x * cosf + rot * sinf

    cqn = (_rms(cq_ref[...]) * qg_ref[...]).astype(BF16)
    qn = jnp.dot(cqn, wqn_ref[...], preferred_element_type=F32).astype(BF16)
    qr = jnp.dot(cqn, wqr_ref[...], preferred_element_type=F32)
    ql = jnp.dot(qn, wuk_ref[...], preferred_element_type=F32)
    for h in range(MLA_HEADS):
        qcat_ref[:, 2 * h * LANES:(2 * h + 1) * LANES] = ql[:, h * LANES:(h + 1) * LANES].astype(BF16)
        qcat_ref[:, (2 * h + 1) * LANES:(2 * h + 2) * LANES] = rope(qr[:, h * LANES:(h + 1) * LANES]).astype(BF16)
    ckvn = _rms(ckv_ref[...]) * kg_ref[...]
    krope = rope(kr_ref[...])
    ckvn_ref[...] = ckvn
    krope_ref[...] = krope
    kcat_ref[:, :LANES] = ckvn.astype(BF16)
    kcat_ref[:, LANES:] = krope.astype(BF16)


def mla_prep_call(cq, ckv, kr, cosf, sinf, q_gain, kv_gain, w_qn, w_qr, w_uk_bd, tm=512):
    G, R, _ = cq.shape
    tm = min(tm, R)
    rows = lambda n: pl.BlockSpec((None, tm, n), lambda g, i: (g, i, 0))
    const = lambda a: pl.BlockSpec(a.shape, lambda g, i: (0, 0))
    W = 2 * LANES * MLA_HEADS
    return pl.pallas_call(
        _mla_prep_body,
        grid=(G, R // tm),
        in_specs=[rows(cq.shape[2]), rows(LANES), rows(LANES),
                  pl.BlockSpec((tm, LANES), lambda g, i: (i, 0)), pl.BlockSpec((tm, LANES), lambda g, i: (i, 0)),
                  const(q_gain), const(kv_gain), const(w_qn), const(w_qr), const(w_uk_bd)],
        out_specs=[rows(W), rows(2 * LANES), rows(LANES), rows(LANES)],
        out_shape=[jax.ShapeDtypeStruct((G, R, W), BF16), jax.ShapeDtypeStruct((G, R, 2 * LANES), BF16),
                   jax.ShapeDtypeStruct((G, R, LANES), F32), jax.ShapeDtypeStruct((G, R, LANES), F32)],
        compiler_params=_cparams(("arbitrary", "arbitrary")),
        name="mla_prep",
    )(cq, ckv, kr, cosf, sinf, q_gain, kv_gain, w_qn, w_qr, w_uk_bd)


def _rope_rows(pos):
    cos, sin = _rope_tables(pos)
    z = jnp.zeros((pos.shape[0], LANES - MLA_ROPE), F32)
    return jnp.concatenate([cos, cos, z], axis=1), jnp.concatenate([-sin, sin, z], axis=1)


def _ffn_body(x_ref, sh_ref, sc_ref, ga_ref, gpre_ref, gpost_ref, wr_ref, br_ref,
              wg_ref, wu_ref, wd_ref, o_ref, h_ref, acc_ref, rw_ref, *, moe):
    j = pl.program_id(2)

    @pl.when(j == 0)
    def _():
        h = _rms(x_ref[...]) * gpre_ref[...] * (1.0 + sc_ref[...]) + sh_ref[...]
        h_ref[...] = h.astype(BF16)
        acc_ref[...] = jnp.zeros_like(acc_ref)
        if moe:
            logits = jnp.dot(h, wr_ref[...], preferred_element_type=F32,
                             precision=lax.Precision.HIGHEST) + br_ref[...]
            lane = lax.broadcasted_iota(jnp.int32, logits.shape, 1)
            neg = jnp.float32(-jnp.inf)
            logits = jnp.where(lane < N_EXPERTS, logits, neg)
            m1 = jnp.max(logits, axis=-1, keepdims=True)
            i1 = jnp.min(jnp.where(logits == m1, lane, LANES), axis=-1, keepdims=True)
            rest = jnp.where(lane == i1, neg, logits)
            m2 = jnp.max(rest, axis=-1, keepdims=True)
            i2 = jnp.min(jnp.where(rest == m2, lane, LANES), axis=-1, keepdims=True)
            e2 = jnp.exp(m2 - m1)
            w1 = 1.0 / (1.0 + e2)
            w2 = e2 / (1.0 + e2)
            rw_ref[...] = jnp.where(lane == i1, w1, 0.0) + jnp.where(lane == i2, w2, 0.0)

    hb = h_ref[...]
    g = jnp.dot(hb, wg_ref[...], preferred_element_type=F32)
    u = jnp.dot(hb, wu_ref[...], preferred_element_type=F32)
    f = jnp.dot((_silu(g) * u).astype(BF16), wd_ref[...], preferred_element_type=F32)
    if moe:
        lane = lax.broadcasted_iota(jnp.int32, rw_ref.shape, 1)
        we = jnp.sum(jnp.where(lane == j, rw_ref[...], 0.0), axis=-1, keepdims=True)
        f = f * we
    acc_ref[...] += f

    @pl.when(j == pl.num_programs(2) - 1)
    def _():
        o_ref[...] = x_ref[...] + ga_ref[...] * (_rms(acc_ref[...]) * gpost_ref[...])


def ffn_call(x, shift, scale, gate, gpre, gpost, wg, wu, wd, w_router=None, b_router=None, tm=512):
    G, R, D = x.shape
    tm = min(tm, R)
    J, _, Fd = wg.shape
    moe = w_router is not None
    if not moe:
        w_router = jnp.zeros((D, LANES), F32)
        b_router = jnp.zeros((1, LANES), F32)
    return pl.pallas_call(
        functools.partial(_ffn_body, moe=moe),
        grid=(G, R // tm, J),
        in_specs=[pl.BlockSpec((None, tm, D), lambda g, i, j: (g, i, 0)),
                  _mod_spec(shift, tm), _mod_spec(scale, tm), _mod_spec(gate, tm),
                  pl.BlockSpec((1, D), lambda g, i, j: (0, 0)),
                  pl.BlockSpec((1, D), lambda g, i, j: (0, 0)),
                  pl.BlockSpec((D, LANES), lambda g, i, j: (0, 0)),
                  pl.BlockSpec((1, LANES), lambda g, i, j: (0, 0)),
                  pl.BlockSpec((None, D, Fd), lambda g, i, j: (j, 0, 0)),
                  pl.BlockSpec((None, D, Fd), lambda g, i, j: (j, 0, 0)),
                  pl.BlockSpec((None, Fd, D), lambda g, i, j: (j, 0, 0))],
        out_specs=pl.BlockSpec((None, tm, D), lambda g, i, j: (g, i, 0)),
        out_shape=jax.ShapeDtypeStruct((G, R, D), F32),
        scratch_shapes=[pltpu.VMEM((tm, D), BF16), pltpu.VMEM((tm, D), F32),
                        pltpu.VMEM((tm, LANES), F32)],
        compiler_params=_cparams(("arbitrary", "arbitrary", "arbitrary")),
        name="moe" if moe else "ffn",
    )(x, shift, scale, gate, gpre, gpost, w_router, b_router, wg, wu, wd)


def _scan_body(r_ref, w_ref, k_ref, v_ref, a_ref, b_ref, an_ref, s0_ref, y_ref, s_ref, sa_ref,
               *, tt, dk, norm):
    i = pl.program_id(1)

    @pl.when(i == 0)
    def _():
        s_ref[...] = s0_ref[...]
        sa = jnp.zeros(sa_ref.shape, F32)
        for k in range(dk):
            sa = sa + s0_ref[k] * a_ref[0, k:k + 1, :]
        sa_ref[...] = sa

    def step(t, sa):
        v_t = v_ref[t]
        tn = jnp.minimum(t + 1, tt - 1)
        last = t == tt - 1
        y = jnp.zeros(sa.shape, F32)
        sa_n = jnp.zeros(sa.shape, F32)
        for k in range(dk):
            new = (s_ref[k] * w_ref[t, k:k + 1, :] + sa * b_ref[t, k:k + 1, :]
                   + v_t * k_ref[t, k:k + 1, :])
            s_ref[k] = new
            y = y + new * r_ref[t, k:k + 1, :]
            a_next = jnp.where(last, an_ref[0, k:k + 1, :], a_ref[tn, k:k + 1, :])
            sa_n = sa_n + new * a_next
        if norm == "layer":
            mu = jnp.mean(y, axis=0, keepdims=True)
            d = y - mu
            y = d * lax.rsqrt(jnp.mean(d * d, axis=0, keepdims=True) + RW_LNX_EPS)
        else:
            y = y * lax.rsqrt(jnp.mean(y * y, axis=0, keepdims=True) + NORM_EPS)
        y_ref[t] = y
        return sa_n

    sa_ref[...] = lax.fori_loop(0, tt, step, sa_ref[...])


def scan_call(r, w, k, v, a, b, s0, norm, tt=32):
    G, T, dk, L = r.shape
    dv = v.shape[2]
    tt = min(tt, T)
    nt = T // tt
    tspec = lambda d: pl.BlockSpec((None, tt, d, L), lambda g, i: (g, i, 0, 0))
    nxt = pl.BlockSpec((None, 1, dk, L), lambda g, i: (g, jnp.minimum((i + 1) * tt, T - 1), 0, 0))
    sspec = pl.BlockSpec((None, dk, dv, L), lambda g, i: (g, 0, 0, 0))
    return pl.pallas_call(
        functools.partial(_scan_body, tt=tt, dk=dk, norm=norm),
        grid=(G, nt),
        in_specs=[tspec(dk), tspec(dk), tspec(dk), tspec(dv), tspec(dk), tspec(dk), nxt, sspec],
        out_specs=[tspec(dv), sspec],
        out_shape=[jax.ShapeDtypeStruct((G, T, dv, L), F32),
                   jax.ShapeDtypeStruct((G, dk, dv, L), F32)],
        scratch_shapes=[pltpu.VMEM((dv, L), F32)],
        compiler_params=_cparams(("arbitrary", "arbitrary")),
        name="scan",
    )(r, w, k, v, a, b, a, s0)


def _to_scan(x, lanes_bh):
    B, T, H, C = x.shape
    if lanes_bh:
        return x.transpose(1, 3, 0, 2).reshape(1, T, C, B * H)
    return x.transpose(2, 1, 3, 0)


def _from_scan(y, B, H, lanes_bh):
    G, T, C, L = y.shape
    if lanes_bh:
        return y.reshape(T, C, B, H).transpose(2, 0, 3, 1)
    return y.transpose(3, 1, 0, 2)


def _state_to_scan(s, lanes_bh):
    B, H, dk, dv = s.shape
    if lanes_bh:
        return s.transpose(2, 3, 0, 1).reshape(1, dk, dv, B * H)
    return s.transpose(1, 2, 3, 0)


def _state_from_scan(s, B, H, lanes_bh):
    G, dk, dv, L = s.shape
    if lanes_bh:
        return s.reshape(dk, dv, B, H).transpose(2, 3, 0, 1)
    return s.transpose(3, 0, 1, 2)


NEG = -1e30


def _online_softmax_update(s, v, m_ref, l_ref, acc_ref):
    m_prev = m_ref[...]
    m_new = jnp.maximum(m_prev, jnp.max(s, axis=-1, keepdims=True))
    alpha = jnp.exp(m_prev - m_new)
    p = jnp.exp(s - m_new)
    l_ref[...] = alpha * l_ref[...] + jnp.sum(p, axis=-1, keepdims=True)
    acc_ref[...] = alpha * acc_ref[...] + jnp.dot(p.astype(BF16), v, preferred_element_type=F32)
    m_ref[...] = m_new


_NT = (((1,), (1,)), ((), ()))


def _mla_prompt_body(qi_ref, kj_ref, q_ref, k_ref, o_ref, m_ref, l_ref, acc_ref, *, tq, tk, scale):
    p = pl.program_id(1)
    qi, kj = qi_ref[p], kj_ref[p]

    @pl.when(kj == 0)
    def _():
        m_ref[...] = jnp.full(m_ref.shape, NEG, F32)
        l_ref[...] = jnp.zeros(l_ref.shape, F32)
        acc_ref[...] = jnp.zeros(acc_ref.shape, F32)

    k = k_ref[...]
    s = lax.dot_general(q_ref[...], k, _NT, preferred_element_type=F32) * scale
    tok = qi * tq + lax.broadcasted_iota(jnp.int32, s.shape, 0) // MLA_HEADS
    kpos = kj * tk + lax.broadcasted_iota(jnp.int32, s.shape, 1)
    s = jnp.where(kpos <= tok, s, NEG)
    _online_softmax_update(s, k[:, :MLA_KV_LORA], m_ref, l_ref, acc_ref)

    @pl.when(kj == (qi * tq + tq - 1) // tk)
    def _():
        o_ref[...] = (acc_ref[...] / l_ref[...]).astype(o_ref.dtype)


def mla_prompt_call(qcat, kcat, tq=128, tk=256):
    B, TH, W = qcat.shape
    T = kcat.shape[1]
    pairs = [(i, j) for i in range(T // tq) for j in range((i * tq + tq - 1) // tk + 1)]
    qi_tab = jnp.asarray([p[0] for p in pairs], jnp.int32)
    kj_tab = jnp.asarray([p[1] for p in pairs], jnp.int32)
    R = tq * MLA_HEADS
    return pl.pallas_call(
        functools.partial(_mla_prompt_body, tq=tq, tk=tk, scale=(MLA_NOPE + MLA_ROPE) ** -0.5),
        grid_spec=pltpu.PrefetchScalarGridSpec(
            num_scalar_prefetch=2, grid=(B, len(pairs)),
            in_specs=[pl.BlockSpec((None, R, W), lambda b, p, qi, kj: (b, qi[p], 0)),
                      pl.BlockSpec((None, tk, W), lambda b, p, qi, kj: (b, kj[p], 0))],
            out_specs=pl.BlockSpec((None, R, MLA_KV_LORA), lambda b, p, qi, kj: (b, qi[p], 0)),
            scratch_shapes=[pltpu.VMEM((R, 1), F32), pltpu.VMEM((R, 1), F32),
                            pltpu.VMEM((R, MLA_KV_LORA), F32)]),
        out_shape=jax.ShapeDtypeStruct((B, TH, MLA_KV_LORA), BF16),
        compiler_params=_cparams(("arbitrary", "arbitrary")),
        name="mla_prompt",
    )(qi_tab, kj_tab, qcat, kcat)


def _mla_sample_body(pt_ref, q_ref, kn_ref, *rest, npc, scale):
    cp, rp = rest[:npc], rest[npc:2 * npc]
    o_ref, kc_ref, m_ref, l_ref, acc_ref = rest[2 * npc:]
    c = pl.program_id(1)

    @pl.when(c == 0)
    def _():
        m_ref[...] = jnp.full(m_ref.shape, NEG, F32)
        l_ref[...] = jnp.zeros(l_ref.shape, F32)
        acc_ref[...] = jnp.zeros(acc_ref.shape, F32)
        kc_ref[...] = jnp.zeros(kc_ref.shape, BF16)

    q = q_ref[...]
    for i in range(npc // 2):
        kc_ref[:, :MLA_KV_LORA] = jnp.concatenate([cp[2 * i][...], cp[2 * i + 1][...]], axis=0).astype(BF16)
        kc_ref[:, MLA_KV_LORA:MLA_KV_LORA + MLA_ROPE] = jnp.concatenate(
            [rp[2 * i][...], rp[2 * i + 1][...]], axis=0).astype(BF16)
        kc = kc_ref[...]
        s = lax.dot_general(q, kc, _NT, preferred_element_type=F32) * scale
        _online_softmax_update(s, kc[:, :MLA_KV_LORA], m_ref, l_ref, acc_ref)

    @pl.when(c == pl.num_programs(1) - 1)
    def _():
        kn = kn_ref[...]
        s = lax.dot_general(q, kn, _NT, preferred_element_type=F32) * scale
        qs = lax.broadcasted_iota(jnp.int32, s.shape, 0) // MLA_HEADS
        col = lax.broadcasted_iota(jnp.int32, s.shape, 1)
        s = jnp.where(col <= qs, s, NEG)
        _online_softmax_update(s, kn[:, :MLA_KV_LORA], m_ref, l_ref, acc_ref)
        o_ref[...] = (acc_ref[...] / l_ref[...]).astype(o_ref.dtype)


def mla_sample_call(page_table, qcat, kcat_new, pool_c, pool_r, npc=16):
    B, R, W = qcat.shape
    n_pages = page_table.shape[1]
    nch = n_pages // npc
    page = lambda j: (lambda b, c, pt: (pt[b, c * npc + j], 0, 0))
    return pl.pallas_call(
        functools.partial(_mla_sample_body, npc=npc, scale=(MLA_NOPE + MLA_ROPE) ** -0.5),
        grid_spec=pltpu.PrefetchScalarGridSpec(
            num_scalar_prefetch=1, grid=(B, nch),
            in_specs=([pl.BlockSpec((None, R, W), lambda b, c, pt: (b, 0, 0)),
                       pl.BlockSpec((None,) + kcat_new.shape[1:], lambda b, c, pt: (b, 0, 0))]
                      + [pl.BlockSpec((None, PAGE_SIZE, MLA_KV_LORA), page(j)) for j in range(npc)]
                      + [pl.BlockSpec((None, PAGE_SIZE, MLA_ROPE), page(j)) for j in range(npc)]),
            out_specs=pl.BlockSpec((None, R, MLA_KV_LORA), lambda b, c, pt: (b, 0, 0)),
            scratch_shapes=[pltpu.VMEM((2 * PAGE_SIZE, W), BF16), pltpu.VMEM((R, 1), F32),
                            pltpu.VMEM((R, 1), F32), pltpu.VMEM((R, MLA_KV_LORA), F32)]),
        out_shape=jax.ShapeDtypeStruct((B, R, MLA_KV_LORA), BF16),
        compiler_params=_cparams(("arbitrary", "arbitrary")),
        name="mla_sample",
    )(page_table, qcat, kcat_new, *([pool_c] * npc), *([pool_r] * npc))


def _kmean_body(k_ref, o_ref):
    k = k_ref[...]
    nb = k.shape[0] // MB_BLOCK
    o_ref[...] = jnp.mean(k.reshape(nb, MB_BLOCK, k.shape[1]), axis=1)


def kmean_call(k):
    B, T, W = k.shape
    nb = T // MB_BLOCK
    return pl.pallas_call(
        _kmean_body, grid=(B,),
        in_specs=[pl.BlockSpec((None, T, W), lambda b: (b, 0, 0))],
        out_specs=pl.BlockSpec((None, nb, W), lambda b: (b, 0, 0)),
        out_shape=jax.ShapeDtypeStruct((B, nb, W), F32),
        compiler_params=_cparams(("arbitrary",)),
        name="kmean",
    )(k)


def _moba_prompt_body(qi_ref, kb_ref, last_ref, q_ref, k_ref, v_ref, km_ref, o_ref,
                      sel_ref, m_ref, l_ref, acc_ref, *, nb):
    p = pl.program_id(1)
    qi, kb = qi_ref[p], kb_ref[p]
    own = kb == qi
    q = q_ref[...]
    lane = lax.broadcasted_iota(jnp.int32, (MB_BLOCK, LANES), 1)

    @pl.when(own)
    def _():
        gate = jnp.dot(q, km_ref[...], preferred_element_type=F32, precision=lax.Precision.HIGHEST)
        j = lane % nb
        valid = (j < qi) & (lane < MB_HEADS * nb)
        g = jnp.where(valid, gate, -jnp.inf)
        rank = jnp.zeros(g.shape, jnp.int32)
        for d in range(1, nb):
            up = pltpu.roll(g, LANES - d, 1)
            rank += jnp.where((j + d < nb) & (up > g), 1, 0)
            dn = pltpu.roll(g, d, 1)
            rank += jnp.where((j - d >= 0) & (dn >= g), 1, 0)
        sel_ref[...] = jnp.where(valid & (rank < MB_TOPK), 1.0, 0.0)
        m_ref[...] = jnp.full(m_ref.shape, NEG, F32)
        l_ref[...] = jnp.zeros(l_ref.shape, F32)
        acc_ref[...] = jnp.zeros(acc_ref.shape, F32)

    kb16 = k_ref[...].astype(BF16)
    vb16 = v_ref[...].astype(BF16)
    dist = ((qi - kb) * MB_BLOCK + lax.broadcasted_iota(jnp.int32, (MB_BLOCK, MB_BLOCK), 0)
            - lax.broadcasted_iota(jnp.int32, (MB_BLOCK, MB_BLOCK), 1))
    distf = dist.astype(F32)
    sel = sel_ref[...]
    for h in range(MB_HEADS):
        g = h // MB_GROUP
        hs = slice(h * HEAD_DIM, (h + 1) * HEAD_DIM)
        gs = slice(g * HEAD_DIM, (g + 1) * HEAD_DIM)
        s = lax.dot_general(q[:, hs].astype(BF16), kb16[:, gs], _NT, preferred_element_type=F32)
        s = s * (HEAD_DIM ** -0.5) - (2.0 ** (-8.0 * (h + 1) / MB_HEADS)) * distf
        picked = jnp.sum(jnp.where(lane == h * nb + kb, sel, 0.0), axis=-1, keepdims=True) > 0.5
        s = jnp.where(own, jnp.where(dist >= 0, s, NEG), jnp.where(picked, s, NEG))
        _online_softmax_update(s, vb16[:, gs], m_ref.at[h], l_ref.at[h], acc_ref.at[h])

    @pl.when(last_ref[p] == 1)
    def _():
        o_ref[...] = jnp.concatenate([acc_ref[h] / l_ref[h] for h in range(MB_HEADS)], axis=-1)


def moba_prompt_call(q, k, v, km):
    B, T, Wq = q.shape
    Wk = k.shape[2]
    nb = T // MB_BLOCK
    steps = [(i, kb) for i in range(nb) for kb in [i] + list(range(i))]
    qi_tab = jnp.asarray([s[0] for s in steps], jnp.int32)
    kb_tab = jnp.asarray([s[1] for s in steps], jnp.int32)
    last_tab = jnp.asarray([1 if (i == 0 or kb == i - 1) else 0 for i, kb in steps], jnp.int32)
    return pl.pallas_call(
        functools.partial(_moba_prompt_body, nb=nb),
        grid_spec=pltpu.PrefetchScalarGridSpec(
            num_scalar_prefetch=3, grid=(B, len(steps)),
            in_specs=[pl.BlockSpec((None, MB_BLOCK, Wq), lambda b, p, qi, kb, la: (b, qi[p], 0)),
                      pl.BlockSpec((None, MB_BLOCK, Wk), lambda b, p, qi, kb, la: (b, kb[p], 0)),
                      pl.BlockSpec((None, MB_BLOCK, Wk), lambda b, p, qi, kb, la: (b, kb[p], 0)),
                      pl.BlockSpec((None, Wq, LANES), lambda b, p, qi, kb, la: (b, 0, 0))],
            out_specs=pl.BlockSpec((None, MB_BLOCK, Wq), lambda b, p, qi, kb, la: (b, qi[p], 0)),
            scratch_shapes=[pltpu.VMEM((MB_BLOCK, LANES), F32),
                            pltpu.VMEM((MB_HEADS, MB_BLOCK, 1), F32),
                            pltpu.VMEM((MB_HEADS, MB_BLOCK, 1), F32),
                            pltpu.VMEM((MB_HEADS, MB_BLOCK, HEAD_DIM), F32)]),
        out_shape=jax.ShapeDtypeStruct((B, T, Wq), F32),
        compiler_params=_cparams(("arbitrary", "arbitrary")),
        name="moba_prompt",
    )(qi_tab, kb_tab, last_tab, q, k, v, km)


def _gate_matrix(kmean):
    B, nb, _ = kmean.shape
    kvh = np.arange(MB_HEADS) // MB_GROUP
    km = kmean.reshape(B, nb, MB_KV_HEADS, HEAD_DIM)[:, :, kvh, :].transpose(0, 2, 3, 1)
    eye = jnp.eye(MB_HEADS, dtype=F32)
    km = (km[:, :, :, None, :] * eye[None, :, None, :, None]).reshape(B, MB_HEADS * HEAD_DIM, MB_HEADS * nb)
    return jnp.pad(km, ((0, 0), (0, 0), (0, LANES - MB_HEADS * nb)))


def _moba_sample_body(pt_ref, qbd_ref, kn_ref, vn_ref, rc_ref, *rest, npc, past):
    kp, vp = rest[:npc], rest[npc:2 * npc]
    o_ref, s_ref, p_ref, km_ref, sel_ref, m_ref, l_ref, acc_ref = rest[2 * npc:]
    ph, c = pl.program_id(1), pl.program_id(2)
    nch = pl.num_programs(2)
    bpc = npc // 2
    nb = past // MB_BLOCK
    scale = HEAD_DIM ** -0.5
    R = qbd_ref.shape[0]
    slope = rc_ref[:, 0:1]
    qs = rc_ref[:, 1:2]

    @pl.when(ph == 0)
    def _():
        qb = qbd_ref[...].astype(BF16)
        sums = []
        for i in range(bpc):
            kblk = jnp.concatenate([kp[2 * i][...], kp[2 * i + 1][...]], axis=0)
            sums.append(jnp.sum(kblk, axis=0, keepdims=True))
            s = lax.dot_general(qb, kblk.astype(BF16), _NT, preferred_element_type=F32)
            s_ref[:, pl.ds(pl.multiple_of((c * bpc + i) * MB_BLOCK, MB_BLOCK), MB_BLOCK)] = s
        km_ref[c] = jnp.concatenate(sums, axis=0) * (1.0 / MB_BLOCK)

    @pl.when((ph == 0) & (c == nch - 1))
    def _():
        km = km_ref[...].reshape(nb, km_ref.shape[2])
        gate = lax.dot_general(qbd_ref[...], km, _NT, preferred_element_type=F32,
                               precision=lax.Precision.HIGHEST)
        lane = lax.broadcasted_iota(jnp.int32, gate.shape, 1)
        sel = jnp.zeros(gate.shape, F32)
        g = gate
        for _ in range(MB_TOPK):
            mx = jnp.max(g, axis=-1, keepdims=True)
            idx = jnp.min(jnp.where(g == mx, lane, nb), axis=-1, keepdims=True)
            hit = lane == idx
            sel = jnp.where(hit, 1.0, sel)
            g = jnp.where(hit, -jnp.inf, g)
        sel_ref[...] = sel

        qb = qbd_ref[...].astype(BF16)
        kn = kn_ref[...].astype(BF16)
        s_own = lax.dot_general(qb, kn, _NT, preferred_element_type=F32) * scale
        col = lax.broadcasted_iota(jnp.int32, s_own.shape, 1).astype(F32)
        s_own = jnp.where(col <= qs, s_own - slope * (qs - col), NEG)
        m0 = jnp.max(s_own, axis=-1, keepdims=True)

        kcol = lax.broadcasted_iota(jnp.int32, (R, MB_BLOCK), 1).astype(F32)

        def block_scores(j):
            s = s_ref[:, pl.ds(pl.multiple_of(j * MB_BLOCK, MB_BLOCK), MB_BLOCK)]
            kpos = (j * MB_BLOCK).astype(F32) + kcol
            s = s * scale - slope * ((past + qs) - kpos)
            picked = jnp.sum(jnp.where(lane == j, sel, 0.0), axis=-1, keepdims=True) > 0.5
            return jnp.where(picked, s, NEG)

        m = lax.fori_loop(0, nb, lambda j, m: jnp.maximum(m, jnp.max(block_scores(j), axis=-1, keepdims=True)), m0)

        def fill(j, l):
            p = jnp.exp(block_scores(j) - m)
            p_ref[:, pl.ds(pl.multiple_of(j * MB_BLOCK, MB_BLOCK), MB_BLOCK)] = p.astype(BF16)
            return l + jnp.sum(p, axis=-1, keepdims=True)

        p_own = jnp.exp(s_own - m)
        l_ref[...] = lax.fori_loop(0, nb, fill, jnp.sum(p_own, axis=-1, keepdims=True))
        acc_ref[...] = jnp.dot(p_own.astype(BF16), vn_ref[...].astype(BF16), preferred_element_type=F32)

    @pl.when(ph == 1)
    def _():
        acc = acc_ref[...]
        for i in range(bpc):
            vblk = jnp.concatenate([vp[2 * i][...], vp[2 * i + 1][...]], axis=0).astype(BF16)
            pb = p_ref[:, pl.ds(pl.multiple_of((c * bpc + i) * MB_BLOCK, MB_BLOCK), MB_BLOCK)]
            acc = acc + jnp.dot(pb, vblk, preferred_element_type=F32)
        acc_ref[...] = acc

    @pl.when((ph == 1) & (c == nch - 1))
    def _():
        out = acc_ref[...] / l_ref[...]
        S = R // MB_HEADS
        o_ref[...] = jnp.concatenate(
            [out[h * S:(h + 1) * S, (h // MB_GROUP) * HEAD_DIM:(h // MB_GROUP + 1) * HEAD_DIM]
             for h in range(MB_HEADS)], axis=-1)


def moba_sample_call(page_table, q, k_new, v_new, pool_k, pool_v, npc=16):
    B, S, Wq = q.shape
    Wk = k_new.shape[2]
    n_pages = page_table.shape[1]
    past = n_pages * PAGE_SIZE
    nch = n_pages // npc
    assert past % MB_BLOCK == 0 and past // MB_BLOCK >= MB_TOPK and n_pages % npc == 0
    R = MB_HEADS * S
    kvh = np.arange(MB_HEADS) // MB_GROUP
    onehot = jnp.asarray(np.eye(MB_KV_HEADS, dtype=np.float32)[kvh])
    qh = q.reshape(B, S, MB_HEADS, HEAD_DIM).transpose(0, 2, 1, 3)
    qbd = (qh[:, :, :, None, :] * onehot[None, :, None, :, None]).reshape(B, R, Wk)
    pad = lambda t: jnp.pad(t, ((0, 0), (0, LANES - S), (0, 0)))
    rc = np.zeros((R, LANES), np.float32)
    rc[:, 0] = np.repeat([2.0 ** (-8.0 * (h + 1) / MB_HEADS) for h in range(MB_HEADS)], S)
    rc[:, 1] = np.tile(np.arange(S), MB_HEADS)
    kpage = lambda j: (lambda b, ph, c, pt: (pt[b, jnp.where(ph == 0, c, nch - 1) * npc + j], 0, 0))
    vpage = lambda j: (lambda b, ph, c, pt: (pt[b, jnp.where(ph == 0, 0, c) * npc + j], 0, 0))
    full = lambda shp: pl.BlockSpec((None,) + shp, lambda b, ph, c, pt: (b, 0, 0))
    return pl.pallas_call(
        functools.partial(_moba_sample_body, npc=npc, past=past),
        grid_spec=pltpu.PrefetchScalarGridSpec(
            num_scalar_prefetch=1, grid=(B, 2, nch),
            in_specs=([full((R, Wk)), full((LANES, Wk)), full((LANES, Wk)),
                       pl.BlockSpec((R, LANES), lambda b, ph, c, pt: (0, 0))]
                      + [pl.BlockSpec((None, PAGE_SIZE, Wk), kpage(j)) for j in range(npc)]
                      + [pl.BlockSpec((None, PAGE_SIZE, Wk), vpage(j)) for j in range(npc)]),
            out_specs=full((S, Wq)),
            scratch_shapes=[pltpu.VMEM((R, past), F32), pltpu.VMEM((R, past), BF16),
                            pltpu.VMEM((nch, npc // 2, Wk), F32), pltpu.VMEM((R, past // MB_BLOCK), F32),
                            pltpu.VMEM((R, 1), F32), pltpu.VMEM((R, 1), F32), pltpu.VMEM((R, Wk), F32)]),
        out_shape=jax.ShapeDtypeStruct((B, S, Wq), F32),
        compiler_params=_cparams(("arbitrary", "arbitrary", "arbitrary")),
        name="moba_sample",
    )(page_table, qbd, pad(k_new), pad(v_new), jnp.asarray(rc), *([pool_k] * npc), *([pool_v] * npc))


def _l2norm(x):
    return x * lax.rsqrt(jnp.sum(x * x, axis=-1, keepdims=True) + 1e-12)


def _rope_tables(pos):
    half = MLA_ROPE // 2
    inv = ROPE_THETA ** (-jnp.arange(half, dtype=F32) / half)
    ang = pos.astype(F32)[:, None] * inv[None, :]
    return jnp.cos(ang), jnp.sin(ang)


def _rwkv_mix(p, prev, S0, prm, lanes_bh):
    mu, w0, w2, a0, a2, g2, k_k, k_a, r_k, lnx_g, lnx_b = prm
    B, T, _ = p.shape
    H = RW_HEADS
    p_prev = jnp.concatenate([prev[:, None, :], p[:, :-1]], axis=1)
    pm = p + (p_prev - p) * mu
    cuts = [int(c) for c in np.cumsum(RW_SPLITS)[:-1]]
    r, wl, k, v, al, gl = jnp.split(pm, cuts, axis=-1)
    w_log = -jax.nn.softplus(-(w0 + jnp.tanh(wl) @ w2)) - 0.5
    decay = jnp.exp(-jnp.exp(w_log))
    a = jax.nn.sigmoid(a0 + al @ a2)
    g = jax.nn.sigmoid(gl) @ g2
    heads = lambda t: t.reshape(B, T, H, HEAD_DIM)
    kk = _l2norm(heads(k * k_k))
    k = k * (1.0 + (a - 1.0) * k_a)
    r_h, k_h, v_h, w_h, a_h = heads(r), heads(k), heads(v), heads(decay), heads(a)
    aa, bb = -kk, kk * a_h
    ts = lambda t: _to_scan(t, lanes_bh)
    s0 = _state_to_scan(jnp.swapaxes(S0, -1, -2), lanes_bh)
    y, S = scan_call(ts(r_h), ts(w_h), ts(k_h), ts(v_h), ts(aa), ts(bb), s0, "layer")
    y = _from_scan(y, B, H, lanes_bh).reshape(B, T, RW_WIDTH) * lnx_g + lnx_b
    S = jnp.swapaxes(_state_from_scan(S, B, H, lanes_bh), -1, -2)
    bonus = jnp.sum(r_h * k_h * r_k, axis=-1, keepdims=True) * v_h
    y = (y + bonus.reshape(B, T, RW_WIDTH)) * g
    return y, S, p[:, -1]


def _gdn_mix(qkv, z, a_raw, b_raw, conv_buf, S0, prm, lanes_bh):
    conv_w, a_log, dt_bias, norm_g = prm
    B, T, _ = qkv.shape
    H = GD_HEADS
    xx = jnp.concatenate([conv_buf, qkv], axis=1)
    y = sum(xx[:, j:j + T] * conv_w[j] for j in range(GD_CONV))
    y = _silu(y)
    q, k, v = jnp.split(y, [H * GD_DK, 2 * H * GD_DK], axis=-1)
    q = _l2norm(q.reshape(B, T, H, GD_DK)) * (GD_DK ** -0.5)
    k = _l2norm(k.reshape(B, T, H, GD_DK))
    v = v.reshape(B, T, H, GD_DV)
    beta = jax.nn.sigmoid(b_raw)[..., None]
    alpha = jnp.exp(-jnp.exp(a_log) * jax.nn.softplus(a_raw + dt_bias))[..., None]
    ts = lambda t: _to_scan(t, lanes_bh)
    w = jnp.broadcast_to(alpha, k.shape)
    y, S = scan_call(ts(q), ts(w), ts(k), ts(beta * v), ts(k), ts(-beta * alpha * k),
                     _state_to_scan(S0, lanes_bh), "rms")
    o = _from_scan(y, B, H, lanes_bh) * norm_g * _silu(z.reshape(B, T, H, GD_DV))
    S = _state_from_scan(S, B, H, lanes_bh)
    return o.reshape(B, T, H * GD_DV), S, xx[:, -(GD_CONV - 1):]


def _bdiag(w):
    H, a, b = w.shape
    eye = jnp.eye(H, dtype=w.dtype)
    return (eye[:, None, :, None] * w[:, :, None, :]).reshape(H * a, H * b)


def kernel(x_prompt, x_sample, cache_moba_k, cache_moba_v, cache_mla_ckv, cache_mla_krope, state_rwkv_wkv, state_rwkv_shift, state_gdn, state_gdn_conv, page_table, c_prompt, c_sample, w_ada, b_ada, g_mix_pre, g_mix_post, g_ff_pre, g_ff_post, w_in_even, w_out_even, rw_mu, rw_w0, rw_w2, rw_a0, rw_a2, rw_g2, rw_kk, rw_ka, rw_rk, rw_lnx_g, rw_lnx_b, ffn_w_gate, ffn_w_up, ffn_w_down, w_in_odd, w_out_odd, mla_q_norm, mla_w_q_up, mla_kv_norm, mla_w_uk, mla_w_uv, gdn_conv_w, gdn_a_log, gdn_dt_bias, gdn_norm_g, moe_w_router, moe_b_router, moe_w_gate, moe_w_up, moe_w_down):
    Bp, Tp, D = x_prompt.shape
    Bs, Ts, _ = x_sample.shape
    past = page_table.shape[1] * PAGE_SIZE
    Ms = Bs * Ts
    row = lambda v: v.reshape(1, -1)

    xp = x_prompt
    xs = x_sample.reshape(1, Ms, D)
    c_all = jnp.concatenate([c_prompt, c_sample], axis=0)

    outs_p, outs_s = {}, {}
    for layer in range(w_ada.shape[0]):
        i = layer // 2
        mod = adaln_call(c_all, w_ada[layer].astype(BF16), row(b_ada[layer]))
        mp = [m.reshape(Bp, 1, D) for m in jnp.split(mod[:Bp], 6, axis=-1)]
        ms = [jnp.broadcast_to(m[:, None, :], (Bs, Ts, D)).reshape(1, Ms, D)
              for m in jnp.split(mod[Bp:], 6, axis=-1)]
        gpre, gpost = row(g_mix_pre[layer]), row(g_mix_post[layer])
        if layer % 2 == 0:
            w_in = w_in_even[i].astype(BF16)
            splits = (RW_PROJ, MB_HEADS * HEAD_DIM, MB_KV_HEADS * HEAD_DIM, MB_KV_HEADS * HEAD_DIM)
            rw_prm = (rw_mu[i], rw_w0[i], rw_w2[i], rw_a0[i], rw_a2[i], rw_g2[i], rw_kk[i], rw_ka[i],
                      rw_rk[i], rw_lnx_g[i], rw_lnx_b[i])
            w_out = w_out_even[i].astype(BF16)
            p_rw, q, k, v = modmm_call(xp, mp[0], mp[1], gpre, w_in, splits)
            y_rw, S_p, shift_p = _rwkv_mix(p_rw, jnp.zeros((Bp, RW_PROJ), F32),
                                           jnp.zeros((Bp, RW_HEADS, HEAD_DIM, HEAD_DIM), F32), rw_prm, True)
            k4 = k.reshape(Bp, Tp, MB_KV_HEADS, HEAD_DIM)
            v4 = v.reshape(Bp, Tp, MB_KV_HEADS, HEAD_DIM)
            y_mb = moba_prompt_call(q, k, v, _gate_matrix(kmean_call(k)))
            xp = outproj_call(y_rw, y_mb, w_out[:RW_WIDTH], w_out[RW_WIDTH:], xp, mp[2], gpost)
            outs_p.update(mk=k4[None], mv=v4[None], wkv=S_p[None], shift=shift_p[None])
            p_rw, q, k, v = modmm_call(xs, ms[0], ms[1], gpre, w_in, splits)
            y_rw, S_s, shift_s = _rwkv_mix(p_rw.reshape(Bs, Ts, RW_PROJ), state_rwkv_shift[i],
                                           state_rwkv_wkv[i], rw_prm, False)
            k4 = k.reshape(Bs, Ts, MB_KV_HEADS, HEAD_DIM)
            v4 = v.reshape(Bs, Ts, MB_KV_HEADS, HEAD_DIM)
            n_pool = cache_moba_k.shape[1]
            y_mb = moba_sample_call(page_table, q.reshape(Bs, Ts, -1), k.reshape(Bs, Ts, -1), v.reshape(Bs, Ts, -1),
                                    cache_moba_k[i].reshape(n_pool, PAGE_SIZE, -1),
                                    cache_moba_v[i].reshape(n_pool, PAGE_SIZE, -1))
            xs = outproj_call(y_rw.reshape(1, Ms, RW_WIDTH), y_mb.reshape(1, Ms, -1),
                              w_out[:RW_WIDTH], w_out[RW_WIDTH:], xs, ms[2], gpost)
            outs_s.update(mk=k4[None], mv=v4[None], wkv=S_s[None], shift=shift_s[None])
        else:
            w_in = w_in_odd[i]
            cuts = [int(c) for c in np.cumsum(ODD_SPLITS)[:-1]]
            w_cq, w_ckv, w_kr, w_qkv, w_z, w_a, w_b = jnp.split(w_in, cuts, axis=1)
            padc = lambda w, n: jnp.pad(w, ((0, 0), (0, n - w.shape[1])))
            w_in_p = jnp.concatenate([w_cq, w_ckv, padc(w_kr, LANES), w_qkv, w_z,
                                      padc(jnp.concatenate([w_a, w_b], axis=1), LANES)], axis=1).astype(BF16)
            splits = (MLA_Q_LORA, MLA_KV_LORA, LANES, GD_QKV, GD_HEADS * GD_DV, LANES)
            w_out = w_out_odd[i].astype(BF16)
            gd_prm = (gdn_conv_w[i], gdn_a_log[i], gdn_dt_bias[i], gdn_norm_g[i])
            w_uv_bd = _bdiag(mla_w_uv[i].transpose(1, 0, 2)).astype(BF16)
            w_uk_bd = _bdiag(mla_w_uk[i].transpose(1, 2, 0)).astype(BF16)
            wq = mla_w_q_up[i].reshape(MLA_Q_LORA, MLA_HEADS, MLA_NOPE + MLA_ROPE)
            w_qn = wq[:, :, :MLA_NOPE].reshape(MLA_Q_LORA, -1).astype(BF16)
            w_qr = jnp.pad(wq[:, :, MLA_NOPE:], ((0, 0), (0, 0), (0, LANES - MLA_ROPE)))
            w_qr = w_qr.reshape(MLA_Q_LORA, -1).astype(BF16)
            w_oa, w_ob = w_out[:MLA_HEADS * MLA_V], w_out[MLA_HEADS * MLA_V:]

            def odd(x, m, B, T, cosf, sinf, conv_buf, S0, lanes_bh):
                cq, ckv, kr, qkv, z, ab = modmm_call(x, m[0], m[1], gpre, w_in_p, splits)
                qcat, kcat, ckv_n, k_rope = mla_prep_call(
                    cq, ckv, kr, cosf, sinf, row(mla_q_norm[i]), row(mla_kv_norm[i]), w_qn, w_qr, w_uk_bd)
                qkv, z = qkv.reshape(B, T, -1), z.reshape(B, T, -1)
                ab = ab.reshape(B, T, LANES)
                a_raw, b_raw = ab[..., :GD_HEADS], ab[..., GD_HEADS:2 * GD_HEADS]
                y_gdn, S, buf = _gdn_mix(qkv, z, a_raw, b_raw, conv_buf, S0, gd_prm, lanes_bh)
                ckv_n = ckv_n.reshape(B, T, MLA_KV_LORA)
                k_rope = k_rope.reshape(B, T, LANES)[..., :MLA_ROPE]
                return qcat, kcat, y_gdn, ckv_n, k_rope, S, buf

            cosf, sinf = _rope_rows(jnp.arange(Tp))
            qcat, kcat, y_gdn, ckv_n, k_rope, S_p, buf_p = odd(
                xp, mp, Bp, Tp, cosf, sinf, jnp.zeros((Bp, GD_CONV - 1, GD_QKV), F32),
                jnp.zeros((Bp, GD_HEADS, GD_DK, GD_DV), F32), True)
            o_lat = mla_prompt_call(qcat.reshape(Bp, Tp * MLA_HEADS, 2 * LANES), kcat)
            xp = outproj_call(o_lat.reshape(Bp, Tp, -1), y_gdn, w_oa, w_ob, xp, mp[2], gpost, wpre=w_uv_bd)
            outs_p.update(ckv=ckv_n[None], krope=k_rope[None], gdn=S_p[None], conv=buf_p[None])
            cosf, sinf = _rope_rows(jnp.tile(past + jnp.arange(Ts), Bs))
            qcat, kcat, y_gdn, ckv_n, k_rope, S_s, buf_s = odd(
                xs, ms, Bs, Ts, cosf, sinf, state_gdn_conv[i], state_gdn[i], False)
            kn = jnp.pad(kcat.reshape(Bs, Ts, 2 * LANES), ((0, 0), (0, LANES - Ts), (0, 0)))
            o_lat = mla_sample_call(page_table, qcat.reshape(Bs, Ts * MLA_HEADS, 2 * LANES), kn,
                                    cache_mla_ckv[i], cache_mla_krope[i])
            xs = outproj_call(o_lat.reshape(1, Ms, -1), y_gdn.reshape(1, Ms, -1), w_oa, w_ob, xs, ms[2], gpost,
                              wpre=w_uv_bd)
            outs_s.update(ckv=ckv_n[None], krope=k_rope[None], gdn=S_s[None], conv=buf_s[None])

        gpre, gpost = row(g_ff_pre[layer]), row(g_ff_post[layer])
        if layer % 2 == 0:
            J = 2
            Fd = ffn_w_gate.shape[2] // J
            wg = ffn_w_gate[i].astype(BF16).reshape(D, J, Fd).transpose(1, 0, 2)
            wu = ffn_w_up[i].astype(BF16).reshape(D, J, Fd).transpose(1, 0, 2)
            wd = ffn_w_down[i].astype(BF16).reshape(J, Fd, D)
            xp = ffn_call(xp, mp[3], mp[4], mp[5], gpre, gpost, wg, wu, wd)
            xs = ffn_call(xs, ms[3], ms[4], ms[5], gpre, gpost, wg, wu, wd)
        else:
            wr = jnp.pad(moe_w_router[i], ((0, 0), (0, LANES - N_EXPERTS)))
            br = jnp.pad(row(moe_b_router[i]), ((0, 0), (0, LANES - N_EXPERTS)))
            wg, wu, wd = (w[i].astype(BF16) for w in (moe_w_gate, moe_w_up, moe_w_down))
            xp = ffn_call(xp, mp[3], mp[4], mp[5], gpre, gpost, wg, wu, wd, wr, br)
            xs = ffn_call(xs, ms[3], ms[4], ms[5], gpre, gpost, wg, wu, wd, wr, br)

    names = ('mk', 'mv', 'wkv', 'shift', 'ckv', 'krope', 'gdn', 'conv')
    return ((xp, xs.reshape(Bs, Ts, D)) + tuple(outs_p[n] for n in names)
            + tuple(outs_s[n] for n in names))
```

```python
import functools
import math

import jax
import jax.numpy as jnp
import numpy as np
from jax import lax
from jax.experimental import pallas as pl
from jax.experimental.pallas import tpu as pltpu

F32 = jnp.float32
BF16 = jnp.bfloat16

NORM_EPS = 1e-6
HEAD_DIM = 64
PAGE_SIZE = 128
RW_HEADS = 8
RW_WIDTH = RW_HEADS * HEAD_DIM
RW_SPLITS = (RW_WIDTH, 64, RW_WIDTH, RW_WIDTH, 64, 128)
RW_PROJ = sum(RW_SPLITS)
RW_LNX_EPS = 64e-5
MB_HEADS = 8
MB_KV_HEADS = 4
MB_GROUP = MB_HEADS // MB_KV_HEADS
MB_BLOCK = 256
MB_TOPK = 3
MLA_HEADS = 8
MLA_NOPE = 64
MLA_ROPE = 32
MLA_V = 64
MLA_Q_LORA = 256
MLA_KV_LORA = 128
ROPE_THETA = 10000.0
GD_HEADS = 8
GD_DK = 64
GD_DV = 64
GD_CONV = 4
GD_QKV = GD_HEADS * (2 * GD_DK + GD_DV)
ODD_SPLITS = (MLA_Q_LORA, MLA_KV_LORA, MLA_ROPE, GD_QKV, GD_HEADS * GD_DV, GD_HEADS, GD_HEADS)
N_EXPERTS = 8

LANES = 128
VMEM_LIMIT = 56 * 1024 * 1024


def _cparams(sem):
    return pltpu.CompilerParams(dimension_semantics=sem, vmem_limit_bytes=VMEM_LIMIT)


def _silu(x):
    return x * jax.nn.sigmoid(x)


def _rms(x, eps=NORM_EPS):
    return x * lax.rsqrt(jnp.mean(x * x, axis=-1, keepdims=True) + eps)


def _adaln_body(c_ref, w_ref, b_ref, o_ref):
    c = c_ref[...]
    o_ref[...] = jnp.dot(_silu(c).astype(BF16), w_ref[...], preferred_element_type=F32) + b_ref[...]


def adaln_call(c, w, b, tn=1536):
    R, D = c.shape
    N = w.shape[1]
    return pl.pallas_call(
        _adaln_body,
        grid=(N // tn,),
        in_specs=[pl.BlockSpec((R, D), lambda j: (0, 0)),
                  pl.BlockSpec((D, tn), lambda j: (0, j)),
                  pl.BlockSpec((1, tn), lambda j: (0, j))],
        out_specs=pl.BlockSpec((R, tn), lambda j: (0, j)),
        out_shape=jax.ShapeDtypeStruct((R, N), F32),
        compiler_params=_cparams(("arbitrary",)),
        name="adaln",
    )(c, w, b)


def _mod_spec(mod, tm):
    if mod.shape[1] == 1:
        return pl.BlockSpec((None, 1, mod.shape[2]), lambda g, i, *_: (g, 0, 0))
    return pl.BlockSpec((None, tm, mod.shape[2]), lambda g, i, *_: (g, i, 0))


def _modmm_body(x_ref, sh_ref, sc_ref, g_ref, w_ref, *o_refs, splits):
    h = _rms(x_ref[...]) * g_ref[...] * (1.0 + sc_ref[...]) + sh_ref[...]
    y = jnp.dot(h.astype(BF16), w_ref[...], preferred_element_type=F32)
    off = 0
    for o_ref, n in zip(o_refs, splits):
        o_ref[...] = y[:, off:off + n].astype(o_ref.dtype)
        off += n


def modmm_call(x, shift, scale, gain, w, splits, tm=512):
    G, R, D = x.shape
    tm = min(tm, R)
    N = w.shape[1]
    assert sum(splits) == N and all(s % LANES == 0 for s in splits)
    return pl.pallas_call(
        functools.partial(_modmm_body, splits=splits),
        grid=(G, R // tm),
        in_specs=[pl.BlockSpec((None, tm, D), lambda g, i: (g, i, 0)),
                  _mod_spec(shift, tm), _mod_spec(scale, tm),
                  pl.BlockSpec((1, D), lambda g, i: (0, 0)),
                  pl.BlockSpec((D, N), lambda g, i: (0, 0))],
        out_specs=[pl.BlockSpec((None, tm, n), lambda g, i: (g, i, 0)) for n in splits],
        out_shape=[jax.ShapeDtypeStruct((G, R, n), F32) for n in splits],
        compiler_params=_cparams(("arbitrary", "arbitrary")),
        name="modmm",
    )(x, shift, scale, gain, w)


def _outproj_body(ya_ref, yb_ref, wpre_ref, wa_ref, wb_ref, x_ref, ga_ref, gp_ref, o_ref, *, pre):
    ya = ya_ref[...].astype(BF16)
    if pre:
        ya = jnp.dot(ya, wpre_ref[...], preferred_element_type=F32).astype(BF16)
    y = jnp.dot(ya, wa_ref[...], preferred_element_type=F32)
    y = y + jnp.dot(yb_ref[...].astype(BF16), wb_ref[...], preferred_element_type=F32)
    o_ref[...] = x_ref[...] + ga_ref[...] * (_rms(y) * gp_ref[...])


def outproj_call(ya, yb, wa, wb, x, gate, gpost, wpre=None, tm=512):
    G, R, D = x.shape
    tm = min(tm, R)
    Ka, Kb = ya.shape[2], yb.shape[2]
    pre = wpre is not None
    if not pre:
        wpre = jnp.zeros((8, LANES), BF16)
    return pl.pallas_call(
        functools.partial(_outproj_body, pre=pre),
        grid=(G, R // tm),
        in_specs=[pl.BlockSpec((None, tm, Ka), lambda g, i: (g, i, 0)),
                  pl.BlockSpec((None, tm, Kb), lambda g, i: (g, i, 0)),
                  pl.BlockSpec(wpre.shape, lambda g, i: (0, 0)),
                  pl.BlockSpec(wa.shape, lambda g, i: (0, 0)),
                  pl.BlockSpec(wb.shape, lambda g, i: (0, 0)),
                  pl.BlockSpec((None, tm, D), lambda g, i: (g, i, 0)),
                  _mod_spec(gate, tm),
                  pl.BlockSpec((1, D), lambda g, i: (0, 0))],
        out_specs=pl.BlockSpec((None, tm, D), lambda g, i: (g, i, 0)),
        out_shape=jax.ShapeDtypeStruct((G, R, D), F32),
        compiler_params=_cparams(("arbitrary", "arbitrary")),
        name="outproj",
    )(ya, yb, wpre, wa, wb, x, gate, gpost)


def _mla_prep_body(cq_ref, ckv_ref, kr_ref, cos_ref, sin_ref, qg_ref, kg_ref, wqn_ref, wqr_ref, wuk_ref,
                   qcat_ref, kcat_ref, ckvn_ref, krope_ref):
    cosf, sinf = cos_ref[...], sin_ref[...]
    lane = lax.broadcasted_iota(jnp.int32, cosf.shape, 1)
    half = MLA_ROPE // 2

    def rope(x):
        rot = jnp.where(lane < half, pltpu.roll(x, LANES - half, 1), pltpu.roll(x, half, 1))
        return x * cosf + rot * sinf

    cqn = (_rms(cq_ref[...]) * qg_ref[...]).astype(BF16)
    qn = jnp.dot(cqn, wqn_ref[...], preferred_element_type=F32).astype(BF16)
    qr = jnp.dot(cqn, wqr_ref[...], preferred_element_type=F32)
    ql = jnp.dot(qn, wuk_ref[...], preferred_element_type=F32)
    for h in range(MLA_HEADS):
        qcat_ref[:, 2 * h * LANES:(2 * h + 1) * LANES] = ql[:, h * LANES:(h + 1) * LANES].astype(BF16)
        qcat_ref[:, (2 * h + 1) * LANES:(2 * h + 2) * LANES] = rope(qr[:, h * LANES:(h + 1) * LANES]).astype(BF16)
    ckvn = _rms(ckv_ref[...]) * kg_ref[...]
    krope = rope(kr_ref[...])
    ckvn_ref[...] = ckvn
    krope_ref[...] = krope
    kcat_ref[:, :LANES] = ckvn.astype(BF16)
    kcat_ref[:, LANES:] = krope.astype(BF16)


def mla_prep_call(cq, ckv, kr, cosf, sinf, q_gain, kv_gain, w_qn, w_qr, w_uk_bd, tm=512):
    G, R, _ = cq.shape
    tm = min(tm, R)
    rows = lambda n: pl.BlockSpec((None, tm, n), lambda g, i: (g, i, 0))
    const = lambda a: pl.BlockSpec(a.shape, lambda g, i: (0, 0))
    W = 2 * LANES * MLA_HEADS
    return pl.pallas_call(
        _mla_prep_body,
        grid=(G, R // tm),
        in_specs=[rows(cq.shape[2]), rows(LANES), rows(LANES),
                  pl.BlockSpec((tm, LANES), lambda g, i: (i, 0)), pl.BlockSpec((tm, LANES), lambda g, i: (i, 0)),
                  const(q_gain), const(kv_gain), const(w_qn), const(w_qr), const(w_uk_bd)],
        out_specs=[rows(W), rows(2 * LANES), rows(LANES), rows(LANES)],
        out_shape=[jax.ShapeDtypeStruct((G, R, W), BF16), jax.ShapeDtypeStruct((G, R, 2 * LANES), BF16),
                   jax.ShapeDtypeStruct((G, R, LANES), F32), jax.ShapeDtypeStruct((G, R, LANES), F32)],
        compiler_params=_cparams(("arbitrary", "arbitrary")),
        name="mla_prep",
    )(cq, ckv, kr, cosf, sinf, q_gain, kv_gain, w_qn, w_qr, w_uk_bd)


def _rope_rows(pos):
    cos, sin = _rope_tables(pos)
    z = jnp.zeros((pos.shape[0], LANES - MLA_ROPE), F32)
    return jnp.concatenate([cos, cos, z], axis=1), jnp.concatenate([-sin, sin, z], axis=1)


def _ffn_body(x_ref, sh_ref, sc_ref, ga_ref, gpre_ref, gpost_ref, wr_ref, br_ref,
              wg_ref, wu_ref, wd_ref, o_ref, h_ref, acc_ref, rw_ref, *, moe):
    j = pl.program_id(2)

    @pl.when(j == 0)
    def _():
        h = _rms(x_ref[...]) * gpre_ref[...] * (1.0 + sc_ref[...]) + sh_ref[...]
        h_ref[...] = h.astype(BF16)
        acc_ref[...] = jnp.zeros_like(acc_ref)
        if moe:
            logits = jnp.dot(h, wr_ref[...], preferred_element_type=F32,
                             precision=lax.Precision.HIGHEST) + br_ref[...]
            lane = lax.broadcasted_iota(jnp.int32, logits.shape, 1)
            neg = jnp.float32(-jnp.inf)
            logits = jnp.where(lane < N_EXPERTS, logits, neg)
            m1 = jnp.max(logits, axis=-1, keepdims=True)
            i1 = jnp.min(jnp.where(logits == m1, lane, LANES), axis=-1, keepdims=True)
            rest = jnp.where(lane == i1, neg, logits)
            m2 = jnp.max(rest, axis=-1, keepdims=True)
            i2 = jnp.min(jnp.where(rest == m2, lane, LANES), axis=-1, keepdims=True)
            e2 = jnp.exp(m2 - m1)
            w1 = 1.0 / (1.0 + e2)
            w2 = e2 / (1.0 + e2)
            rw_ref[...] = jnp.where(lane == i1, w1, 0.0) + jnp.where(lane == i2, w2, 0.0)

    hb = h_ref[...]
    g = jnp.dot(hb, wg_ref[...], preferred_element_type=F32)
    u = jnp.dot(hb, wu_ref[...], preferred_element_type=F32)
    f = jnp.dot((_silu(g) * u).astype(BF16), wd_ref[...], preferred_element_type=F32)
    if moe:
        lane = lax.broadcasted_iota(jnp.int32, rw_ref.shape, 1)
        we = jnp.sum(jnp.where(lane == j, rw_ref[...], 0.0), axis=-1, keepdims=True)
        f = f * we
    acc_ref[...] += f

    @pl.when(j == pl.num_programs(2) - 1)
    def _():
        o_ref[...] = x_ref[...] + ga_ref[...] * (_rms(acc_ref[...]) * gpost_ref[...])


def ffn_call(x, shift, scale, gate, gpre, gpost, wg, wu, wd, w_router=None, b_router=None, tm=512):
    G, R, D = x.shape
    tm = min(tm, R)
    J, _, Fd = wg.shape
    moe = w_router is not None
    if not moe:
        w_router = jnp.zeros((D, LANES), F32)
        b_router = jnp.zeros((1, LANES), F32)
    return pl.pallas_call(
        functools.partial(_ffn_body, moe=moe),
        grid=(G, R // tm, J),
        in_specs=[pl.BlockSpec((None, tm, D), lambda g, i, j: (g, i, 0)),
                  _mod_spec(shift, tm), _mod_spec(scale, tm), _mod_spec(gate, tm),
                  pl.BlockSpec((1, D), lambda g, i, j: (0, 0)),
                  pl.BlockSpec((1, D), lambda g, i, j: (0, 0)),
                  pl.BlockSpec((D, LANES), lambda g, i, j: (0, 0)),
                  pl.BlockSpec((1, LANES), lambda g, i, j: (0, 0)),
                  pl.BlockSpec((None, D, Fd), lambda g, i, j: (j, 0, 0)),
                  pl.BlockSpec((None, D, Fd), lambda g, i, j: (j, 0, 0)),
                  pl.BlockSpec((None, Fd, D), lambda g, i, j: (j, 0, 0))],
        out_specs=pl.BlockSpec((None, tm, D), lambda g, i, j: (g, i, 0)),
        out_shape=jax.ShapeDtypeStruct((G, R, D), F32),
        scratch_shapes=[pltpu.VMEM((tm, D), BF16), pltpu.VMEM((tm, D), F32),
                        pltpu.VMEM((tm, LANES), F32)],
        compiler_params=_cparams(("arbitrary", "arbitrary", "arbitrary")),
        name="moe" if moe else "ffn",
    )(x, shift, scale, gate, gpre, gpost, w_router, b_router, wg, wu, wd)


def _scan_body(r_ref, w_ref, k_ref, v_ref, a_ref, b_ref, an_ref, s0_ref, y_ref, s_ref, sa_ref,
               *, tt, dk, norm):
    i = pl.program_id(1)

    @pl.when(i == 0)
    def _():
        s_ref[...] = s0_ref[...]
        sa = jnp.zeros(sa_ref.shape, F32)
        for k in range(dk):
            sa = sa + s0_ref[k] * a_ref[0, k:k + 1, :]
        sa_ref[...] = sa

    def step(t, sa):
        v_t = v_ref[t]
        tn = jnp.minimum(t + 1, tt - 1)
        last = t == tt - 1
        y = jnp.zeros(sa.shape, F32)
        sa_n = jnp.zeros(sa.shape, F32)
        for k in range(dk):
            new = (s_ref[k] * w_ref[t, k:k + 1, :] + sa * b_ref[t, k:k + 1, :]
                   + v_t * k_ref[t, k:k + 1, :])
            s_ref[k] = new
            y = y + new * r_ref[t, k:k + 1, :]
            a_next = jnp.where(last, an_ref[0, k:k + 1, :], a_ref[tn, k:k + 1, :])
            sa_n = sa_n + new * a_next
        if norm == "layer":
            mu = jnp.mean(y, axis=0, keepdims=True)
            d = y - mu
            y = d * lax.rsqrt(jnp.mean(d * d, axis=0, keepdims=True) + RW_LNX_EPS)
        else:
            y = y * lax.rsqrt(jnp.mean(y * y, axis=0, keepdims=True) + NORM_EPS)
        y_ref[t] = y
        return sa_n

    sa_ref[...] = lax.fori_loop(0, tt, step, sa_ref[...])


def scan_call(r, w, k, v, a, b, s0, norm, tt=32):
    G, T, dk, L = r.shape
    dv = v.shape[2]
    tt = min(tt, T)
    nt = T // tt
    tspec = lambda d: pl.BlockSpec((None, tt, d, L), lambda g, i: (g, i, 0, 0))
    nxt = pl.BlockSpec((None, 1, dk, L), lambda g, i: (g, jnp.minimum((i + 1) * tt, T - 1), 0, 0))
    sspec = pl.BlockSpec((None, dk, dv, L), lambda g, i: (g, 0, 0, 0))
    return pl.pallas_call(
        functools.partial(_scan_body, tt=tt, dk=dk, norm=norm),
        grid=(G, nt),
        in_specs=[tspec(dk), tspec(dk), tspec(dk), tspec(dv), tspec(dk), tspec(dk), nxt, sspec],
        out_specs=[tspec(dv), sspec],
        out_shape=[jax.ShapeDtypeStruct((G, T, dv, L), F32),
                   jax.ShapeDtypeStruct((G, dk, dv, L), F32)],
        scratch_shapes=[pltpu.VMEM((dv, L), F32)],
        compiler_params=_cparams(("arbitrary", "arbitrary")),
        name="scan",
    )(r, w, k, v, a, b, a, s0)


def _to_scan(x, lanes_bh):
    B, T, H, C = x.shape
    if lanes_bh:
        return x.transpose(1, 3, 0, 2).reshape(1, T, C, B * H)
    return x.transpose(2, 1, 3, 0)


def _from_scan(y, B, H, lanes_bh):
    G, T, C, L = y.shape
    if lanes_bh:
        return y.reshape(T, C, B, H).transpose(2, 0, 3, 1)
    return y.transpose(3, 1, 0, 2)


def _state_to_scan(s, lanes_bh):
    B, H, dk, dv = s.shape
    if lanes_bh:
        return s.transpose(2, 3, 0, 1).reshape(1, dk, dv, B * H)
    return s.transpose(1, 2, 3, 0)


def _state_from_scan(s, B, H, lanes_bh):
    G, dk, dv, L = s.shape
    if lanes_bh:
        return s.reshape(dk, dv, B, H).transpose(2, 3, 0, 1)
    return s.transpose(3, 0, 1, 2)


NEG = -1e30


def _online_softmax_update(s, v, m_ref, l_ref, acc_ref):
    m_prev = m_ref[...]
    m_new = jnp.maximum(m_prev, jnp.max(s, axis=-1, keepdims=True))
    alpha = jnp.exp(m_prev - m_new)
    p = jnp.exp(s - m_new)
    l_ref[...] = alpha * l_ref[...] + jnp.sum(p, axis=-1, keepdims=True)
    acc_ref[...] = alpha * acc_ref[...] + jnp.dot(p.astype(BF16), v, preferred_element_type=F32)
    m_ref[...] = m_new


_NT = (((1,), (1,)), ((), ()))


def _mla_prompt_body(qi_ref, kj_ref, q_ref, k_ref, o_ref, m_ref, l_ref, acc_ref, *, tq, tk, scale):
    p = pl.program_id(1)
    qi, kj = qi_ref[p], kj_ref[p]

    @pl.when(kj == 0)
    def _():
        m_ref[...] = jnp.full(m_ref.shape, NEG, F32)
        l_ref[...] = jnp.zeros(l_ref.shape, F32)
        acc_ref[...] = jnp.zeros(acc_ref.shape, F32)

    k = k_ref[...]
    s = lax.dot_general(q_ref[...], k, _NT, preferred_element_type=F32) * scale
    tok = qi * tq + lax.broadcasted_iota(jnp.int32, s.shape, 0) // MLA_HEADS
    kpos = kj * tk + lax.broadcasted_iota(jnp.int32, s.shape, 1)
    s = jnp.where(kpos <= tok, s, NEG)
    _online_softmax_update(s, k[:, :MLA_KV_LORA], m_ref, l_ref, acc_ref)

    @pl.when(kj == (qi * tq + tq - 1) // tk)
    def _():
        o_ref[...] = (acc_ref[...] / l_ref[...]).astype(o_ref.dtype)


def mla_prompt_call(qcat, kcat, tq=256, tk=256):
    B, TH, W = qcat.shape
    T = kcat.shape[1]
    pairs = [(i, j) for i in range(T // tq) for j in range((i * tq + tq - 1) // tk + 1)]
    qi_tab = jnp.asarray([p[0] for p in pairs], jnp.int32)
    kj_tab = jnp.asarray([p[1] for p in pairs], jnp.int32)
    R = tq * MLA_HEADS
    return pl.pallas_call(
        functools.partial(_mla_prompt_body, tq=tq, tk=tk, scale=(MLA_NOPE + MLA_ROPE) ** -0.5),
        grid_spec=pltpu.PrefetchScalarGridSpec(
            num_scalar_prefetch=2, grid=(B, len(pairs)),
            in_specs=[pl.BlockSpec((None, R, W), lambda b, p, qi, kj: (b, qi[p], 0)),
                      pl.BlockSpec((None, tk, W), lambda b, p, qi, kj: (b, kj[p], 0))],
            out_specs=pl.BlockSpec((None, R, MLA_KV_LORA), lambda b, p, qi, kj: (b, qi[p], 0)),
            scratch_shapes=[pltpu.VMEM((R, 1), F32), pltpu.VMEM((R, 1), F32),
                            pltpu.VMEM((R, MLA_KV_LORA), F32)]),
        out_shape=jax.ShapeDtypeStruct((B, TH, MLA_KV_LORA), BF16),
        compiler_params=_cparams(("arbitrary", "arbitrary")),
        name="mla_prompt",
    )(qi_tab, kj_tab, qcat, kcat)


def _page_stream(pt_ref, streams, npc, n_pages):
    b, t = pl.program_id(0), pl.program_id(1)
    steps = pl.num_programs(1)
    n = b * steps + t
    slot = n % 2

    def each(bb, tt, sl, act):
        page0 = (tt * npc) % n_pages
        for pred, pool, buf, sem in streams:
            def go(pool=pool, buf=buf, sem=sem):
                for j in range(npc):
                    act(pltpu.make_async_copy(pool.at[pt_ref[bb, page0 + j]], buf.at[sl, j], sem.at[sl]))
            live = pred(tt)
            if live is True:
                go()
            else:
                pl.when(live)(go)

    @pl.when(n == 0)
    def _():
        each(b, t, slot, lambda cp: cp.start())

    @pl.when(n + 1 < pl.num_programs(0) * steps)
    def _():
        wrap = t + 1 == steps
        each(jnp.where(wrap, b + 1, b), jnp.where(wrap, 0, t + 1), 1 - slot, lambda cp: cp.start())

    each(b, t, slot, lambda cp: cp.wait())
    return slot


def _mla_sample_body(pt_ref, q_ref, kn_ref, poolc_ref, poolr_ref, o_ref,
                     cbuf, rbuf, semc, semr, m_ref, l_ref, acc_ref, *, npc, n_pages, scale):
    c = pl.program_id(1)
    nch = n_pages // npc
    always = lambda t: True
    slot = _page_stream(pt_ref, [(always, poolc_ref, cbuf, semc), (always, poolr_ref, rbuf, semr)],
                        npc, n_pages)

    @pl.when(c == 0)
    def _():
        m_ref[...] = jnp.full(m_ref.shape, NEG, F32)
        l_ref[...] = jnp.zeros(l_ref.shape, F32)
        acc_ref[...] = jnp.zeros(acc_ref.shape, F32)

    q = q_ref[...]
    q_lat, q_rope = q[:, :MLA_KV_LORA], q[:, MLA_KV_LORA:]
    ckv = cbuf[slot].reshape(npc * PAGE_SIZE, MLA_KV_LORA).astype(BF16)
    rt = jnp.concatenate([rbuf[slot, j] for j in range(npc)], axis=1)
    rt = jnp.concatenate([rt, jnp.zeros((LANES - MLA_ROPE, rt.shape[1]), F32)], axis=0).astype(BF16)
    s = lax.dot_general(q_lat, ckv, _NT, preferred_element_type=F32)
    s = (s + jnp.dot(q_rope, rt, preferred_element_type=F32)) * scale
    _online_softmax_update(s, ckv, m_ref, l_ref, acc_ref)

    @pl.when(c == nch - 1)
    def _():
        kn = kn_ref[...]
        s = lax.dot_general(q, kn, _NT, preferred_element_type=F32) * scale
        qs = lax.broadcasted_iota(jnp.int32, s.shape, 0) // MLA_HEADS
        col = lax.broadcasted_iota(jnp.int32, s.shape, 1)
        s = jnp.where(col <= qs, s, NEG)
        _online_softmax_update(s, kn[:, :MLA_KV_LORA], m_ref, l_ref, acc_ref)
        o_ref[...] = (acc_ref[...] / l_ref[...]).astype(o_ref.dtype)


def mla_sample_call(page_table, qcat, kcat_new, pool_c, pool_rt, npc=16):
    B, R, W = qcat.shape
    n_pages = page_table.shape[1]
    nch = n_pages // npc
    assert n_pages % npc == 0
    return pl.pallas_call(
        functools.partial(_mla_sample_body, npc=npc, n_pages=n_pages, scale=(MLA_NOPE + MLA_ROPE) ** -0.5),
        grid_spec=pltpu.PrefetchScalarGridSpec(
            num_scalar_prefetch=1, grid=(B, nch),
            in_specs=[pl.BlockSpec((None, R, W), lambda b, c, pt: (b, 0, 0)),
                      pl.BlockSpec((None,) + kcat_new.shape[1:], lambda b, c, pt: (b, 0, 0)),
                      pl.BlockSpec(memory_space=pl.ANY), pl.BlockSpec(memory_space=pl.ANY)],
            out_specs=pl.BlockSpec((None, R, MLA_KV_LORA), lambda b, c, pt: (b, 0, 0)),
            scratch_shapes=[pltpu.VMEM((2, npc, PAGE_SIZE, MLA_KV_LORA), F32),
                            pltpu.VMEM((2, npc, MLA_ROPE, PAGE_SIZE), F32),
                            pltpu.SemaphoreType.DMA((2,)), pltpu.SemaphoreType.DMA((2,)),
                            pltpu.VMEM((R, 1), F32), pltpu.VMEM((R, 1), F32),
                            pltpu.VMEM((R, MLA_KV_LORA), F32)]),
        out_shape=jax.ShapeDtypeStruct((B, R, MLA_KV_LORA), BF16),
        compiler_params=_cparams(("arbitrary", "arbitrary")),
        name="mla_sample",
    )(page_table, qcat, kcat_new, pool_c, pool_rt)


def _kmean_body(k_ref, o_ref):
    k = k_ref[...]
    nb = k.shape[0] // MB_BLOCK
    o_ref[...] = jnp.mean(k.reshape(nb, MB_BLOCK, k.shape[1]), axis=1)


def kmean_call(k):
    B, T, W = k.shape
    nb = T // MB_BLOCK
    return pl.pallas_call(
        _kmean_body, grid=(B,),
        in_specs=[pl.BlockSpec((None, T, W), lambda b: (b, 0, 0))],
        out_specs=pl.BlockSpec((None, nb, W), lambda b: (b, 0, 0)),
        out_shape=jax.ShapeDtypeStruct((B, nb, W), F32),
        compiler_params=_cparams(("arbitrary",)),
        name="kmean",
    )(k)


def _moba_prompt_body(qi_ref, kb_ref, last_ref, q_ref, k_ref, v_ref, km_ref, o_ref,
                      sel_ref, m_ref, l_ref, acc_ref, *, nb):
    p = pl.program_id(1)
    qi, kb = qi_ref[p], kb_ref[p]
    own = kb == qi
    q = q_ref[...]
    lane = lax.broadcasted_iota(jnp.int32, (MB_BLOCK, LANES), 1)

    @pl.when(own)
    def _():
        gate = jnp.dot(q, km_ref[...], preferred_element_type=F32, precision=lax.Precision.HIGHEST)
        j = lane % nb
        valid = (j < qi) & (lane < MB_HEADS * nb)
        g = jnp.where(valid, gate, -jnp.inf)
        rank = jnp.zeros(g.shape, jnp.int32)
        for d in range(1, nb):
            up = pltpu.roll(g, LANES - d, 1)
            rank += jnp.where((j + d < nb) & (up > g), 1, 0)
            dn = pltpu.roll(g, d, 1)
            rank += jnp.where((j - d >= 0) & (dn >= g), 1, 0)
        sel_ref[...] = jnp.where(valid & (rank < MB_TOPK), 1.0, 0.0)
        m_ref[...] = jnp.full(m_ref.shape, NEG, F32)
        l_ref[...] = jnp.zeros(l_ref.shape, F32)
        acc_ref[...] = jnp.zeros(acc_ref.shape, F32)

    kb16 = k_ref[...].astype(BF16)
    vb16 = v_ref[...].astype(BF16)
    dist = ((qi - kb) * MB_BLOCK + lax.broadcasted_iota(jnp.int32, (MB_BLOCK, MB_BLOCK), 0)
            - lax.broadcasted_iota(jnp.int32, (MB_BLOCK, MB_BLOCK), 1))
    distf = dist.astype(F32)
    sel = sel_ref[...]
    for h in range(MB_HEADS):
        g = h // MB_GROUP
        hs = slice(h * HEAD_DIM, (h + 1) * HEAD_DIM)
        gs = slice(g * HEAD_DIM, (g + 1) * HEAD_DIM)
        s = lax.dot_general(q[:, hs].astype(BF16), kb16[:, gs], _NT, preferred_element_type=F32)
        s = s * (HEAD_DIM ** -0.5) - (2.0 ** (-8.0 * (h + 1) / MB_HEADS)) * distf
        picked = jnp.sum(jnp.where(lane == h * nb + kb, sel, 0.0), axis=-1, keepdims=True) > 0.5
        s = jnp.where(own, jnp.where(dist >= 0, s, NEG), jnp.where(picked, s, NEG))
        _online_softmax_update(s, vb16[:, gs], m_ref.at[h], l_ref.at[h], acc_ref.at[h])

    @pl.when(last_ref[p] == 1)
    def _():
        o_ref[...] = jnp.concatenate([acc_ref[h] / l_ref[h] for h in range(MB_HEADS)], axis=-1)


def moba_prompt_call(q, k, v, km):
    B, T, Wq = q.shape
    Wk = k.shape[2]
    nb = T // MB_BLOCK
    steps = [(i, kb) for i in range(nb) for kb in [i] + list(range(i))]
    qi_tab = jnp.asarray([s[0] for s in steps], jnp.int32)
    kb_tab = jnp.asarray([s[1] for s in steps], jnp.int32)
    last_tab = jnp.asarray([1 if (i == 0 or kb == i - 1) else 0 for i, kb in steps], jnp.int32)
    return pl.pallas_call(
        functools.partial(_moba_prompt_body, nb=nb),
        grid_spec=pltpu.PrefetchScalarGridSpec(
            num_scalar_prefetch=3, grid=(B, len(steps)),
            in_specs=[pl.BlockSpec((None, MB_BLOCK, Wq), lambda b, p, qi, kb, la: (b, qi[p], 0)),
                      pl.BlockSpec((None, MB_BLOCK, Wk), lambda b, p, qi, kb, la: (b, kb[p], 0)),
                      pl.BlockSpec((None, MB_BLOCK, Wk), lambda b, p, qi, kb, la: (b, kb[p], 0)),
                      pl.BlockSpec((None, Wq, LANES), lambda b, p, qi, kb, la: (b, 0, 0))],
            out_specs=pl.BlockSpec((None, MB_BLOCK, Wq), lambda b, p, qi, kb, la: (b, qi[p], 0)),
            scratch_shapes=[pltpu.VMEM((MB_BLOCK, LANES), F32),
                            pltpu.VMEM((MB_HEADS, MB_BLOCK, 1), F32),
                            pltpu.VMEM((MB_HEADS, MB_BLOCK, 1), F32),
                            pltpu.VMEM((MB_HEADS, MB_BLOCK, HEAD_DIM), F32)]),
        out_shape=jax.ShapeDtypeStruct((B, T, Wq), F32),
        compiler_params=_cparams(("arbitrary", "arbitrary")),
        name="moba_prompt",
    )(qi_tab, kb_tab, last_tab, q, k, v, km)


def _gate_matrix(kmean):
    B, nb, _ = kmean.shape
    kvh = np.arange(MB_HEADS) // MB_GROUP
    km = kmean.reshape(B, nb, MB_KV_HEADS, HEAD_DIM)[:, :, kvh, :].transpose(0, 2, 3, 1)
    eye = jnp.eye(MB_HEADS, dtype=F32)
    km = (km[:, :, :, None, :] * eye[None, :, None, :, None]).reshape(B, MB_HEADS * HEAD_DIM, MB_HEADS * nb)
    return jnp.pad(km, ((0, 0), (0, 0), (0, LANES - MB_HEADS * nb)))


def _moba_sample_body(pt_ref, qbd_ref, kn_ref, vn_ref, rc_ref, poolk_ref, poolv_ref, o_ref,
                      buf, sem, s_ref, p_ref, sel_ref, l_ref, acc_ref, *, npc, n_pages):
    t = pl.program_id(1)
    nch = n_pages // npc
    past = n_pages * PAGE_SIZE
    ph, c = t // nch, t % nch
    nb = past // MB_BLOCK
    ck = npc * PAGE_SIZE
    scale = HEAD_DIM ** -0.5
    R = qbd_ref.shape[0]
    slope = rc_ref[:, 0:1]
    qs = rc_ref[:, 1:2]
    slot = _page_stream(pt_ref, [(lambda tt: tt < nch, poolk_ref, buf, sem),
                                 (lambda tt: tt >= nch, poolv_ref, buf, sem)], npc, n_pages)

    def chunk_t():
        return jnp.concatenate([buf[slot, j] for j in range(npc)], axis=1).astype(BF16)

    @pl.when(ph == 0)
    def _():
        s = jnp.dot(qbd_ref[...].astype(BF16), chunk_t(), preferred_element_type=F32)
        s_ref[:, pl.ds(pl.multiple_of(c * ck, ck), ck)] = s

    @pl.when(t == nch - 1)
    def _():
        lane = lax.broadcasted_iota(jnp.int32, sel_ref.shape, 1)

        def gate_col(j, g):
            sj = s_ref[:, pl.ds(pl.multiple_of(j * MB_BLOCK, MB_BLOCK), MB_BLOCK)]
            return jnp.where(lane == j, jnp.sum(sj, axis=-1, keepdims=True), g)

        g = lax.fori_loop(0, nb, gate_col, jnp.zeros(sel_ref.shape, F32))
        sel = jnp.zeros(g.shape, F32)
        for _ in range(MB_TOPK):
            mx = jnp.max(g, axis=-1, keepdims=True)
            idx = jnp.min(jnp.where(g == mx, lane, nb), axis=-1, keepdims=True)
            hit = lane == idx
            sel = jnp.where(hit, 1.0, sel)
            g = jnp.where(hit, -jnp.inf, g)
        sel_ref[...] = sel

        qb = qbd_ref[...].astype(BF16)
        kn = kn_ref[...].astype(BF16)
        s_own = lax.dot_general(qb, kn, _NT, preferred_element_type=F32) * scale
        col = lax.broadcasted_iota(jnp.int32, s_own.shape, 1).astype(F32)
        s_own = jnp.where(col <= qs, s_own - slope * (qs - col), NEG)
        m0 = jnp.max(s_own, axis=-1, keepdims=True)

        kcol = lax.broadcasted_iota(jnp.int32, (R, MB_BLOCK), 1).astype(F32)

        def block_scores(j):
            s = s_ref[:, pl.ds(pl.multiple_of(j * MB_BLOCK, MB_BLOCK), MB_BLOCK)]
            kpos = jnp.asarray(j * MB_BLOCK).astype(F32) + kcol
            s = s * scale - slope * ((past + qs) - kpos)
            picked = jnp.sum(jnp.where(lane == j, sel, 0.0), axis=-1, keepdims=True) > 0.5
            return jnp.where(picked, s, NEG)

        m = lax.fori_loop(0, nb, lambda j, m: jnp.maximum(m, jnp.max(block_scores(j), axis=-1, keepdims=True)), m0)

        def fill(j, l):
            p = jnp.exp(block_scores(j) - m)
            p_ref[:, pl.ds(pl.multiple_of(j * MB_BLOCK, MB_BLOCK), MB_BLOCK)] = p.astype(BF16)
            return l + jnp.sum(p, axis=-1, keepdims=True)

        p_own = jnp.exp(s_own - m)
        l_ref[...] = lax.fori_loop(0, nb, fill, jnp.sum(p_own, axis=-1, keepdims=True))
        acc_ref[...] = jnp.dot(p_own.astype(BF16), vn_ref[...].astype(BF16), preferred_element_type=F32)

    @pl.when(ph == 1)
    def _():
        pb = p_ref[:, pl.ds(pl.multiple_of(c * ck, ck), ck)]
        acc_ref[...] += lax.dot_general(pb, chunk_t(), _NT, preferred_element_type=F32)

    @pl.when(t == 2 * nch - 1)
    def _():
        out = acc_ref[...] / l_ref[...]
        S = R // MB_HEADS
        o_ref[...] = jnp.concatenate(
            [out[h * S:(h + 1) * S, (h // MB_GROUP) * HEAD_DIM:(h // MB_GROUP + 1) * HEAD_DIM]
             for h in range(MB_HEADS)], axis=-1)


def moba_sample_call(page_table, q, k_new, v_new, pool_kt, pool_vt, npc=16):
    B, S, Wq = q.shape
    Wk = k_new.shape[2]
    n_pages = page_table.shape[1]
    past = n_pages * PAGE_SIZE
    nch = n_pages // npc
    assert past % MB_BLOCK == 0 and past // MB_BLOCK >= MB_TOPK and n_pages % npc == 0
    R = MB_HEADS * S
    kvh = np.arange(MB_HEADS) // MB_GROUP
    onehot = jnp.asarray(np.eye(MB_KV_HEADS, dtype=np.float32)[kvh])
    qh = q.reshape(B, S, MB_HEADS, HEAD_DIM).transpose(0, 2, 1, 3)
    qbd = (qh[:, :, :, None, :] * onehot[None, :, None, :, None]).reshape(B, R, Wk)
    pad = lambda t: jnp.pad(t, ((0, 0), (0, LANES - S), (0, 0)))
    rc = np.zeros((R, LANES), np.float32)
    rc[:, 0] = np.repeat([2.0 ** (-8.0 * (h + 1) / MB_HEADS) for h in range(MB_HEADS)], S)
    rc[:, 1] = np.tile(np.arange(S), MB_HEADS)
    full = lambda shp: pl.BlockSpec((None,) + shp, lambda b, t, pt: (b, 0, 0))
    return pl.pallas_call(
        functools.partial(_moba_sample_body, npc=npc, n_pages=n_pages),
        grid_spec=pltpu.PrefetchScalarGridSpec(
            num_scalar_prefetch=1, grid=(B, 2 * nch),
            in_specs=[full((R, Wk)), full((LANES, Wk)), full((LANES, Wk)),
                      pl.BlockSpec((R, LANES), lambda b, t, pt: (0, 0)),
                      pl.BlockSpec(memory_space=pl.ANY), pl.BlockSpec(memory_space=pl.ANY)],
            out_specs=full((S, Wq)),
            scratch_shapes=[pltpu.VMEM((2, npc, Wk, PAGE_SIZE), F32), pltpu.SemaphoreType.DMA((2,)),
                            pltpu.VMEM((R, past), F32), pltpu.VMEM((R, past), BF16),
                            pltpu.VMEM((R, past // MB_BLOCK), F32),
                            pltpu.VMEM((R, 1), F32), pltpu.VMEM((R, Wk), F32)]),
        out_shape=jax.ShapeDtypeStruct((B, S, Wq), F32),
        compiler_params=_cparams(("arbitrary", "arbitrary")),
        name="moba_sample",
    )(page_table, qbd, pad(k_new), pad(v_new), jnp.asarray(rc), pool_kt, pool_vt)


def _l2norm(x):
    return x * lax.rsqrt(jnp.sum(x * x, axis=-1, keepdims=True) + 1e-12)


def _rope_tables(pos):
    half = MLA_ROPE // 2
    inv = ROPE_THETA ** (-jnp.arange(half, dtype=F32) / half)
    ang = pos.astype(F32)[:, None] * inv[None, :]
    return jnp.cos(ang), jnp.sin(ang)


def _rwkv_mix(p, prev, S0, prm, lanes_bh):
    mu, w0, w2, a0, a2, g2, k_k, k_a, r_k, lnx_g, lnx_b = prm
    B, T, _ = p.shape
    H = RW_HEADS
    p_prev = jnp.concatenate([prev[:, None, :], p[:, :-1]], axis=1)
    pm = p + (p_prev - p) * mu
    cuts = [int(c) for c in np.cumsum(RW_SPLITS)[:-1]]
    r, wl, k, v, al, gl = jnp.split(pm, cuts, axis=-1)
    w_log = -jax.nn.softplus(-(w0 + jnp.tanh(wl) @ w2)) - 0.5
    decay = jnp.exp(-jnp.exp(w_log))
    a = jax.nn.sigmoid(a0 + al @ a2)
    g = jax.nn.sigmoid(gl) @ g2
    heads = lambda t: t.reshape(B, T, H, HEAD_DIM)
    kk = _l2norm(heads(k * k_k))
    k = k * (1.0 + (a - 1.0) * k_a)
    r_h, k_h, v_h, w_h, a_h = heads(r), heads(k), heads(v), heads(decay), heads(a)
    aa, bb = -kk, kk * a_h
    ts = lambda t: _to_scan(t, lanes_bh)
    s0 = _state_to_scan(jnp.swapaxes(S0, -1, -2), lanes_bh)
    y, S = scan_call(ts(r_h), ts(w_h), ts(k_h), ts(v_h), ts(aa), ts(bb), s0, "layer")
    y = _from_scan(y, B, H, lanes_bh).reshape(B, T, RW_WIDTH) * lnx_g + lnx_b
    S = jnp.swapaxes(_state_from_scan(S, B, H, lanes_bh), -1, -2)
    bonus = jnp.sum(r_h * k_h * r_k, axis=-1, keepdims=True) * v_h
    y = (y + bonus.reshape(B, T, RW_WIDTH)) * g
    return y, S, p[:, -1]


def _gdn_mix(qkv, z, a_raw, b_raw, conv_buf, S0, prm, lanes_bh):
    conv_w, a_log, dt_bias, norm_g = prm
    B, T, _ = qkv.shape
    H = GD_HEADS
    xx = jnp.concatenate([conv_buf, qkv], axis=1)
    y = sum(xx[:, j:j + T] * conv_w[j] for j in range(GD_CONV))
    y = _silu(y)
    q, k, v = jnp.split(y, [H * GD_DK, 2 * H * GD_DK], axis=-1)
    q = _l2norm(q.reshape(B, T, H, GD_DK)) * (GD_DK ** -0.5)
    k = _l2norm(k.reshape(B, T, H, GD_DK))
    v = v.reshape(B, T, H, GD_DV)
    beta = jax.nn.sigmoid(b_raw)[..., None]
    alpha = jnp.exp(-jnp.exp(a_log) * jax.nn.softplus(a_raw + dt_bias))[..., None]
    ts = lambda t: _to_scan(t, lanes_bh)
    w = jnp.broadcast_to(alpha, k.shape)
    y, S = scan_call(ts(q), ts(w), ts(k), ts(beta * v), ts(k), ts(-beta * alpha * k),
                     _state_to_scan(S0, lanes_bh), "rms")
    o = _from_scan(y, B, H, lanes_bh) * norm_g * _silu(z.reshape(B, T, H, GD_DV))
    S = _state_from_scan(S, B, H, lanes_bh)
    return o.reshape(B, T, H * GD_DV), S, xx[:, -(GD_CONV - 1):]


def _bdiag(w):
    H, a, b = w.shape
    eye = jnp.eye(H, dtype=w.dtype)
    return (eye[:, None, :, None] * w[:, :, None, :]).reshape(H * a, H * b)


def kernel(x_prompt, x_sample, cache_moba_k, cache_moba_v, cache_mla_ckv, cache_mla_krope, state_rwkv_wkv, state_rwkv_shift, state_gdn, state_gdn_conv, page_table, c_prompt, c_sample, w_ada, b_ada, g_mix_pre, g_mix_post, g_ff_pre, g_ff_post, w_in_even, w_out_even, rw_mu, rw_w0, rw_w2, rw_a0, rw_a2, rw_g2, rw_kk, rw_ka, rw_rk, rw_lnx_g, rw_lnx_b, ffn_w_gate, ffn_w_up, ffn_w_down, w_in_odd, w_out_odd, mla_q_norm, mla_w_q_up, mla_kv_norm, mla_w_uk, mla_w_uv, gdn_conv_w, gdn_a_log, gdn_dt_bias, gdn_norm_g, moe_w_router, moe_b_router, moe_w_gate, moe_w_up, moe_w_down):
    Bp, Tp, D = x_prompt.shape
    Bs, Ts, _ = x_sample.shape
    past = page_table.shape[1] * PAGE_SIZE
    Ms = Bs * Ts
    row = lambda v: v.reshape(1, -1)

    xp = x_prompt
    xs = x_sample.reshape(1, Ms, D)
    c_all = jnp.concatenate([c_prompt, c_sample], axis=0)

    outs_p, outs_s = {}, {}
    for layer in range(w_ada.shape[0]):
        i = layer // 2
        mod = adaln_call(c_all, w_ada[layer].astype(BF16), row(b_ada[layer]))
        mp = [m.reshape(Bp, 1, D) for m in jnp.split(mod[:Bp], 6, axis=-1)]
        ms = [jnp.broadcast_to(m[:, None, :], (Bs, Ts, D)).reshape(1, Ms, D)
              for m in jnp.split(mod[Bp:], 6, axis=-1)]
        gpre, gpost = row(g_mix_pre[layer]), row(g_mix_post[layer])
        if layer % 2 == 0:
            w_in = w_in_even[i].astype(BF16)
            splits = (RW_PROJ, MB_HEADS * HEAD_DIM, MB_KV_HEADS * HEAD_DIM, MB_KV_HEADS * HEAD_DIM)
            rw_prm = (rw_mu[i], rw_w0[i], rw_w2[i], rw_a0[i], rw_a2[i], rw_g2[i], rw_kk[i], rw_ka[i],
                      rw_rk[i], rw_lnx_g[i], rw_lnx_b[i])
            w_out = w_out_even[i].astype(BF16)
            p_rw, q, k, v = modmm_call(xp, mp[0], mp[1], gpre, w_in, splits)
            y_rw, S_p, shift_p = _rwkv_mix(p_rw, jnp.zeros((Bp, RW_PROJ), F32),
                                           jnp.zeros((Bp, RW_HEADS, HEAD_DIM, HEAD_DIM), F32), rw_prm, True)
            k4 = k.reshape(Bp, Tp, MB_KV_HEADS, HEAD_DIM)
            v4 = v.reshape(Bp, Tp, MB_KV_HEADS, HEAD_DIM)
            y_mb = moba_prompt_call(q, k, v, _gate_matrix(kmean_call(k)))
            xp = outproj_call(y_rw, y_mb, w_out[:RW_WIDTH], w_out[RW_WIDTH:], xp, mp[2], gpost)
            outs_p.update(mk=k4[None], mv=v4[None], wkv=S_p[None], shift=shift_p[None])
            p_rw, q, k, v = modmm_call(xs, ms[0], ms[1], gpre, w_in, splits)
            y_rw, S_s, shift_s = _rwkv_mix(p_rw.reshape(Bs, Ts, RW_PROJ), state_rwkv_shift[i],
                                           state_rwkv_wkv[i], rw_prm, False)
            k4 = k.reshape(Bs, Ts, MB_KV_HEADS, HEAD_DIM)
            v4 = v.reshape(Bs, Ts, MB_KV_HEADS, HEAD_DIM)
            n_pool = cache_moba_k.shape[1]
            pool_t = lambda c: c[i].transpose(0, 2, 3, 1).reshape(n_pool, -1, PAGE_SIZE)
            y_mb = moba_sample_call(page_table, q.reshape(Bs, Ts, -1), k.reshape(Bs, Ts, -1), v.reshape(Bs, Ts, -1),
                                    pool_t(cache_moba_k), pool_t(cache_moba_v))
            xs = outproj_call(y_rw.reshape(1, Ms, RW_WIDTH), y_mb.reshape(1, Ms, -1),
                              w_out[:RW_WIDTH], w_out[RW_WIDTH:], xs, ms[2], gpost)
            outs_s.update(mk=k4[None], mv=v4[None], wkv=S_s[None], shift=shift_s[None])
        else:
            w_in = w_in_odd[i]
            cuts = [int(c) for c in np.cumsum(ODD_SPLITS)[:-1]]
            w_cq, w_ckv, w_kr, w_qkv, w_z, w_a, w_b = jnp.split(w_in, cuts, axis=1)
            padc = lambda w, n: jnp.pad(w, ((0, 0), (0, n - w.shape[1])))
            w_in_p = jnp.concatenate([w_cq, w_ckv, padc(w_kr, LANES), w_qkv, w_z,
                                      padc(jnp.concatenate([w_a, w_b], axis=1), LANES)], axis=1).astype(BF16)
            splits = (MLA_Q_LORA, MLA_KV_LORA, LANES, GD_QKV, GD_HEADS * GD_DV, LANES)
            w_out = w_out_odd[i].astype(BF16)
            gd_prm = (gdn_conv_w[i], gdn_a_log[i], gdn_dt_bias[i], gdn_norm_g[i])
            w_uv_bd = _bdiag(mla_w_uv[i].transpose(1, 0, 2)).astype(BF16)
            w_uk_bd = _bdiag(mla_w_uk[i].transpose(1, 2, 0)).astype(BF16)
            wq = mla_w_q_up[i].reshape(MLA_Q_LORA, MLA_HEADS, MLA_NOPE + MLA_ROPE)
            w_qn = wq[:, :, :MLA_NOPE].reshape(MLA_Q_LORA, -1).astype(BF16)
            w_qr = jnp.pad(wq[:, :, MLA_NOPE:], ((0, 0), (0, 0), (0, LANES - MLA_ROPE)))
            w_qr = w_qr.reshape(MLA_Q_LORA, -1).astype(BF16)
            w_oa, w_ob = w_out[:MLA_HEADS * MLA_V], w_out[MLA_HEADS * MLA_V:]

            def odd(x, m, B, T, cosf, sinf, conv_buf, S0, lanes_bh):
                cq, ckv, kr, qkv, z, ab = modmm_call(x, m[0], m[1], gpre, w_in_p, splits)
                qcat, kcat, ckv_n, k_rope = mla_prep_call(
                    cq, ckv, kr, cosf, sinf, row(mla_q_norm[i]), row(mla_kv_norm[i]), w_qn, w_qr, w_uk_bd)
                qkv, z = qkv.reshape(B, T, -1), z.reshape(B, T, -1)
                ab = ab.reshape(B, T, LANES)
                a_raw, b_raw = ab[..., :GD_HEADS], ab[..., GD_HEADS:2 * GD_HEADS]
                y_gdn, S, buf = _gdn_mix(qkv, z, a_raw, b_raw, conv_buf, S0, gd_prm, lanes_bh)
                ckv_n = ckv_n.reshape(B, T, MLA_KV_LORA)
                k_rope = k_rope.reshape(B, T, LANES)[..., :MLA_ROPE]
                return qcat, kcat, y_gdn, ckv_n, k_rope, S, buf

            cosf, sinf = _rope_rows(jnp.arange(Tp))
            qcat, kcat, y_gdn, ckv_n, k_rope, S_p, buf_p = odd(
                xp, mp, Bp, Tp, cosf, sinf, jnp.zeros((Bp, GD_CONV - 1, GD_QKV), F32),
                jnp.zeros((Bp, GD_HEADS, GD_DK, GD_DV), F32), True)
            o_lat = mla_prompt_call(qcat.reshape(Bp, Tp * MLA_HEADS, 2 * LANES), kcat)
            xp = outproj_call(o_lat.reshape(Bp, Tp, -1), y_gdn, w_oa, w_ob, xp, mp[2], gpost, wpre=w_uv_bd)
            outs_p.update(ckv=ckv_n[None], krope=k_rope[None], gdn=S_p[None], conv=buf_p[None])
            cosf, sinf = _rope_rows(jnp.tile(past + jnp.arange(Ts), Bs))
            qcat, kcat, y_gdn, ckv_n, k_rope, S_s, buf_s = odd(
                xs, ms, Bs, Ts, cosf, sinf, state_gdn_conv[i], state_gdn[i], False)
            kn = jnp.pad(kcat.reshape(Bs, Ts, 2 * LANES), ((0, 0), (0, LANES - Ts), (0, 0)))
            o_lat = mla_sample_call(page_table, qcat.reshape(Bs, Ts * MLA_HEADS, 2 * LANES), kn,
                                    cache_mla_ckv[i], cache_mla_krope[i].transpose(0, 2, 1))
            xs = outproj_call(o_lat.reshape(1, Ms, -1), y_gdn.reshape(1, Ms, -1), w_oa, w_ob, xs, ms[2], gpost,
                              wpre=w_uv_bd)
            outs_s.update(ckv=ckv_n[None], krope=k_rope[None], gdn=S_s[None], conv=buf_s[None])

        gpre, gpost = row(g_ff_pre[layer]), row(g_ff_post[layer])
        if layer % 2 == 0:
            J = 2
            Fd = ffn_w_gate.shape[2] // J
            wg = ffn_w_gate[i].astype(BF16).reshape(D, J, Fd).transpose(1, 0, 2)
            wu = ffn_w_up[i].astype(BF16).reshape(D, J, Fd).transpose(1, 0, 2)
            wd = ffn_w_down[i].astype(BF16).reshape(J, Fd, D)
            xp = ffn_call(xp, mp[3], mp[4], mp[5], gpre, gpost, wg, wu, wd)
            xs = ffn_call(xs, ms[3], ms[4], ms[5], gpre, gpost, wg, wu, wd)
        else:
            wr = jnp.pad(moe_w_router[i], ((0, 0), (0, LANES - N_EXPERTS)))
            br = jnp.pad(row(moe_b_router[i]), ((0, 0), (0, LANES - N_EXPERTS)))
            wg, wu, wd = (w[i].astype(BF16) for w in (moe_w_gate, moe_w_up, moe_w_down))
            xp = ffn_call(xp, mp[3], mp[4], mp[5], gpre, gpost, wg, wu, wd, wr, br)
            xs = ffn_call(xs, ms[3], ms[4], ms[5], gpre, gpost, wg, wu, wd, wr, br)

    names = ('mk', 'mv', 'wkv', 'shift', 'ckv', 'krope', 'gdn', 'conv')
    return ((xp, xs.reshape(Bs, Ts, D)) + tuple(outs_p[n] for n in names)
            + tuple(outs_s[n] for n in names))
```

```python
import functools
import math

import jax
import jax.numpy as jnp
import numpy as np
from jax import lax
from jax.experimental import pallas as pl
from jax.experimental.pallas import tpu as pltpu

F32 = jnp.float32
BF16 = jnp.bfloat16

NORM_EPS = 1e-6
HEAD_DIM = 64
PAGE_SIZE = 128
RW_HEADS = 8
RW_WIDTH = RW_HEADS * HEAD_DIM
RW_SPLITS = (RW_WIDTH, 64, RW_WIDTH, RW_WIDTH, 64, 128)
RW_PROJ = sum(RW_SPLITS)
RW_LNX_EPS = 64e-5
MB_HEADS = 8
MB_KV_HEADS = 4
MB_GROUP = MB_HEADS // MB_KV_HEADS
MB_BLOCK = 256
MB_TOPK = 3
MLA_HEADS = 8
MLA_NOPE = 64
MLA_ROPE = 32
MLA_V = 64
MLA_Q_LORA = 256
MLA_KV_LORA = 128
ROPE_THETA = 10000.0
GD_HEADS = 8
GD_DK = 64
GD_DV = 64
GD_CONV = 4
GD_QKV = GD_HEADS * (2 * GD_DK + GD_DV)
ODD_SPLITS = (MLA_Q_LORA, MLA_KV_LORA, MLA_ROPE, GD_QKV, GD_HEADS * GD_DV, GD_HEADS, GD_HEADS)
N_EXPERTS = 8

LANES = 128
VMEM_LIMIT = 56 * 1024 * 1024


def _cparams(sem):
    return pltpu.CompilerParams(dimension_semantics=sem, vmem_limit_bytes=VMEM_LIMIT)


def _silu(x):
    return x * jax.nn.sigmoid(x)


def _rms(x, eps=NORM_EPS):
    return x * lax.rsqrt(jnp.mean(x * x, axis=-1, keepdims=True) + eps)


def _adaln_body(c_ref, w_ref, b_ref, o_ref):
    c = c_ref[...]
    o_ref[...] = jnp.dot(_silu(c).astype(BF16), w_ref[...], preferred_element_type=F32) + b_ref[...]


def adaln_call(c, w, b, tn=1536):
    R, D = c.shape
    N = w.shape[1]
    return pl.pallas_call(
        _adaln_body,
        grid=(N // tn,),
        in_specs=[pl.BlockSpec((R, D), lambda j: (0, 0)),
                  pl.BlockSpec((D, tn), lambda j: (0, j)),
                  pl.BlockSpec((1, tn), lambda j: (0, j))],
        out_specs=pl.BlockSpec((R, tn), lambda j: (0, j)),
        out_shape=jax.ShapeDtypeStruct((R, N), F32),
        compiler_params=_cparams(("arbitrary",)),
        name="adaln",
    )(c, w, b)


def _mod_spec(mod, tm):
    if mod.shape[1] == 1:
        return pl.BlockSpec((None, 1, mod.shape[2]), lambda g, i, *_: (g, 0, 0))
    return pl.BlockSpec((None, tm, mod.shape[2]), lambda g, i, *_: (g, i, 0))


def _modmm_body(x_ref, sh_ref, sc_ref, g_ref, w_ref, *o_refs, splits):
    h = _rms(x_ref[...]) * g_ref[...] * (1.0 + sc_ref[...]) + sh_ref[...]
    y = jnp.dot(h.astype(BF16), w_ref[...], preferred_element_type=F32)
    off = 0
    for o_ref, n in zip(o_refs, splits):
        o_ref[...] = y[:, off:off + n].astype(o_ref.dtype)
        off += n


def modmm_call(x, shift, scale, gain, w, splits, tm=512):
    G, R, D = x.shape
    tm = min(tm, R)
    N = w.shape[1]
    assert sum(splits) == N and all(s % LANES == 0 for s in splits)
    return pl.pallas_call(
        functools.partial(_modmm_body, splits=splits),
        grid=(G, R // tm),
        in_specs=[pl.BlockSpec((None, tm, D), lambda g, i: (g, i, 0)),
                  _mod_spec(shift, tm), _mod_spec(scale, tm),
                  pl.BlockSpec((1, D), lambda g, i: (0, 0)),
                  pl.BlockSpec((D, N), lambda g, i: (0, 0))],
        out_specs=[pl.BlockSpec((None, tm, n), lambda g, i: (g, i, 0)) for n in splits],
        out_shape=[jax.ShapeDtypeStruct((G, R, n), F32) for n in splits],
        compiler_params=_cparams(("arbitrary", "arbitrary")),
        name="modmm",
    )(x, shift, scale, gain, w)


def _outproj_body(ya_ref, yb_ref, wpre_ref, wa_ref, wb_ref, x_ref, ga_ref, gp_ref, o_ref, *, pre):
    ya = ya_ref[...].astype(BF16)
    if pre:
        ya = jnp.dot(ya, wpre_ref[...], preferred_element_type=F32).astype(BF16)
    y = jnp.dot(ya, wa_ref[...], preferred_element_type=F32)
    y = y + jnp.dot(yb_ref[...].astype(BF16), wb_ref[...], preferred_element_type=F32)
    o_ref[...] = x_ref[...] + ga_ref[...] * (_rms(y) * gp_ref[...])


def outproj_call(ya, yb, wa, wb, x, gate, gpost, wpre=None, tm=512):
    G, R, D = x.shape
    tm = min(tm, R)
    Ka, Kb = ya.shape[2], yb.shape[2]
    pre = wpre is not None
    if not pre:
        wpre = jnp.zeros((8, LANES), BF16)
    return pl.pallas_call(
        functools.partial(_outproj_body, pre=pre),
        grid=(G, R // tm),
        in_specs=[pl.BlockSpec((None, tm, Ka), lambda g, i: (g, i, 0)),
                  pl.BlockSpec((None, tm, Kb), lambda g, i: (g, i, 0)),
                  pl.BlockSpec(wpre.shape, lambda g, i: (0, 0)),
                  pl.BlockSpec(wa.shape, lambda g, i: (0, 0)),
                  pl.BlockSpec(wb.shape, lambda g, i: (0, 0)),
                  pl.BlockSpec((None, tm, D), lambda g, i: (g, i, 0)),
                  _mod_spec(gate, tm),
                  pl.BlockSpec((1, D), lambda g, i: (0, 0))],
        out_specs=pl.BlockSpec((None, tm, D), lambda g, i: (g, i, 0)),
        out_shape=jax.ShapeDtypeStruct((G, R, D), F32),
        compiler_params=_cparams(("arbitrary", "arbitrary")),
        name="outproj",
    )(ya, yb, wpre, wa, wb, x, gate, gpost)


def _mla_prep_body(cq_ref, ckv_ref, kr_ref, cos_ref, sin_ref, qg_ref, kg_ref, wqn_ref, wqr_ref, wuk_ref,
                   qcat_ref, kcat_ref, ckvn_ref, krope_ref):
    cosf, sinf = cos_ref[...], sin_ref[...]
    lane = lax.broadcasted_iota(jnp.int32, cosf.shape, 1)
    half = MLA_ROPE // 2

    def rope(x):
        rot = jnp.where(lane < half, pltpu.roll(x, LANES - half, 1), pltpu.roll(x, half, 1))
        return x * cosf + rot * sinf

    cqn = (_rms(cq_ref[...]) * qg_ref[...]).astype(BF16)
    qn = jnp.dot(cqn, wqn_ref[...], preferred_element_type=F32).astype(BF16)
    qr = jnp.dot(cqn, wqr_ref[...], preferred_element_type=F32)
    ql = jnp.dot(qn, wuk_ref[...], preferred_element_type=F32)
    for h in range(MLA_HEADS):
        qcat_ref[:, 2 * h * LANES:(2 * h + 1) * LANES] = ql[:, h * LANES:(h + 1) * LANES].astype(BF16)
        qcat_ref[:, (2 * h + 1) * LANES:(2 * h + 2) * LANES] = rope(qr[:, h * LANES:(h + 1) * LANES]).astype(BF16)
    ckvn = _rms(ckv_ref[...]) * kg_ref[...]
    krope = rope(kr_ref[...])
    ckvn_ref[...] = ckvn
    krope_ref[...] = krope
    kcat_ref[:, :LANES] = ckvn.astype(BF16)
    kcat_ref[:, LANES:] = krope.astype(BF16)


def mla_prep_call(cq, ckv, kr, cosf, sinf, q_gain, kv_gain, w_qn, w_qr, w_uk_bd, tm=512):
    G, R, _ = cq.shape
    tm = min(tm, R)
    rows = lambda n: pl.BlockSpec((None, tm, n), lambda g, i: (g, i, 0))
    const = lambda a: pl.BlockSpec(a.shape, lambda g, i: (0, 0))
    W = 2 * LANES * MLA_HEADS
    return pl.pallas_call(
        _mla_prep_body,
        grid=(G, R // tm),
        in_specs=[rows(cq.shape[2]), rows(LANES), rows(LANES),
                  pl.BlockSpec((tm, LANES), lambda g, i: (i, 0)), pl.BlockSpec((tm, LANES), lambda g, i: (i, 0)),
                  const(q_gain), const(kv_gain), const(w_qn), const(w_qr), const(w_uk_bd)],
        out_specs=[rows(W), rows(2 * LANES), rows(LANES), rows(LANES)],
        out_shape=[jax.ShapeDtypeStruct((G, R, W), BF16), jax.ShapeDtypeStruct((G, R, 2 * LANES), BF16),
                   jax.ShapeDtypeStruct((G, R, LANES), F32), jax.ShapeDtypeStruct((G, R, LANES), F32)],
        compiler_params=_cparams(("arbitrary", "arbitrary")),
        name="mla_prep",
    )(cq, ckv, kr, cosf, sinf, q_gain, kv_gain, w_qn, w_qr, w_uk_bd)


def _rope_rows(pos):
    cos, sin = _rope_tables(pos)
    z = jnp.zeros((pos.shape[0], LANES - MLA_ROPE), F32)
    return jnp.concatenate([cos, cos, z], axis=1), jnp.concatenate([-sin, sin, z], axis=1)


def _ffn_body(x_ref, sh_ref, sc_ref, ga_ref, gpre_ref, gpost_ref, wr_ref, br_ref,
              wg_ref, wu_ref, wd_ref, o_ref, h_ref, acc_ref, rw_ref, *, moe):
    j = pl.program_id(2)

    @pl.when(j == 0)
    def _():
        h = _rms(x_ref[...]) * gpre_ref[...] * (1.0 + sc_ref[...]) + sh_ref[...]
        h_ref[...] = h.astype(BF16)
        acc_ref[...] = jnp.zeros_like(acc_ref)
        if moe:
            logits = jnp.dot(h, wr_ref[...], preferred_element_type=F32,
                             precision=lax.Precision.HIGHEST) + br_ref[...]
            lane = lax.broadcasted_iota(jnp.int32, logits.shape, 1)
            neg = jnp.float32(-jnp.inf)
            logits = jnp.where(lane < N_EXPERTS, logits, neg)
            m1 = jnp.max(logits, axis=-1, keepdims=True)
            i1 = jnp.min(jnp.where(logits == m1, lane, LANES), axis=-1, keepdims=True)
            rest = jnp.where(lane == i1, neg, logits)
            m2 = jnp.max(rest, axis=-1, keepdims=True)
            i2 = jnp.min(jnp.where(rest == m2, lane, LANES), axis=-1, keepdims=True)
            e2 = jnp.exp(m2 - m1)
            w1 = 1.0 / (1.0 + e2)
            w2 = e2 / (1.0 + e2)
            rw_ref[...] = jnp.where(lane == i1, w1, 0.0) + jnp.where(lane == i2, w2, 0.0)

    hb = h_ref[...]
    g = jnp.dot(hb, wg_ref[...], preferred_element_type=F32)
    u = jnp.dot(hb, wu_ref[...], preferred_element_type=F32)
    f = jnp.dot((_silu(g) * u).astype(BF16), wd_ref[...], preferred_element_type=F32)
    if moe:
        lane = lax.broadcasted_iota(jnp.int32, rw_ref.shape, 1)
        we = jnp.sum(jnp.where(lane == j, rw_ref[...], 0.0), axis=-1, keepdims=True)
        f = f * we
    acc_ref[...] += f

    @pl.when(j == pl.num_programs(2) - 1)
    def _():
        o_ref[...] = x_ref[...] + ga_ref[...] * (_rms(acc_ref[...]) * gpost_ref[...])


def ffn_call(x, shift, scale, gate, gpre, gpost, wg, wu, wd, w_router=None, b_router=None, tm=512):
    G, R, D = x.shape
    tm = min(tm, R)
    J, _, Fd = wg.shape
    moe = w_router is not None
    if not moe:
        w_router = jnp.zeros((D, LANES), F32)
        b_router = jnp.zeros((1, LANES), F32)
    return pl.pallas_call(
        functools.partial(_ffn_body, moe=moe),
        grid=(G, R // tm, J),
        in_specs=[pl.BlockSpec((None, tm, D), lambda g, i, j: (g, i, 0)),
                  _mod_spec(shift, tm), _mod_spec(scale, tm), _mod_spec(gate, tm),
                  pl.BlockSpec((1, D), lambda g, i, j: (0, 0)),
                  pl.BlockSpec((1, D), lambda g, i, j: (0, 0)),
                  pl.BlockSpec((D, LANES), lambda g, i, j: (0, 0)),
                  pl.BlockSpec((1, LANES), lambda g, i, j: (0, 0)),
                  pl.BlockSpec((None, D, Fd), lambda g, i, j: (j, 0, 0)),
                  pl.BlockSpec((None, D, Fd), lambda g, i, j: (j, 0, 0)),
                  pl.BlockSpec((None, Fd, D), lambda g, i, j: (j, 0, 0))],
        out_specs=pl.BlockSpec((None, tm, D), lambda g, i, j: (g, i, 0)),
        out_shape=jax.ShapeDtypeStruct((G, R, D), F32),
        scratch_shapes=[pltpu.VMEM((tm, D), BF16), pltpu.VMEM((tm, D), F32),
                        pltpu.VMEM((tm, LANES), F32)],
        compiler_params=_cparams(("arbitrary", "arbitrary", "arbitrary")),
        name="moe" if moe else "ffn",
    )(x, shift, scale, gate, gpre, gpost, w_router, b_router, wg, wu, wd)


def _scan_body(r_ref, w_ref, k_ref, v_ref, a_ref, b_ref, an_ref, s0_ref, y_ref, s_ref, sa_ref,
               *, tt, dk, norm):
    i = pl.program_id(1)

    @pl.when(i == 0)
    def _():
        s_ref[...] = s0_ref[...]
        sa = jnp.zeros(sa_ref.shape, F32)
        for k in range(dk):
            sa = sa + s0_ref[k] * a_ref[0, k:k + 1, :]
        sa_ref[...] = sa

    def step(t, sa):
        v_t = v_ref[t]
        tn = jnp.minimum(t + 1, tt - 1)
        last = t == tt - 1
        y = jnp.zeros(sa.shape, F32)
        sa_n = jnp.zeros(sa.shape, F32)
        for k in range(dk):
            new = (s_ref[k] * w_ref[t, k:k + 1, :] + sa * b_ref[t, k:k + 1, :]
                   + v_t * k_ref[t, k:k + 1, :])
            s_ref[k] = new
            y = y + new * r_ref[t, k:k + 1, :]
            a_next = jnp.where(last, an_ref[0, k:k + 1, :], a_ref[tn, k:k + 1, :])
            sa_n = sa_n + new * a_next
        if norm == "layer":
            mu = jnp.mean(y, axis=0, keepdims=True)
            d = y - mu
            y = d * lax.rsqrt(jnp.mean(d * d, axis=0, keepdims=True) + RW_LNX_EPS)
        else:
            y = y * lax.rsqrt(jnp.mean(y * y, axis=0, keepdims=True) + NORM_EPS)
        y_ref[t] = y
        return sa_n

    sa_ref[...] = lax.fori_loop(0, tt, step, sa_ref[...])


def scan_call(r, w, k, v, a, b, s0, norm, tt=32):
    G, T, dk, L = r.shape
    dv = v.shape[2]
    tt = min(tt, T)
    nt = T // tt
    tspec = lambda d: pl.BlockSpec((None, tt, d, L), lambda g, i: (g, i, 0, 0))
    nxt = pl.BlockSpec((None, 1, dk, L), lambda g, i: (g, jnp.minimum((i + 1) * tt, T - 1), 0, 0))
    sspec = pl.BlockSpec((None, dk, dv, L), lambda g, i: (g, 0, 0, 0))
    return pl.pallas_call(
        functools.partial(_scan_body, tt=tt, dk=dk, norm=norm),
        grid=(G, nt),
        in_specs=[tspec(dk), tspec(dk), tspec(dk), tspec(dv), tspec(dk), tspec(dk), nxt, sspec],
        out_specs=[tspec(dv), sspec],
        out_shape=[jax.ShapeDtypeStruct((G, T, dv, L), F32),
                   jax.ShapeDtypeStruct((G, dk, dv, L), F32)],
        scratch_shapes=[pltpu.VMEM((dv, L), F32)],
        compiler_params=_cparams(("arbitrary", "arbitrary")),
        name="scan",
    )(r, w, k, v, a, b, a, s0)


def _to_scan(x, lanes_bh):
    B, T, H, C = x.shape
    if lanes_bh:
        return x.transpose(1, 3, 0, 2).reshape(1, T, C, B * H)
    return x.transpose(2, 1, 3, 0)


def _from_scan(y, B, H, lanes_bh):
    G, T, C, L = y.shape
    if lanes_bh:
        return y.reshape(T, C, B, H).transpose(2, 0, 3, 1)
    return y.transpose(3, 1, 0, 2)


def _state_to_scan(s, lanes_bh):
    B, H, dk, dv = s.shape
    if lanes_bh:
        return s.transpose(2, 3, 0, 1).reshape(1, dk, dv, B * H)
    return s.transpose(1, 2, 3, 0)


def _state_from_scan(s, B, H, lanes_bh):
    G, dk, dv, L = s.shape
    if lanes_bh:
        return s.reshape(dk, dv, B, H).transpose(2, 3, 0, 1)
    return s.transpose(3, 0, 1, 2)


NEG = -1e30


def _online_softmax_update(s, v, m_ref, l_ref, acc_ref):
    m_prev = m_ref[...]
    m_new = jnp.maximum(m_prev, jnp.max(s, axis=-1, keepdims=True))
    alpha = jnp.exp(m_prev - m_new)
    p = jnp.exp(s - m_new)
    l_ref[...] = alpha * l_ref[...] + jnp.sum(p, axis=-1, keepdims=True)
    acc_ref[...] = alpha * acc_ref[...] + jnp.dot(p.astype(BF16), v, preferred_element_type=F32)
    m_ref[...] = m_new


_NT = (((1,), (1,)), ((), ()))


def _mla_prompt_body(qi_ref, kj_ref, q_ref, k_ref, o_ref, m_ref, l_ref, acc_ref, *, tq, tk, scale):
    p = pl.program_id(1)
    qi, kj = qi_ref[p], kj_ref[p]

    @pl.when(kj == 0)
    def _():
        m_ref[...] = jnp.full(m_ref.shape, NEG, F32)
        l_ref[...] = jnp.zeros(l_ref.shape, F32)
        acc_ref[...] = jnp.zeros(acc_ref.shape, F32)

    k = k_ref[...]
    s = lax.dot_general(q_ref[...], k, _NT, preferred_element_type=F32) * scale
    tok = qi * tq + lax.broadcasted_iota(jnp.int32, s.shape, 0) // MLA_HEADS
    kpos = kj * tk + lax.broadcasted_iota(jnp.int32, s.shape, 1)
    s = jnp.where(kpos <= tok, s, NEG)
    _online_softmax_update(s, k[:, :MLA_KV_LORA], m_ref, l_ref, acc_ref)

    @pl.when(kj == (qi * tq + tq - 1) // tk)
    def _():
        o_ref[...] = (acc_ref[...] / l_ref[...]).astype(o_ref.dtype)


def mla_prompt_call(qcat, kcat, tq=256, tk=256):
    B, TH, W = qcat.shape
    T = kcat.shape[1]
    pairs = [(i, j) for i in range(T // tq) for j in range((i * tq + tq - 1) // tk + 1)]
    qi_tab = jnp.asarray([p[0] for p in pairs], jnp.int32)
    kj_tab = jnp.asarray([p[1] for p in pairs], jnp.int32)
    R = tq * MLA_HEADS
    return pl.pallas_call(
        functools.partial(_mla_prompt_body, tq=tq, tk=tk, scale=(MLA_NOPE + MLA_ROPE) ** -0.5),
        grid_spec=pltpu.PrefetchScalarGridSpec(
            num_scalar_prefetch=2, grid=(B, len(pairs)),
            in_specs=[pl.BlockSpec((None, R, W), lambda b, p, qi, kj: (b, qi[p], 0)),
                      pl.BlockSpec((None, tk, W), lambda b, p, qi, kj: (b, kj[p], 0))],
            out_specs=pl.BlockSpec((None, R, MLA_KV_LORA), lambda b, p, qi, kj: (b, qi[p], 0)),
            scratch_shapes=[pltpu.VMEM((R, 1), F32), pltpu.VMEM((R, 1), F32),
                            pltpu.VMEM((R, MLA_KV_LORA), F32)]),
        out_shape=jax.ShapeDtypeStruct((B, TH, MLA_KV_LORA), BF16),
        compiler_params=_cparams(("arbitrary", "arbitrary")),
        name="mla_prompt",
    )(qi_tab, kj_tab, qcat, kcat)


def _page_stream(pt_ref, streams, npc, n_pages):
    b, t = pl.program_id(0), pl.program_id(1)
    steps = pl.num_programs(1)
    n = b * steps + t
    slot = n % 2

    def each(bb, tt, sl, act):
        page0 = (tt * npc) % n_pages
        for pred, pool, buf, sem in streams:
            def go(pool=pool, buf=buf, sem=sem):
                for j in range(npc):
                    act(pltpu.make_async_copy(pool.at[pt_ref[bb, page0 + j]], buf.at[sl, j], sem.at[sl]))
            live = pred(tt)
            if live is True:
                go()
            else:
                pl.when(live)(go)

    @pl.when(n == 0)
    def _():
        each(b, t, slot, lambda cp: cp.start())

    @pl.when(n + 1 < pl.num_programs(0) * steps)
    def _():
        wrap = t + 1 == steps
        each(jnp.where(wrap, b + 1, b), jnp.where(wrap, 0, t + 1), 1 - slot, lambda cp: cp.start())

    each(b, t, slot, lambda cp: cp.wait())
    return slot


def _mla_sample_body(pt_ref, q_ref, kn_ref, poolc_ref, poolr_ref, o_ref,
                     cbuf, rbuf, semc, semr, m_ref, l_ref, acc_ref, *, npc, n_pages, scale):
    c = pl.program_id(1)
    nch = n_pages // npc
    always = lambda t: True
    slot = _page_stream(pt_ref, [(always, poolc_ref, cbuf, semc), (always, poolr_ref, rbuf, semr)],
                        npc, n_pages)

    @pl.when(c == 0)
    def _():
        m_ref[...] = jnp.full(m_ref.shape, NEG, F32)
        l_ref[...] = jnp.zeros(l_ref.shape, F32)
        acc_ref[...] = jnp.zeros(acc_ref.shape, F32)

    q = q_ref[...]
    q_lat, q_rope = q[:, :MLA_KV_LORA], q[:, MLA_KV_LORA:]
    ckv = cbuf[slot].reshape(npc * PAGE_SIZE, MLA_KV_LORA).astype(BF16)
    rt = jnp.concatenate([rbuf[slot, j] for j in range(npc)], axis=1)
    rt = jnp.concatenate([rt, jnp.zeros((LANES - MLA_ROPE, rt.shape[1]), F32)], axis=0).astype(BF16)
    s = lax.dot_general(q_lat, ckv, _NT, preferred_element_type=F32)
    s = (s + jnp.dot(q_rope, rt, preferred_element_type=F32)) * scale
    _online_softmax_update(s, ckv, m_ref, l_ref, acc_ref)

    @pl.when(c == nch - 1)
    def _():
        kn = kn_ref[...]
        s = lax.dot_general(q, kn, _NT, preferred_element_type=F32) * scale
        qs = lax.broadcasted_iota(jnp.int32, s.shape, 0) // MLA_HEADS
        col = lax.broadcasted_iota(jnp.int32, s.shape, 1)
        s = jnp.where(col <= qs, s, NEG)
        _online_softmax_update(s, kn[:, :MLA_KV_LORA], m_ref, l_ref, acc_ref)
        o_ref[...] = (acc_ref[...] / l_ref[...]).astype(o_ref.dtype)


def mla_sample_call(page_table, qcat, kcat_new, pool_c, pool_rt, npc=16):
    B, R, W = qcat.shape
    n_pages = page_table.shape[1]
    nch = n_pages // npc
    assert n_pages % npc == 0
    return pl.pallas_call(
        functools.partial(_mla_sample_body, npc=npc, n_pages=n_pages, scale=(MLA_NOPE + MLA_ROPE) ** -0.5),
        grid_spec=pltpu.PrefetchScalarGridSpec(
            num_scalar_prefetch=1, grid=(B, nch),
            in_specs=[pl.BlockSpec((None, R, W), lambda b, c, pt: (b, 0, 0)),
                      pl.BlockSpec((None,) + kcat_new.shape[1:], lambda b, c, pt: (b, 0, 0)),
                      pl.BlockSpec(memory_space=pl.ANY), pl.BlockSpec(memory_space=pl.ANY)],
            out_specs=pl.BlockSpec((None, R, MLA_KV_LORA), lambda b, c, pt: (b, 0, 0)),
            scratch_shapes=[pltpu.VMEM((2, npc, PAGE_SIZE, MLA_KV_LORA), F32),
                            pltpu.VMEM((2, npc, MLA_ROPE, PAGE_SIZE), F32),
                            pltpu.SemaphoreType.DMA((2,)), pltpu.SemaphoreType.DMA((2,)),
                            pltpu.VMEM((R, 1), F32), pltpu.VMEM((R, 1), F32),
                            pltpu.VMEM((R, MLA_KV_LORA), F32)]),
        out_shape=jax.ShapeDtypeStruct((B, R, MLA_KV_LORA), BF16),
        compiler_params=_cparams(("arbitrary", "arbitrary")),
        name="mla_sample",
    )(page_table, qcat, kcat_new, pool_c, pool_rt)


def _kmean_body(k_ref, o_ref):
    k = k_ref[...]
    nb = k.shape[0] // MB_BLOCK
    o_ref[...] = jnp.mean(k.reshape(nb, MB_BLOCK, k.shape[1]), axis=1)


def kmean_call(k):
    B, T, W = k.shape
    nb = T // MB_BLOCK
    return pl.pallas_call(
        _kmean_body, grid=(B,),
        in_specs=[pl.BlockSpec((None, T, W), lambda b: (b, 0, 0))],
        out_specs=pl.BlockSpec((None, nb, W), lambda b: (b, 0, 0)),
        out_shape=jax.ShapeDtypeStruct((B, nb, W), F32),
        compiler_params=_cparams(("arbitrary",)),
        name="kmean",
    )(k)


def _moba_prompt_body(qi_ref, kb_ref, last_ref, q_ref, k_ref, v_ref, km_ref, o_ref,
                      sel_ref, m_ref, l_ref, acc_ref, *, nb):
    p = pl.program_id(1)
    qi, kb = qi_ref[p], kb_ref[p]
    own = kb == qi
    q = q_ref[...]
    lane = lax.broadcasted_iota(jnp.int32, (MB_BLOCK, LANES), 1)

    @pl.when(own)
    def _():
        gate = jnp.dot(q, km_ref[...], preferred_element_type=F32, precision=lax.Precision.HIGHEST)
        j = lane % nb
        valid = (j < qi) & (lane < MB_HEADS * nb)
        g = jnp.where(valid, gate, -jnp.inf)
        rank = jnp.zeros(g.shape, jnp.int32)
        for d in range(1, nb):
            up = pltpu.roll(g, LANES - d, 1)
            rank += jnp.where((j + d < nb) & (up > g), 1, 0)
            dn = pltpu.roll(g, d, 1)
            rank += jnp.where((j - d >= 0) & (dn >= g), 1, 0)
        sel_ref[...] = jnp.where(valid & (rank < MB_TOPK), 1.0, 0.0)
        m_ref[...] = jnp.full(m_ref.shape, NEG, F32)
        l_ref[...] = jnp.zeros(l_ref.shape, F32)
        acc_ref[...] = jnp.zeros(acc_ref.shape, F32)

    kb16 = k_ref[...].astype(BF16)
    vb16 = v_ref[...].astype(BF16)
    dist = ((qi - kb) * MB_BLOCK + lax.broadcasted_iota(jnp.int32, (MB_BLOCK, MB_BLOCK), 0)
            - lax.broadcasted_iota(jnp.int32, (MB_BLOCK, MB_BLOCK), 1))
    distf = dist.astype(F32)
    sel = sel_ref[...]
    for h in range(MB_HEADS):
        g = h // MB_GROUP
        hs = slice(h * HEAD_DIM, (h + 1) * HEAD_DIM)
        gs = slice(g * HEAD_DIM, (g + 1) * HEAD_DIM)
        s = lax.dot_general(q[:, hs].astype(BF16), kb16[:, gs], _NT, preferred_element_type=F32)
        s = s * (HEAD_DIM ** -0.5) - (2.0 ** (-8.0 * (h + 1) / MB_HEADS)) * distf
        picked = jnp.sum(jnp.where(lane == h * nb + kb, sel, 0.0), axis=-1, keepdims=True) > 0.5
        s = jnp.where(own, jnp.where(dist >= 0, s, NEG), jnp.where(picked, s, NEG))
        _online_softmax_update(s, vb16[:, gs], m_ref.at[h], l_ref.at[h], acc_ref.at[h])

    @pl.when(last_ref[p] == 1)
    def _():
        o_ref[...] = jnp.concatenate([acc_ref[h] / l_ref[h] for h in range(MB_HEADS)], axis=-1)


def moba_prompt_call(q, k, v, km):
    B, T, Wq = q.shape
    Wk = k.shape[2]
    nb = T // MB_BLOCK
    steps = [(i, kb) for i in range(nb) for kb in [i] + list(range(i))]
    qi_tab = jnp.asarray([s[0] for s in steps], jnp.int32)
    kb_tab = jnp.asarray([s[1] for s in steps], jnp.int32)
    last_tab = jnp.asarray([1 if (i == 0 or kb == i - 1) else 0 for i, kb in steps], jnp.int32)
    return pl.pallas_call(
        functools.partial(_moba_prompt_body, nb=nb),
        grid_spec=pltpu.PrefetchScalarGridSpec(
            num_scalar_prefetch=3, grid=(B, len(steps)),
            in_specs=[pl.BlockSpec((None, MB_BLOCK, Wq), lambda b, p, qi, kb, la: (b, qi[p], 0)),
                      pl.BlockSpec((None, MB_BLOCK, Wk), lambda b, p, qi, kb, la: (b, kb[p], 0)),
                      pl.BlockSpec((None, MB_BLOCK, Wk), lambda b, p, qi, kb, la: (b, kb[p], 0)),
                      pl.BlockSpec((None, Wq, LANES), lambda b, p, qi, kb, la: (b, 0, 0))],
            out_specs=pl.BlockSpec((None, MB_BLOCK, Wq), lambda b, p, qi, kb, la: (b, qi[p], 0)),
            scratch_shapes=[pltpu.VMEM((MB_BLOCK, LANES), F32),
                            pltpu.VMEM((MB_HEADS, MB_BLOCK, 1), F32),
                            pltpu.VMEM((MB_HEADS, MB_BLOCK, 1), F32),
                            pltpu.VMEM((MB_HEADS, MB_BLOCK, HEAD_DIM), F32)]),
        out_shape=jax.ShapeDtypeStruct((B, T, Wq), F32),
        compiler_params=_cparams(("arbitrary", "arbitrary")),
        name="moba_prompt",
    )(qi_tab, kb_tab, last_tab, q, k, v, km)


def _gate_matrix(kmean):
    B, nb, _ = kmean.shape
    kvh = np.arange(MB_HEADS) // MB_GROUP
    km = kmean.reshape(B, nb, MB_KV_HEADS, HEAD_DIM)[:, :, kvh, :].transpose(0, 2, 3, 1)
    eye = jnp.eye(MB_HEADS, dtype=F32)
    km = (km[:, :, :, None, :] * eye[None, :, None, :, None]).reshape(B, MB_HEADS * HEAD_DIM, MB_HEADS * nb)
    return jnp.pad(km, ((0, 0), (0, 0), (0, LANES - MB_HEADS * nb)))


def _moba_sample_body(pt_ref, qbd_ref, kn_ref, vn_ref, rc_ref, poolk_ref, poolv_ref, o_ref,
                      buf, sem, s_ref, p_ref, gate_ref, l_ref, acc_ref, *, npc, n_pages):
    t = pl.program_id(1)
    nch = n_pages // npc
    past = n_pages * PAGE_SIZE
    ph, c = t // nch, t % nch
    nb = past // MB_BLOCK
    ck = npc * PAGE_SIZE
    scale = HEAD_DIM ** -0.5
    R = qbd_ref.shape[0]
    slope = rc_ref[:, 0:1]
    qs = rc_ref[:, 1:2]
    slot = _page_stream(pt_ref, [(lambda tt: tt < nch, poolk_ref, buf, sem),
                                 (lambda tt: tt >= nch, poolv_ref, buf, sem)], npc, n_pages)

    def chunk_t():
        return jnp.concatenate([buf[slot, j] for j in range(npc)], axis=1).astype(BF16)

    glane = lax.broadcasted_iota(jnp.int32, gate_ref.shape, 1)

    @pl.when(ph == 0)
    def _():
        s = jnp.dot(qbd_ref[...].astype(BF16), chunk_t(), preferred_element_type=F32)
        s_ref[:, pl.ds(pl.multiple_of(c * ck, ck), ck)] = s
        g = jnp.where(c == 0, jnp.full(gate_ref.shape, -jnp.inf, F32), gate_ref[...])
        for i in range(ck // MB_BLOCK):
            bsum = jnp.sum(s[:, i * MB_BLOCK:(i + 1) * MB_BLOCK], axis=-1, keepdims=True)
            g = jnp.where(glane == c * (ck // MB_BLOCK) + i, bsum, g)
        gate_ref[...] = g

    @pl.when(t == nch - 1)
    def _():
        g = gate_ref[...]
        picks = []
        for _ in range(MB_TOPK):
            mx = jnp.max(g, axis=-1, keepdims=True)
            idx = jnp.min(jnp.where(g == mx, glane, LANES), axis=-1, keepdims=True)
            picks.append(idx)
            g = jnp.where(glane == idx, -jnp.inf, g)

        qb = qbd_ref[...].astype(BF16)
        kn = kn_ref[...].astype(BF16)
        s_own = lax.dot_general(qb, kn, _NT, preferred_element_type=F32) * scale
        col = lax.broadcasted_iota(jnp.int32, s_own.shape, 1).astype(F32)
        s_own = jnp.where(col <= qs, s_own - slope * (qs - col), NEG)
        m0 = jnp.max(s_own, axis=-1, keepdims=True)

        kcol = lax.broadcasted_iota(jnp.int32, (R, MB_BLOCK), 1).astype(F32)

        def block_scores(j):
            s = s_ref[:, pl.ds(pl.multiple_of(j * MB_BLOCK, MB_BLOCK), MB_BLOCK)]
            kpos = jnp.asarray(j * MB_BLOCK).astype(F32) + kcol
            s = s * scale - slope * ((past + qs) - kpos)
            picked = (picks[0] == j) | (picks[1] == j) | (picks[2] == j)
            return jnp.where(picked, s, NEG)

        m_acc = lax.fori_loop(0, nb, lambda j, a: jnp.maximum(a, block_scores(j)),
                              jnp.full((R, MB_BLOCK), NEG, F32))
        m = jnp.maximum(m0, jnp.max(m_acc, axis=-1, keepdims=True))

        def fill(j, a):
            p = jnp.exp(block_scores(j) - m)
            p_ref[:, pl.ds(pl.multiple_of(j * MB_BLOCK, MB_BLOCK), MB_BLOCK)] = p.astype(BF16)
            return a + p

        l_acc = lax.fori_loop(0, nb, fill, jnp.zeros((R, MB_BLOCK), F32))
        p_own = jnp.exp(s_own - m)
        l_ref[...] = jnp.sum(p_own, axis=-1, keepdims=True) + jnp.sum(l_acc, axis=-1, keepdims=True)
        acc_ref[...] = jnp.dot(p_own.astype(BF16), vn_ref[...].astype(BF16), preferred_element_type=F32)

    @pl.when(ph == 1)
    def _():
        pb = p_ref[:, pl.ds(pl.multiple_of(c * ck, ck), ck)]
        acc_ref[...] += lax.dot_general(pb, chunk_t(), _NT, preferred_element_type=F32)

    @pl.when(t == 2 * nch - 1)
    def _():
        out = acc_ref[...] / l_ref[...]
        S = R // MB_HEADS
        o_ref[...] = jnp.concatenate(
            [out[h * S:(h + 1) * S, (h // MB_GROUP) * HEAD_DIM:(h // MB_GROUP + 1) * HEAD_DIM]
             for h in range(MB_HEADS)], axis=-1)


def moba_sample_call(page_table, q, k_new, v_new, pool_kt, pool_vt, npc=16):
    B, S, Wq = q.shape
    Wk = k_new.shape[2]
    n_pages = page_table.shape[1]
    past = n_pages * PAGE_SIZE
    nch = n_pages // npc
    assert past % MB_BLOCK == 0 and past // MB_BLOCK >= MB_TOPK and n_pages % npc == 0
    R = MB_HEADS * S
    kvh = np.arange(MB_HEADS) // MB_GROUP
    onehot = jnp.asarray(np.eye(MB_KV_HEADS, dtype=np.float32)[kvh])
    qh = q.reshape(B, S, MB_HEADS, HEAD_DIM).transpose(0, 2, 1, 3)
    qbd = (qh[:, :, :, None, :] * onehot[None, :, None, :, None]).reshape(B, R, Wk)
    pad = lambda t: jnp.pad(t, ((0, 0), (0, LANES - S), (0, 0)))
    rc = np.zeros((R, LANES), np.float32)
    rc[:, 0] = np.repeat([2.0 ** (-8.0 * (h + 1) / MB_HEADS) for h in range(MB_HEADS)], S)
    rc[:, 1] = np.tile(np.arange(S), MB_HEADS)
    full = lambda shp: pl.BlockSpec((None,) + shp, lambda b, t, pt: (b, 0, 0))
    return pl.pallas_call(
        functools.partial(_moba_sample_body, npc=npc, n_pages=n_pages),
        grid_spec=pltpu.PrefetchScalarGridSpec(
            num_scalar_prefetch=1, grid=(B, 2 * nch),
            in_specs=[full((R, Wk)), full((LANES, Wk)), full((LANES, Wk)),
                      pl.BlockSpec((R, LANES), lambda b, t, pt: (0, 0)),
                      pl.BlockSpec(memory_space=pl.ANY), pl.BlockSpec(memory_space=pl.ANY)],
            out_specs=full((S, Wq)),
            scratch_shapes=[pltpu.VMEM((2, npc, Wk, PAGE_SIZE), F32), pltpu.SemaphoreType.DMA((2,)),
                            pltpu.VMEM((R, past), F32), pltpu.VMEM((R, past), BF16),
                            pltpu.VMEM((R, LANES), F32),
                            pltpu.VMEM((R, 1), F32), pltpu.VMEM((R, Wk), F32)]),
        out_shape=jax.ShapeDtypeStruct((B, S, Wq), F32),
        compiler_params=_cparams(("arbitrary", "arbitrary")),
        name="moba_sample",
    )(page_table, qbd, pad(k_new), pad(v_new), jnp.asarray(rc), pool_kt, pool_vt)


def _l2norm(x):
    return x * lax.rsqrt(jnp.sum(x * x, axis=-1, keepdims=True) + 1e-12)


def _rope_tables(pos):
    half = MLA_ROPE // 2
    inv = ROPE_THETA ** (-jnp.arange(half, dtype=F32) / half)
    ang = pos.astype(F32)[:, None] * inv[None, :]
    return jnp.cos(ang), jnp.sin(ang)


def _rwkv_mix(p, prev, S0, prm, lanes_bh):
    mu, w0, w2, a0, a2, g2, k_k, k_a, r_k, lnx_g, lnx_b = prm
    B, T, _ = p.shape
    H = RW_HEADS
    p_prev = jnp.concatenate([prev[:, None, :], p[:, :-1]], axis=1)
    pm = p + (p_prev - p) * mu
    cuts = [int(c) for c in np.cumsum(RW_SPLITS)[:-1]]
    r, wl, k, v, al, gl = jnp.split(pm, cuts, axis=-1)
    w_log = -jax.nn.softplus(-(w0 + jnp.tanh(wl) @ w2)) - 0.5
    decay = jnp.exp(-jnp.exp(w_log))
    a = jax.nn.sigmoid(a0 + al @ a2)
    g = jax.nn.sigmoid(gl) @ g2
    heads = lambda t: t.reshape(B, T, H, HEAD_DIM)
    kk = _l2norm(heads(k * k_k))
    k = k * (1.0 + (a - 1.0) * k_a)
    r_h, k_h, v_h, w_h, a_h = heads(r), heads(k), heads(v), heads(decay), heads(a)
    aa, bb = -kk, kk * a_h
    ts = lambda t: _to_scan(t, lanes_bh)
    s0 = _state_to_scan(jnp.swapaxes(S0, -1, -2), lanes_bh)
    y, S = scan_call(ts(r_h), ts(w_h), ts(k_h), ts(v_h), ts(aa), ts(bb), s0, "layer")
    y = _from_scan(y, B, H, lanes_bh).reshape(B, T, RW_WIDTH) * lnx_g + lnx_b
    S = jnp.swapaxes(_state_from_scan(S, B, H, lanes_bh), -1, -2)
    bonus = jnp.sum(r_h * k_h * r_k, axis=-1, keepdims=True) * v_h
    y = (y + bonus.reshape(B, T, RW_WIDTH)) * g
    return y, S, p[:, -1]


def _gdn_mix(qkv, z, a_raw, b_raw, conv_buf, S0, prm, lanes_bh):
    conv_w, a_log, dt_bias, norm_g = prm
    B, T, _ = qkv.shape
    H = GD_HEADS
    xx = jnp.concatenate([conv_buf, qkv], axis=1)
    y = sum(xx[:, j:j + T] * conv_w[j] for j in range(GD_CONV))
    y = _silu(y)
    q, k, v = jnp.split(y, [H * GD_DK, 2 * H * GD_DK], axis=-1)
    q = _l2norm(q.reshape(B, T, H, GD_DK)) * (GD_DK ** -0.5)
    k = _l2norm(k.reshape(B, T, H, GD_DK))
    v = v.reshape(B, T, H, GD_DV)
    beta = jax.nn.sigmoid(b_raw)[..., None]
    alpha = jnp.exp(-jnp.exp(a_log) * jax.nn.softplus(a_raw + dt_bias))[..., None]
    ts = lambda t: _to_scan(t, lanes_bh)
    w = jnp.broadcast_to(alpha, k.shape)
    y, S = scan_call(ts(q), ts(w), ts(k), ts(beta * v), ts(k), ts(-beta * alpha * k),
                     _state_to_scan(S0, lanes_bh), "rms")
    o = _from_scan(y, B, H, lanes_bh) * norm_g * _silu(z.reshape(B, T, H, GD_DV))
    S = _state_from_scan(S, B, H, lanes_bh)
    return o.reshape(B, T, H * GD_DV), S, xx[:, -(GD_CONV - 1):]


def _bdiag(w):
    H, a, b = w.shape
    eye = jnp.eye(H, dtype=w.dtype)
    return (eye[:, None, :, None] * w[:, :, None, :]).reshape(H * a, H * b)


def kernel(x_prompt, x_sample, cache_moba_k, cache_moba_v, cache_mla_ckv, cache_mla_krope, state_rwkv_wkv, state_rwkv_shift, state_gdn, state_gdn_conv, page_table, c_prompt, c_sample, w_ada, b_ada, g_mix_pre, g_mix_post, g_ff_pre, g_ff_post, w_in_even, w_out_even, rw_mu, rw_w0, rw_w2, rw_a0, rw_a2, rw_g2, rw_kk, rw_ka, rw_rk, rw_lnx_g, rw_lnx_b, ffn_w_gate, ffn_w_up, ffn_w_down, w_in_odd, w_out_odd, mla_q_norm, mla_w_q_up, mla_kv_norm, mla_w_uk, mla_w_uv, gdn_conv_w, gdn_a_log, gdn_dt_bias, gdn_norm_g, moe_w_router, moe_b_router, moe_w_gate, moe_w_up, moe_w_down):
    Bp, Tp, D = x_prompt.shape
    Bs, Ts, _ = x_sample.shape
    past = page_table.shape[1] * PAGE_SIZE
    Ms = Bs * Ts
    row = lambda v: v.reshape(1, -1)

    xp = x_prompt
    xs = x_sample.reshape(1, Ms, D)
    c_all = jnp.concatenate([c_prompt, c_sample], axis=0)

    outs_p, outs_s = {}, {}
    for layer in range(w_ada.shape[0]):
        i = layer // 2
        mod = adaln_call(c_all, w_ada[layer].astype(BF16), row(b_ada[layer]))
        mp = [m.reshape(Bp, 1, D) for m in jnp.split(mod[:Bp], 6, axis=-1)]
        ms = [jnp.broadcast_to(m[:, None, :], (Bs, Ts, D)).reshape(1, Ms, D)
              for m in jnp.split(mod[Bp:], 6, axis=-1)]
        gpre, gpost = row(g_mix_pre[layer]), row(g_mix_post[layer])
        if layer % 2 == 0:
            w_in = w_in_even[i].astype(BF16)
            splits = (RW_PROJ, MB_HEADS * HEAD_DIM, MB_KV_HEADS * HEAD_DIM, MB_KV_HEADS * HEAD_DIM)
            rw_prm = (rw_mu[i], rw_w0[i], rw_w2[i], rw_a0[i], rw_a2[i], rw_g2[i], rw_kk[i], rw_ka[i],
                      rw_rk[i], rw_lnx_g[i], rw_lnx_b[i])
            w_out = w_out_even[i].astype(BF16)
            p_rw, q, k, v = modmm_call(xp, mp[0], mp[1], gpre, w_in, splits)
            y_rw, S_p, shift_p = _rwkv_mix(p_rw, jnp.zeros((Bp, RW_PROJ), F32),
                                           jnp.zeros((Bp, RW_HEADS, HEAD_DIM, HEAD_DIM), F32), rw_prm, True)
            k4 = k.reshape(Bp, Tp, MB_KV_HEADS, HEAD_DIM)
            v4 = v.reshape(Bp, Tp, MB_KV_HEADS, HEAD_DIM)
            y_mb = moba_prompt_call(q, k, v, _gate_matrix(kmean_call(k)))
            xp = outproj_call(y_rw, y_mb, w_out[:RW_WIDTH], w_out[RW_WIDTH:], xp, mp[2], gpost)
            outs_p.update(mk=k4[None], mv=v4[None], wkv=S_p[None], shift=shift_p[None])
            p_rw, q, k, v = modmm_call(xs, ms[0], ms[1], gpre, w_in, splits)
            y_rw, S_s, shift_s = _rwkv_mix(p_rw.reshape(Bs, Ts, RW_PROJ), state_rwkv_shift[i],
                                           state_rwkv_wkv[i], rw_prm, False)
            k4 = k.reshape(Bs, Ts, MB_KV_HEADS, HEAD_DIM)
            v4 = v.reshape(Bs, Ts, MB_KV_HEADS, HEAD_DIM)
            n_pool = cache_moba_k.shape[1]
            pool_t = lambda c: c[i].transpose(0, 2, 3, 1).reshape(n_pool, -1, PAGE_SIZE)
            y_mb = moba_sample_call(page_table, q.reshape(Bs, Ts, -1), k.reshape(Bs, Ts, -1), v.reshape(Bs, Ts, -1),
                                    pool_t(cache_moba_k), pool_t(cache_moba_v))
            xs = outproj_call(y_rw.reshape(1, Ms, RW_WIDTH), y_mb.reshape(1, Ms, -1),
                              w_out[:RW_WIDTH], w_out[RW_WIDTH:], xs, ms[2], gpost)
            outs_s.update(mk=k4[None], mv=v4[None], wkv=S_s[None], shift=shift_s[None])
        else:
            w_in = w_in_odd[i]
            cuts = [int(c) for c in np.cumsum(ODD_SPLITS)[:-1]]
            w_cq, w_ckv, w_kr, w_qkv, w_z, w_a, w_b = jnp.split(w_in, cuts, axis=1)
            padc = lambda w, n: jnp.pad(w, ((0, 0), (0, n - w.shape[1])))
            w_in_p = jnp.concatenate([w_cq, w_ckv, padc(w_kr, LANES), w_qkv, w_z,
                                      padc(jnp.concatenate([w_a, w_b], axis=1), LANES)], axis=1).astype(BF16)
            splits = (MLA_Q_LORA, MLA_KV_LORA, LANES, GD_QKV, GD_HEADS * GD_DV, LANES)
            w_out = w_out_odd[i].astype(BF16)
            gd_prm = (gdn_conv_w[i], gdn_a_log[i], gdn_dt_bias[i], gdn_norm_g[i])
            w_uv_bd = _bdiag(mla_w_uv[i].transpose(1, 0, 2)).astype(BF16)
            w_uk_bd = _bdiag(mla_w_uk[i].transpose(1, 2, 0)).astype(BF16)
            wq = mla_w_q_up[i].reshape(MLA_Q_LORA, MLA_HEADS, MLA_NOPE + MLA_ROPE)
            w_qn = wq[:, :, :MLA_NOPE].reshape(MLA_Q_LORA, -1).astype(BF16)
            w_qr = jnp.pad(wq[:, :, MLA_NOPE:], ((0, 0), (0, 0), (0, LANES - MLA_ROPE)))
            w_qr = w_qr.reshape(MLA_Q_LORA, -1).astype(BF16)
            w_oa, w_ob = w_out[:MLA_HEADS * MLA_V], w_out[MLA_HEADS * MLA_V:]

            def odd(x, m, B, T, cosf, sinf, conv_buf, S0, lanes_bh):
                cq, ckv, kr, qkv, z, ab = modmm_call(x, m[0], m[1], gpre, w_in_p, splits)
                qcat, kcat, ckv_n, k_rope = mla_prep_call(
                    cq, ckv, kr, cosf, sinf, row(mla_q_norm[i]), row(mla_kv_norm[i]), w_qn, w_qr, w_uk_bd)
                qkv, z = qkv.reshape(B, T, -1), z.reshape(B, T, -1)
                ab = ab.reshape(B, T, LANES)
                a_raw, b_raw = ab[..., :GD_HEADS], ab[..., GD_HEADS:2 * GD_HEADS]
                y_gdn, S, buf = _gdn_mix(qkv, z, a_raw, b_raw, conv_buf, S0, gd_prm, lanes_bh)
                ckv_n = ckv_n.reshape(B, T, MLA_KV_LORA)
                k_rope = k_rope.reshape(B, T, LANES)[..., :MLA_ROPE]
                return qcat, kcat, y_gdn, ckv_n, k_rope, S, buf

            cosf, sinf = _rope_rows(jnp.arange(Tp))
            qcat, kcat, y_gdn, ckv_n, k_rope, S_p, buf_p = odd(
                xp, mp, Bp, Tp, cosf, sinf, jnp.zeros((Bp, GD_CONV - 1, GD_QKV), F32),
                jnp.zeros((Bp, GD_HEADS, GD_DK, GD_DV), F32), True)
            o_lat = mla_prompt_call(qcat.reshape(Bp, Tp * MLA_HEADS, 2 * LANES), kcat)
            xp = outproj_call(o_lat.reshape(Bp, Tp, -1), y_gdn, w_oa, w_ob, xp, mp[2], gpost, wpre=w_uv_bd)
            outs_p.update(ckv=ckv_n[None], krope=k_rope[None], gdn=S_p[None], conv=buf_p[None])
            cosf, sinf = _rope_rows(jnp.tile(past + jnp.arange(Ts), Bs))
            qcat, kcat, y_gdn, ckv_n, k_rope, S_s, buf_s = odd(
                xs, ms, Bs, Ts, cosf, sinf, state_gdn_conv[i], state_gdn[i], False)
            kn = jnp.pad(kcat.reshape(Bs, Ts, 2 * LANES), ((0, 0), (0, LANES - Ts), (0, 0)))
            o_lat = mla_sample_call(page_table, qcat.reshape(Bs, Ts * MLA_HEADS, 2 * LANES), kn,
                                    cache_mla_ckv[i], cache_mla_krope[i].transpose(0, 2, 1))
            xs = outproj_call(o_lat.reshape(1, Ms, -1), y_gdn.reshape(1, Ms, -1), w_oa, w_ob, xs, ms[2], gpost,
                              wpre=w_uv_bd)
            outs_s.update(ckv=ckv_n[None], krope=k_rope[None], gdn=S_s[None], conv=buf_s[None])

        gpre, gpost = row(g_ff_pre[layer]), row(g_ff_post[layer])
        if layer % 2 == 0:
            J = 2
            Fd = ffn_w_gate.shape[2] // J
            wg = ffn_w_gate[i].astype(BF16).reshape(D, J, Fd).transpose(1, 0, 2)
            wu = ffn_w_up[i].astype(BF16).reshape(D, J, Fd).transpose(1, 0, 2)
            wd = ffn_w_down[i].astype(BF16).reshape(J, Fd, D)
            xp = ffn_call(xp, mp[3], mp[4], mp[5], gpre, gpost, wg, wu, wd)
            xs = ffn_call(xs, ms[3], ms[4], ms[5], gpre, gpost, wg, wu, wd)
        else:
            wr = jnp.pad(moe_w_router[i], ((0, 0), (0, LANES - N_EXPERTS)))
            br = jnp.pad(row(moe_b_router[i]), ((0, 0), (0, LANES - N_EXPERTS)))
            wg, wu, wd = (w[i].astype(BF16) for w in (moe_w_gate, moe_w_up, moe_w_down))
            xp = ffn_call(xp, mp[3], mp[4], mp[5], gpre, gpost, wg, wu, wd, wr, br)
            xs = ffn_call(xs, ms[3], ms[4], ms[5], gpre, gpost, wg, wu, wd, wr, br)

    names = ('mk', 'mv', 'wkv', 'shift', 'ckv', 'krope', 'gdn', 'conv')
    return ((xp, xs.reshape(Bs, Ts, D)) + tuple(outs_p[n] for n in names)
            + tuple(outs_s[n] for n in names))
```

```python
import functools
import math

import jax
import jax.numpy as jnp
import numpy as np
from jax import lax
from jax.experimental import pallas as pl
from jax.experimental.pallas import tpu as pltpu

F32 = jnp.float32
BF16 = jnp.bfloat16

NORM_EPS = 1e-6
HEAD_DIM = 64
PAGE_SIZE = 128
RW_HEADS = 8
RW_WIDTH = RW_HEADS * HEAD_DIM
RW_SPLITS = (RW_WIDTH, 64, RW_WIDTH, RW_WIDTH, 64, 128)
RW_PROJ = sum(RW_SPLITS)
RW_LNX_EPS = 64e-5
MB_HEADS = 8
MB_KV_HEADS = 4
MB_GROUP = MB_HEADS // MB_KV_HEADS
MB_BLOCK = 256
MB_TOPK = 3
MLA_HEADS = 8
MLA_NOPE = 64
MLA_ROPE = 32
MLA_V = 64
MLA_Q_LORA = 256
MLA_KV_LORA = 128
ROPE_THETA = 10000.0
GD_HEADS = 8
GD_DK = 64
GD_DV = 64
GD_CONV = 4
GD_QKV = GD_HEADS * (2 * GD_DK + GD_DV)
ODD_SPLITS = (MLA_Q_LORA, MLA_KV_LORA, MLA_ROPE, GD_QKV, GD_HEADS * GD_DV, GD_HEADS, GD_HEADS)
N_EXPERTS = 8

LANES = 128
VMEM_LIMIT = 56 * 1024 * 1024


def _cparams(sem):
    return pltpu.CompilerParams(dimension_semantics=sem, vmem_limit_bytes=VMEM_LIMIT)


def _silu(x):
    return x * jax.nn.sigmoid(x)


def _rms(x, eps=NORM_EPS):
    return x * lax.rsqrt(jnp.mean(x * x, axis=-1, keepdims=True) + eps)


def _adaln_body(c_ref, w_ref, b_ref, o_ref):
    c = c_ref[...]
    o_ref[...] = jnp.dot(_silu(c).astype(BF16), w_ref[...], preferred_element_type=F32) + b_ref[...]


def adaln_call(c, w, b, tn=1536):
    R, D = c.shape
    N = w.shape[1]
    return pl.pallas_call(
        _adaln_body,
        grid=(N // tn,),
        in_specs=[pl.BlockSpec((R, D), lambda j: (0, 0)),
                  pl.BlockSpec((D, tn), lambda j: (0, j)),
                  pl.BlockSpec((1, tn), lambda j: (0, j))],
        out_specs=pl.BlockSpec((R, tn), lambda j: (0, j)),
        out_shape=jax.ShapeDtypeStruct((R, N), F32),
        compiler_params=_cparams(("arbitrary",)),
        name="adaln",
    )(c, w, b)


def _mod_spec(mod, tm):
    if mod.shape[1] == 1:
        return pl.BlockSpec((None, 1, mod.shape[2]), lambda g, i, *_: (g, 0, 0))
    return pl.BlockSpec((None, tm, mod.shape[2]), lambda g, i, *_: (g, i, 0))


def _modmm_body(x_ref, sh_ref, sc_ref, g_ref, w_ref, *o_refs, splits):
    h = _rms(x_ref[...]) * g_ref[...] * (1.0 + sc_ref[...]) + sh_ref[...]
    y = jnp.dot(h.astype(BF16), w_ref[...], preferred_element_type=F32)
    off = 0
    for o_ref, n in zip(o_refs, splits):
        o_ref[...] = y[:, off:off + n].astype(o_ref.dtype)
        off += n


def modmm_call(x, shift, scale, gain, w, splits, tm=512):
    G, R, D = x.shape
    tm = min(tm, R)
    N = w.shape[1]
    assert sum(splits) == N and all(s % LANES == 0 for s in splits)
    return pl.pallas_call(
        functools.partial(_modmm_body, splits=splits),
        grid=(G, R // tm),
        in_specs=[pl.BlockSpec((None, tm, D), lambda g, i: (g, i, 0)),
                  _mod_spec(shift, tm), _mod_spec(scale, tm),
                  pl.BlockSpec((1, D), lambda g, i: (0, 0)),
                  pl.BlockSpec((D, N), lambda g, i: (0, 0))],
        out_specs=[pl.BlockSpec((None, tm, n), lambda g, i: (g, i, 0)) for n in splits],
        out_shape=[jax.ShapeDtypeStruct((G, R, n), F32) for n in splits],
        compiler_params=_cparams(("arbitrary", "arbitrary")),
        name="modmm",
    )(x, shift, scale, gain, w)


def _outproj_body(ya_ref, yb_ref, wpre_ref, wa_ref, wb_ref, x_ref, ga_ref, gp_ref, o_ref, *, pre):
    ya = ya_ref[...].astype(BF16)
    if pre:
        ya = jnp.dot(ya, wpre_ref[...], preferred_element_type=F32).astype(BF16)
    y = jnp.dot(ya, wa_ref[...], preferred_element_type=F32)
    y = y + jnp.dot(yb_ref[...].astype(BF16), wb_ref[...], preferred_element_type=F32)
    o_ref[...] = x_ref[...] + ga_ref[...] * (_rms(y) * gp_ref[...])


def outproj_call(ya, yb, wa, wb, x, gate, gpost, wpre=None, tm=512):
    G, R, D = x.shape
    tm = min(tm, R)
    Ka, Kb = ya.shape[2], yb.shape[2]
    pre = wpre is not None
    if not pre:
        wpre = jnp.zeros((8, LANES), BF16)
    return pl.pallas_call(
        functools.partial(_outproj_body, pre=pre),
        grid=(G, R // tm),
        in_specs=[pl.BlockSpec((None, tm, Ka), lambda g, i: (g, i, 0)),
                  pl.BlockSpec((None, tm, Kb), lambda g, i: (g, i, 0)),
                  pl.BlockSpec(wpre.shape, lambda g, i: (0, 0)),
                  pl.BlockSpec(wa.shape, lambda g, i: (0, 0)),
                  pl.BlockSpec(wb.shape, lambda g, i: (0, 0)),
                  pl.BlockSpec((None, tm, D), lambda g, i: (g, i, 0)),
                  _mod_spec(gate, tm),
                  pl.BlockSpec((1, D), lambda g, i: (0, 0))],
        out_specs=pl.BlockSpec((None, tm, D), lambda g, i: (g, i, 0)),
        out_shape=jax.ShapeDtypeStruct((G, R, D), F32),
        compiler_params=_cparams(("arbitrary", "arbitrary")),
        name="outproj",
    )(ya, yb, wpre, wa, wb, x, gate, gpost)


def _mla_prep_body(cq_ref, ckv_ref, kr_ref, cos_ref, sin_ref, qg_ref, kg_ref, wqn_ref, wqr_ref, wuk_ref,
                   qcat_ref, kcat_ref, ckvn_ref, krope_ref):
    cosf, sinf = cos_ref[...], sin_ref[...]
    lane = lax.broadcasted_iota(jnp.int32, cosf.shape, 1)
    half = MLA_ROPE // 2

    def rope(x):
        rot = jnp.where(lane < half, pltpu.roll(x, LANES - half, 1), pltpu.roll(x, half, 1))
        return x * cosf + rot * sinf

    cqn = (_rms(cq_ref[...]) * qg_ref[...]).astype(BF16)
    qn = jnp.dot(cqn, wqn_ref[...], preferred_element_type=F32).astype(BF16)
    qr = jnp.dot(cqn, wqr_ref[...], preferred_element_type=F32)
    ql = jnp.dot(qn, wuk_ref[...], preferred_element_type=F32)
    for h in range(MLA_HEADS):
        qcat_ref[:, 2 * h * LANES:(2 * h + 1) * LANES] = ql[:, h * LANES:(h + 1) * LANES].astype(BF16)
        qcat_ref[:, (2 * h + 1) * LANES:(2 * h + 2) * LANES] = rope(qr[:, h * LANES:(h + 1) * LANES]).astype(BF16)
    ckvn = _rms(ckv_ref[...]) * kg_ref[...]
    krope = rope(kr_ref[...])
    ckvn_ref[...] = ckvn
    krope_ref[...] = krope
    kcat_ref[:, :LANES] = ckvn.astype(BF16)
    kcat_ref[:, LANES:] = krope.astype(BF16)


def mla_prep_call(cq, ckv, kr, cosf, sinf, q_gain, kv_gain, w_qn, w_qr, w_uk_bd, tm=512):
    G, R, _ = cq.shape
    tm = min(tm, R)
    rows = lambda n: pl.BlockSpec((None, tm, n), lambda g, i: (g, i, 0))
    const = lambda a: pl.BlockSpec(a.shape, lambda g, i: (0, 0))
    W = 2 * LANES * MLA_HEADS
    return pl.pallas_call(
        _mla_prep_body,
        grid=(G, R // tm),
        in_specs=[rows(cq.shape[2]), rows(LANES), rows(LANES),
                  pl.BlockSpec((tm, LANES), lambda g, i: (i, 0)), pl.BlockSpec((tm, LANES), lambda g, i: (i, 0)),
                  const(q_gain), const(kv_gain), const(w_qn), const(w_qr), const(w_uk_bd)],
        out_specs=[rows(W), rows(2 * LANES), rows(LANES), rows(LANES)],
        out_shape=[jax.ShapeDtypeStruct((G, R, W), BF16), jax.ShapeDtypeStruct((G, R, 2 * LANES), BF16),
                   jax.ShapeDtypeStruct((G, R, LANES), F32), jax.ShapeDtypeStruct((G, R, LANES), F32)],
        compiler_params=_cparams(("arbitrary", "arbitrary")),
        name="mla_prep",
    )(cq, ckv, kr, cosf, sinf, q_gain, kv_gain, w_qn, w_qr, w_uk_bd)


def _rope_rows(pos):
    cos, sin = _rope_tables(pos)
    z = jnp.zeros((pos.shape[0], LANES - MLA_ROPE), F32)
    return jnp.concatenate([cos, cos, z], axis=1), jnp.concatenate([-sin, sin, z], axis=1)


def _ffn_body(x_ref, sh_ref, sc_ref, ga_ref, gpre_ref, gpost_ref, wr_ref, br_ref,
              wg_ref, wu_ref, wd_ref, o_ref, h_ref, acc_ref, rw_ref, *, moe):
    j = pl.program_id(2)

    @pl.when(j == 0)
    def _():
        h = _rms(x_ref[...]) * gpre_ref[...] * (1.0 + sc_ref[...]) + sh_ref[...]
        h_ref[...] = h.astype(BF16)
        acc_ref[...] = jnp.zeros_like(acc_ref)
        if moe:
            logits = jnp.dot(h, wr_ref[...], preferred_element_type=F32,
                             precision=lax.Precision.HIGHEST) + br_ref[...]
            lane = lax.broadcasted_iota(jnp.int32, logits.shape, 1)
            neg = jnp.float32(-jnp.inf)
            logits = jnp.where(lane < N_EXPERTS, logits, neg)
            m1 = jnp.max(logits, axis=-1, keepdims=True)
            i1 = jnp.min(jnp.where(logits == m1, lane, LANES), axis=-1, keepdims=True)
            rest = jnp.where(lane == i1, neg, logits)
            m2 = jnp.max(rest, axis=-1, keepdims=True)
            i2 = jnp.min(jnp.where(rest == m2, lane, LANES), axis=-1, keepdims=True)
            e2 = jnp.exp(m2 - m1)
            w1 = 1.0 / (1.0 + e2)
            w2 = e2 / (1.0 + e2)
            rw_ref[...] = jnp.where(lane == i1, w1, 0.0) + jnp.where(lane == i2, w2, 0.0)

    hb = h_ref[...]
    g = jnp.dot(hb, wg_ref[...], preferred_element_type=F32)
    u = jnp.dot(hb, wu_ref[...], preferred_element_type=F32)
    f = jnp.dot((_silu(g) * u).astype(BF16), wd_ref[...], preferred_element_type=F32)
    if moe:
        lane = lax.broadcasted_iota(jnp.int32, rw_ref.shape, 1)
        we = jnp.sum(jnp.where(lane == j, rw_ref[...], 0.0), axis=-1, keepdims=True)
        f = f * we
    acc_ref[...] += f

    @pl.when(j == pl.num_programs(2) - 1)
    def _():
        o_ref[...] = x_ref[...] + ga_ref[...] * (_rms(acc_ref[...]) * gpost_ref[...])


def ffn_call(x, shift, scale, gate, gpre, gpost, wg, wu, wd, w_router=None, b_router=None, tm=512):
    G, R, D = x.shape
    tm = min(tm, R)
    J, _, Fd = wg.shape
    moe = w_router is not None
    if not moe:
        w_router = jnp.zeros((D, LANES), F32)
        b_router = jnp.zeros((1, LANES), F32)
    return pl.pallas_call(
        functools.partial(_ffn_body, moe=moe),
        grid=(G, R // tm, J),
        in_specs=[pl.BlockSpec((None, tm, D), lambda g, i, j: (g, i, 0)),
                  _mod_spec(shift, tm), _mod_spec(scale, tm), _mod_spec(gate, tm),
                  pl.BlockSpec((1, D), lambda g, i, j: (0, 0)),
                  pl.BlockSpec((1, D), lambda g, i, j: (0, 0)),
                  pl.BlockSpec((D, LANES), lambda g, i, j: (0, 0)),
                  pl.BlockSpec((1, LANES), lambda g, i, j: (0, 0)),
                  pl.BlockSpec((None, D, Fd), lambda g, i, j: (j, 0, 0)),
                  pl.BlockSpec((None, D, Fd), lambda g, i, j: (j, 0, 0)),
                  pl.BlockSpec((None, Fd, D), lambda g, i, j: (j, 0, 0))],
        out_specs=pl.BlockSpec((None, tm, D), lambda g, i, j: (g, i, 0)),
        out_shape=jax.ShapeDtypeStruct((G, R, D), F32),
        scratch_shapes=[pltpu.VMEM((tm, D), BF16), pltpu.VMEM((tm, D), F32),
                        pltpu.VMEM((tm, LANES), F32)],
        compiler_params=_cparams(("arbitrary", "arbitrary", "arbitrary")),
        name="moe" if moe else "ffn",
    )(x, shift, scale, gate, gpre, gpost, w_router, b_router, wg, wu, wd)


def _scan_body(r_ref, w_ref, k_ref, v_ref, a_ref, b_ref, an_ref, s0_ref, y_ref, s_ref, sa_ref,
               *, tt, dk, norm):
    i = pl.program_id(1)

    @pl.when(i == 0)
    def _():
        s_ref[...] = s0_ref[...]
        sa = jnp.zeros(sa_ref.shape, F32)
        for k in range(dk):
            sa = sa + s0_ref[k] * a_ref[0, k:k + 1, :]
        sa_ref[...] = sa

    def step(t, sa):
        v_t = v_ref[t]
        tn = jnp.minimum(t + 1, tt - 1)
        last = t == tt - 1
        y = jnp.zeros(sa.shape, F32)
        sa_n = jnp.zeros(sa.shape, F32)
        for k in range(dk):
            new = (s_ref[k] * w_ref[t, k:k + 1, :] + sa * b_ref[t, k:k + 1, :]
                   + v_t * k_ref[t, k:k + 1, :])
            s_ref[k] = new
            y = y + new * r_ref[t, k:k + 1, :]
            a_next = jnp.where(last, an_ref[0, k:k + 1, :], a_ref[tn, k:k + 1, :])
            sa_n = sa_n + new * a_next
        if norm == "layer":
            mu = jnp.mean(y, axis=0, keepdims=True)
            d = y - mu
            y = d * lax.rsqrt(jnp.mean(d * d, axis=0, keepdims=True) + RW_LNX_EPS)
        else:
            y = y * lax.rsqrt(jnp.mean(y * y, axis=0, keepdims=True) + NORM_EPS)
        y_ref[t] = y
        return sa_n

    sa_ref[...] = lax.fori_loop(0, tt, step, sa_ref[...])


def scan_call(r, w, k, v, a, b, s0, norm, tt=32):
    G, T, dk, L = r.shape
    dv = v.shape[2]
    tt = min(tt, T)
    nt = T // tt
    tspec = lambda d: pl.BlockSpec((None, tt, d, L), lambda g, i: (g, i, 0, 0))
    nxt = pl.BlockSpec((None, 1, dk, L), lambda g, i: (g, jnp.minimum((i + 1) * tt, T - 1), 0, 0))
    sspec = pl.BlockSpec((None, dk, dv, L), lambda g, i: (g, 0, 0, 0))
    return pl.pallas_call(
        functools.partial(_scan_body, tt=tt, dk=dk, norm=norm),
        grid=(G, nt),
        in_specs=[tspec(dk), tspec(dk), tspec(dk), tspec(dv), tspec(dk), tspec(dk), nxt, sspec],
        out_specs=[tspec(dv), sspec],
        out_shape=[jax.ShapeDtypeStruct((G, T, dv, L), F32),
                   jax.ShapeDtypeStruct((G, dk, dv, L), F32)],
        scratch_shapes=[pltpu.VMEM((dv, L), F32)],
        compiler_params=_cparams(("arbitrary", "arbitrary")),
        name="scan",
    )(r, w, k, v, a, b, a, s0)


def _to_scan(x, lanes_bh):
    B, T, H, C = x.shape
    if lanes_bh:
        return x.transpose(1, 3, 0, 2).reshape(1, T, C, B * H)
    return x.transpose(2, 1, 3, 0)


def _from_scan(y, B, H, lanes_bh):
    G, T, C, L = y.shape
    if lanes_bh:
        return y.reshape(T, C, B, H).transpose(2, 0, 3, 1)
    return y.transpose(3, 1, 0, 2)


def _state_to_scan(s, lanes_bh):
    B, H, dk, dv = s.shape
    if lanes_bh:
        return s.transpose(2, 3, 0, 1).reshape(1, dk, dv, B * H)
    return s.transpose(1, 2, 3, 0)


def _state_from_scan(s, B, H, lanes_bh):
    G, dk, dv, L = s.shape
    if lanes_bh:
        return s.reshape(dk, dv, B, H).transpose(2, 3, 0, 1)
    return s.transpose(3, 0, 1, 2)


NEG = -1e30


def _online_softmax_update(s, v, m_ref, l_ref, acc_ref):
    m_prev = m_ref[...]
    m_new = jnp.maximum(m_prev, jnp.max(s, axis=-1, keepdims=True))
    alpha = jnp.exp(m_prev - m_new)
    p = jnp.exp(s - m_new)
    l_ref[...] = alpha * l_ref[...] + jnp.sum(p, axis=-1, keepdims=True)
    acc_ref[...] = alpha * acc_ref[...] + jnp.dot(p.astype(BF16), v, preferred_element_type=F32)
    m_ref[...] = m_new


_NT = (((1,), (1,)), ((), ()))


def _mla_prompt_body(qi_ref, kj_ref, q_ref, k_ref, vt_ref, o_ref, m_ref, l_ref, acc_ref, *, tq, tk, scale):
    p = pl.program_id(1)
    qi, kj = qi_ref[p], kj_ref[p]

    @pl.when(kj == 0)
    def _():
        m_ref[...] = jnp.full(m_ref.shape, NEG, F32)
        l_ref[...] = jnp.zeros(l_ref.shape, F32)
        acc_ref[...] = jnp.zeros(acc_ref.shape, F32)

    st = lax.dot_general(k_ref[...], q_ref[...], _NT, preferred_element_type=F32) * scale
    kpos = kj * tk + lax.broadcasted_iota(jnp.int32, st.shape, 0)
    tok = qi * tq + lax.broadcasted_iota(jnp.int32, st.shape, 1) // MLA_HEADS
    st = jnp.where(kpos <= tok, st, NEG)
    m_prev = m_ref[...]
    m_new = jnp.maximum(m_prev, jnp.max(st, axis=0, keepdims=True))
    alpha = jnp.exp(m_prev - m_new)
    pt = jnp.exp(st - m_new)
    l_ref[...] = alpha * l_ref[...] + jnp.sum(pt, axis=0, keepdims=True)
    acc_ref[...] = alpha * acc_ref[...] + jnp.dot(vt_ref[...], pt.astype(BF16), preferred_element_type=F32)
    m_ref[...] = m_new

    @pl.when(kj == (qi * tq + tq - 1) // tk)
    def _():
        o_ref[...] = (acc_ref[...] / l_ref[...]).T.astype(o_ref.dtype)


def mla_prompt_call(qcat, kcat, tq=256, tk=256):
    B, TH, W = qcat.shape
    T = kcat.shape[1]
    vt = kcat[:, :, :MLA_KV_LORA].transpose(0, 2, 1)
    pairs = [(i, j) for i in range(T // tq) for j in range((i * tq + tq - 1) // tk + 1)]
    qi_tab = jnp.asarray([p[0] for p in pairs], jnp.int32)
    kj_tab = jnp.asarray([p[1] for p in pairs], jnp.int32)
    R = tq * MLA_HEADS
    return pl.pallas_call(
        functools.partial(_mla_prompt_body, tq=tq, tk=tk, scale=(MLA_NOPE + MLA_ROPE) ** -0.5),
        grid_spec=pltpu.PrefetchScalarGridSpec(
            num_scalar_prefetch=2, grid=(B, len(pairs)),
            in_specs=[pl.BlockSpec((None, R, W), lambda b, p, qi, kj: (b, qi[p], 0)),
                      pl.BlockSpec((None, tk, W), lambda b, p, qi, kj: (b, kj[p], 0)),
                      pl.BlockSpec((None, MLA_KV_LORA, tk), lambda b, p, qi, kj: (b, 0, kj[p]))],
            out_specs=pl.BlockSpec((None, R, MLA_KV_LORA), lambda b, p, qi, kj: (b, qi[p], 0)),
            scratch_shapes=[pltpu.VMEM((1, R), F32), pltpu.VMEM((1, R), F32),
                            pltpu.VMEM((MLA_KV_LORA, R), F32)]),
        out_shape=jax.ShapeDtypeStruct((B, TH, MLA_KV_LORA), BF16),
        compiler_params=_cparams(("arbitrary", "arbitrary")),
        name="mla_prompt",
    )(qi_tab, kj_tab, qcat, kcat, vt)


def _page_stream(pt_ref, streams, npc, n_pages):
    b, t = pl.program_id(0), pl.program_id(1)
    steps = pl.num_programs(1)
    n = b * steps + t
    slot = n % 2

    def each(bb, tt, sl, act):
        page0 = (tt * npc) % n_pages
        for pred, pool, buf, sem in streams:
            def go(pool=pool, buf=buf, sem=sem):
                for j in range(npc):
                    act(pltpu.make_async_copy(pool.at[pt_ref[bb, page0 + j]], buf.at[sl, j], sem.at[sl]))
            live = pred(tt)
            if live is True:
                go()
            else:
                pl.when(live)(go)

    @pl.when(n == 0)
    def _():
        each(b, t, slot, lambda cp: cp.start())

    @pl.when(n + 1 < pl.num_programs(0) * steps)
    def _():
        wrap = t + 1 == steps
        each(jnp.where(wrap, b + 1, b), jnp.where(wrap, 0, t + 1), 1 - slot, lambda cp: cp.start())

    each(b, t, slot, lambda cp: cp.wait())
    return slot


def _mla_sample_body(pt_ref, q_ref, kn_ref, poolc_ref, poolr_ref, o_ref,
                     cbuf, rbuf, semc, semr, m_ref, l_ref, acc_ref, *, npc, n_pages, scale):
    c = pl.program_id(1)
    nch = n_pages // npc
    always = lambda t: True
    slot = _page_stream(pt_ref, [(always, poolc_ref, cbuf, semc), (always, poolr_ref, rbuf, semr)],
                        npc, n_pages)

    @pl.when(c == 0)
    def _():
        m_ref[...] = jnp.full(m_ref.shape, NEG, F32)
        l_ref[...] = jnp.zeros(l_ref.shape, F32)
        acc_ref[...] = jnp.zeros(acc_ref.shape, F32)

    q = q_ref[...]
    q_lat, q_rope = q[:, :MLA_KV_LORA], q[:, MLA_KV_LORA:]
    ckv = cbuf[slot].reshape(npc * PAGE_SIZE, MLA_KV_LORA).astype(BF16)
    rt = jnp.concatenate([rbuf[slot, j] for j in range(npc)], axis=1)
    rt = jnp.concatenate([rt, jnp.zeros((LANES - MLA_ROPE, rt.shape[1]), F32)], axis=0).astype(BF16)
    s = lax.dot_general(q_lat, ckv, _NT, preferred_element_type=F32)
    s = (s + jnp.dot(q_rope, rt, preferred_element_type=F32)) * scale
    _online_softmax_update(s, ckv, m_ref, l_ref, acc_ref)

    @pl.when(c == nch - 1)
    def _():
        kn = kn_ref[...]
        s = lax.dot_general(q, kn, _NT, preferred_element_type=F32) * scale
        qs = lax.broadcasted_iota(jnp.int32, s.shape, 0) // MLA_HEADS
        col = lax.broadcasted_iota(jnp.int32, s.shape, 1)
        s = jnp.where(col <= qs, s, NEG)
        _online_softmax_update(s, kn[:, :MLA_KV_LORA], m_ref, l_ref, acc_ref)
        o_ref[...] = (acc_ref[...] / l_ref[...]).astype(o_ref.dtype)


def mla_sample_call(page_table, qcat, kcat_new, pool_c, pool_rt, npc=16):
    B, R, W = qcat.shape
    n_pages = page_table.shape[1]
    nch = n_pages // npc
    assert n_pages % npc == 0
    return pl.pallas_call(
        functools.partial(_mla_sample_body, npc=npc, n_pages=n_pages, scale=(MLA_NOPE + MLA_ROPE) ** -0.5),
        grid_spec=pltpu.PrefetchScalarGridSpec(
            num_scalar_prefetch=1, grid=(B, nch),
            in_specs=[pl.BlockSpec((None, R, W), lambda b, c, pt: (b, 0, 0)),
                      pl.BlockSpec((None,) + kcat_new.shape[1:], lambda b, c, pt: (b, 0, 0)),
                      pl.BlockSpec(memory_space=pl.ANY), pl.BlockSpec(memory_space=pl.ANY)],
            out_specs=pl.BlockSpec((None, R, MLA_KV_LORA), lambda b, c, pt: (b, 0, 0)),
            scratch_shapes=[pltpu.VMEM((2, npc, PAGE_SIZE, MLA_KV_LORA), F32),
                            pltpu.VMEM((2, npc, MLA_ROPE, PAGE_SIZE), F32),
                            pltpu.SemaphoreType.DMA((2,)), pltpu.SemaphoreType.DMA((2,)),
                            pltpu.VMEM((R, 1), F32), pltpu.VMEM((R, 1), F32),
                            pltpu.VMEM((R, MLA_KV_LORA), F32)]),
        out_shape=jax.ShapeDtypeStruct((B, R, MLA_KV_LORA), BF16),
        compiler_params=_cparams(("arbitrary", "arbitrary")),
        name="mla_sample",
    )(page_table, qcat, kcat_new, pool_c, pool_rt)


def _kmean_body(k_ref, o_ref):
    k = k_ref[...]
    nb = k.shape[0] // MB_BLOCK
    o_ref[...] = jnp.mean(k.reshape(nb, MB_BLOCK, k.shape[1]), axis=1)


def kmean_call(k):
    B, T, W = k.shape
    nb = T // MB_BLOCK
    return pl.pallas_call(
        _kmean_body, grid=(B,),
        in_specs=[pl.BlockSpec((None, T, W), lambda b: (b, 0, 0))],
        out_specs=pl.BlockSpec((None, nb, W), lambda b: (b, 0, 0)),
        out_shape=jax.ShapeDtypeStruct((B, nb, W), F32),
        compiler_params=_cparams(("arbitrary",)),
        name="kmean",
    )(k)


def _moba_prompt_body(qi_ref, kb_ref, last_ref, q_ref, k_ref, vt_ref, kmt_ref, o_ref,
                      sel_ref, m_ref, l_ref, acc_ref, *, nb):
    p = pl.program_id(1)
    qi, kb = qi_ref[p], kb_ref[p]
    own = kb == qi
    q = q_ref[...]

    @pl.when(own)
    def _():
        gate = lax.dot_general(kmt_ref[...], q, _NT, preferred_element_type=F32,
                               precision=lax.Precision.HIGHEST)
        row = lax.broadcasted_iota(jnp.int32, gate.shape, 0)
        j = row % nb
        valid = (j < qi) & (row < MB_HEADS * nb)
        g = jnp.where(valid, gate, -jnp.inf)
        rank = jnp.zeros(g.shape, jnp.int32)
        for d in range(1, nb):
            up = pltpu.roll(g, LANES - d, 0)
            rank += jnp.where((j + d < nb) & (up > g), 1, 0)
            dn = pltpu.roll(g, d, 0)
            rank += jnp.where((j - d >= 0) & (dn >= g), 1, 0)
        sel_ref[...] = jnp.where(valid & (rank < MB_TOPK), 1.0, 0.0)
        m_ref[...] = jnp.full(m_ref.shape, NEG, F32)
        l_ref[...] = jnp.zeros(l_ref.shape, F32)
        acc_ref[...] = jnp.zeros(acc_ref.shape, F32)

    kb16 = k_ref[...].astype(BF16)
    vt16 = vt_ref[...].astype(BF16)
    dist = ((qi - kb) * MB_BLOCK + lax.broadcasted_iota(jnp.int32, (MB_BLOCK, MB_BLOCK), 1)
            - lax.broadcasted_iota(jnp.int32, (MB_BLOCK, MB_BLOCK), 0))
    distf = dist.astype(F32)
    for h in range(MB_HEADS):
        g = h // MB_GROUP
        hs = slice(h * HEAD_DIM, (h + 1) * HEAD_DIM)
        gs = slice(g * HEAD_DIM, (g + 1) * HEAD_DIM)
        st = lax.dot_general(kb16[:, gs], q[:, hs].astype(BF16), _NT, preferred_element_type=F32)
        st = st * (HEAD_DIM ** -0.5) - (2.0 ** (-8.0 * (h + 1) / MB_HEADS)) * distf
        picked = sel_ref[pl.ds(h * nb + kb, 1), :] > 0.5
        st = jnp.where(own, jnp.where(dist >= 0, st, NEG), jnp.where(picked, st, NEG))
        m_prev = m_ref[h]
        m_new = jnp.maximum(m_prev, jnp.max(st, axis=0, keepdims=True))
        alpha = jnp.exp(m_prev - m_new)
        pt = jnp.exp(st - m_new)
        l_ref[h] = alpha * l_ref[h] + jnp.sum(pt, axis=0, keepdims=True)
        acc_ref[h] = alpha * acc_ref[h] + jnp.dot(vt16[gs, :], pt.astype(BF16), preferred_element_type=F32)
        m_ref[h] = m_new

    @pl.when(last_ref[p] == 1)
    def _():
        out_t = jnp.concatenate([acc_ref[h] / l_ref[h] for h in range(MB_HEADS)], axis=0)
        o_ref[...] = out_t.T


def moba_prompt_call(q, k, v, km):
    B, T, Wq = q.shape
    Wk = k.shape[2]
    nb = T // MB_BLOCK
    vt = v.transpose(0, 2, 1)
    kmt = km.transpose(0, 2, 1)
    steps = [(i, kb) for i in range(nb) for kb in [i] + list(range(i))]
    qi_tab = jnp.asarray([s[0] for s in steps], jnp.int32)
    kb_tab = jnp.asarray([s[1] for s in steps], jnp.int32)
    last_tab = jnp.asarray([1 if (i == 0 or kb == i - 1) else 0 for i, kb in steps], jnp.int32)
    return pl.pallas_call(
        functools.partial(_moba_prompt_body, nb=nb),
        grid_spec=pltpu.PrefetchScalarGridSpec(
            num_scalar_prefetch=3, grid=(B, len(steps)),
            in_specs=[pl.BlockSpec((None, MB_BLOCK, Wq), lambda b, p, qi, kb, la: (b, qi[p], 0)),
                      pl.BlockSpec((None, MB_BLOCK, Wk), lambda b, p, qi, kb, la: (b, kb[p], 0)),
                      pl.BlockSpec((None, Wk, MB_BLOCK), lambda b, p, qi, kb, la: (b, 0, kb[p])),
                      pl.BlockSpec((None, LANES, Wq), lambda b, p, qi, kb, la: (b, 0, 0))],
            out_specs=pl.BlockSpec((None, MB_BLOCK, Wq), lambda b, p, qi, kb, la: (b, qi[p], 0)),
            scratch_shapes=[pltpu.VMEM((LANES, MB_BLOCK), F32),
                            pltpu.VMEM((MB_HEADS, 1, MB_BLOCK), F32),
                            pltpu.VMEM((MB_HEADS, 1, MB_BLOCK), F32),
                            pltpu.VMEM((MB_HEADS, HEAD_DIM, MB_BLOCK), F32)]),
        out_shape=jax.ShapeDtypeStruct((B, T, Wq), F32),
        compiler_params=_cparams(("arbitrary", "arbitrary")),
        name="moba_prompt",
    )(qi_tab, kb_tab, last_tab, q, k, vt, kmt)


def _gate_matrix(kmean):
    B, nb, _ = kmean.shape
    kvh = np.arange(MB_HEADS) // MB_GROUP
    km = kmean.reshape(B, nb, MB_KV_HEADS, HEAD_DIM)[:, :, kvh, :].transpose(0, 2, 3, 1)
    eye = jnp.eye(MB_HEADS, dtype=F32)
    km = (km[:, :, :, None, :] * eye[None, :, None, :, None]).reshape(B, MB_HEADS * HEAD_DIM, MB_HEADS * nb)
    return jnp.pad(km, ((0, 0), (0, 0), (0, LANES - MB_HEADS * nb)))


def _moba_sample_body(pt_ref, qbd_ref, kn_ref, vn_ref, rc_ref, poolk_ref, poolv_ref, o_ref,
                      buf, sem, s_ref, p_ref, gate_ref, l_ref, acc_ref, *, npc, n_pages):
    t = pl.program_id(1)
    nch = n_pages // npc
    past = n_pages * PAGE_SIZE
    ph, c = t // nch, t % nch
    nb = past // MB_BLOCK
    ck = npc * PAGE_SIZE
    scale = HEAD_DIM ** -0.5
    R = qbd_ref.shape[0]
    slope = rc_ref[:, 0:1]
    qs = rc_ref[:, 1:2]
    slot = _page_stream(pt_ref, [(lambda tt: tt < nch, poolk_ref, buf, sem),
                                 (lambda tt: tt >= nch, poolv_ref, buf, sem)], npc, n_pages)

    def chunk_t():
        return jnp.concatenate([buf[slot, j] for j in range(npc)], axis=1).astype(BF16)

    glane = lax.broadcasted_iota(jnp.int32, gate_ref.shape, 1)

    @pl.when(ph == 0)
    def _():
        s = jnp.dot(qbd_ref[...].astype(BF16), chunk_t(), preferred_element_type=F32)
        s_ref[:, pl.ds(pl.multiple_of(c * ck, ck), ck)] = s
        g = jnp.where(c == 0, jnp.full(gate_ref.shape, -jnp.inf, F32), gate_ref[...])
        for i in range(ck // MB_BLOCK):
            bsum = jnp.sum(s[:, i * MB_BLOCK:(i + 1) * MB_BLOCK], axis=-1, keepdims=True)
            g = jnp.where(glane == c * (ck // MB_BLOCK) + i, bsum, g)
        gate_ref[...] = g

    @pl.when(t == nch - 1)
    def _():
        g = gate_ref[...]
        picks = []
        for _ in range(MB_TOPK):
            mx = jnp.max(g, axis=-1, keepdims=True)
            idx = jnp.min(jnp.where(g == mx, glane, LANES), axis=-1, keepdims=True)
            picks.append(idx)
            g = jnp.where(glane == idx, -jnp.inf, g)

        qb = qbd_ref[...].astype(BF16)
        kn = kn_ref[...].astype(BF16)
        s_own = lax.dot_general(qb, kn, _NT, preferred_element_type=F32) * scale
        col = lax.broadcasted_iota(jnp.int32, s_own.shape, 1).astype(F32)
        s_own = jnp.where(col <= qs, s_own - slope * (qs - col), NEG)
        m0 = jnp.max(s_own, axis=-1, keepdims=True)

        kcol = lax.broadcasted_iota(jnp.int32, (R, MB_BLOCK), 1).astype(F32)

        def block_scores(j):
            s = s_ref[:, pl.ds(pl.multiple_of(j * MB_BLOCK, MB_BLOCK), MB_BLOCK)]
            kpos = jnp.asarray(j * MB_BLOCK).astype(F32) + kcol
            s = s * scale - slope * ((past + qs) - kpos)
            picked = (picks[0] == j) | (picks[1] == j) | (picks[2] == j)
            return jnp.where(picked, s, NEG)

        m_acc = lax.fori_loop(0, nb, lambda j, a: jnp.maximum(a, block_scores(j)),
                              jnp.full((R, MB_BLOCK), NEG, F32))
        m = jnp.maximum(m0, jnp.max(m_acc, axis=-1, keepdims=True))

        def fill(j, a):
            p = jnp.exp(block_scores(j) - m)
            p_ref[:, pl.ds(pl.multiple_of(j * MB_BLOCK, MB_BLOCK), MB_BLOCK)] = p.astype(BF16)
            return a + p

        l_acc = lax.fori_loop(0, nb, fill, jnp.zeros((R, MB_BLOCK), F32))
        p_own = jnp.exp(s_own - m)
        l_ref[...] = jnp.sum(p_own, axis=-1, keepdims=True) + jnp.sum(l_acc, axis=-1, keepdims=True)
        acc_ref[...] = jnp.dot(p_own.astype(BF16), vn_ref[...].astype(BF16), preferred_element_type=F32)

    @pl.when(ph == 1)
    def _():
        pb = p_ref[:, pl.ds(pl.multiple_of(c * ck, ck), ck)]
        acc_ref[...] += lax.dot_general(pb, chunk_t(), _NT, preferred_element_type=F32)

    @pl.when(t == 2 * nch - 1)
    def _():
        out = acc_ref[...] / l_ref[...]
        S = R // MB_HEADS
        o_ref[...] = jnp.concatenate(
            [out[h * S:(h + 1) * S, (h // MB_GROUP) * HEAD_DIM:(h // MB_GROUP + 1) * HEAD_DIM]
             for h in range(MB_HEADS)], axis=-1)


def moba_sample_call(page_table, q, k_new, v_new, pool_kt, pool_vt, npc=16):
    B, S, Wq = q.shape
    Wk = k_new.shape[2]
    n_pages = page_table.shape[1]
    past = n_pages * PAGE_SIZE
    nch = n_pages // npc
    assert past % MB_BLOCK == 0 and past // MB_BLOCK >= MB_TOPK and n_pages % npc == 0
    R = MB_HEADS * S
    kvh = np.arange(MB_HEADS) // MB_GROUP
    onehot = jnp.asarray(np.eye(MB_KV_HEADS, dtype=np.float32)[kvh])
    qh = q.reshape(B, S, MB_HEADS, HEAD_DIM).transpose(0, 2, 1, 3)
    qbd = (qh[:, :, :, None, :] * onehot[None, :, None, :, None]).reshape(B, R, Wk)
    pad = lambda t: jnp.pad(t, ((0, 0), (0, LANES - S), (0, 0)))
    rc = np.zeros((R, LANES), np.float32)
    rc[:, 0] = np.repeat([2.0 ** (-8.0 * (h + 1) / MB_HEADS) for h in range(MB_HEADS)], S)
    rc[:, 1] = np.tile(np.arange(S), MB_HEADS)
    full = lambda shp: pl.BlockSpec((None,) + shp, lambda b, t, pt: (b, 0, 0))
    return pl.pallas_call(
        functools.partial(_moba_sample_body, npc=npc, n_pages=n_pages),
        grid_spec=pltpu.PrefetchScalarGridSpec(
            num_scalar_prefetch=1, grid=(B, 2 * nch),
            in_specs=[full((R, Wk)), full((LANES, Wk)), full((LANES, Wk)),
                      pl.BlockSpec((R, LANES), lambda b, t, pt: (0, 0)),
                      pl.BlockSpec(memory_space=pl.ANY), pl.BlockSpec(memory_space=pl.ANY)],
            out_specs=full((S, Wq)),
            scratch_shapes=[pltpu.VMEM((2, npc, Wk, PAGE_SIZE), F32), pltpu.SemaphoreType.DMA((2,)),
                            pltpu.VMEM((R, past), F32), pltpu.VMEM((R, past), BF16),
                            pltpu.VMEM((R, LANES), F32),
                            pltpu.VMEM((R, 1), F32), pltpu.VMEM((R, Wk), F32)]),
        out_shape=jax.ShapeDtypeStruct((B, S, Wq), F32),
        compiler_params=_cparams(("arbitrary", "arbitrary")),
        name="moba_sample",
    )(page_table, qbd, pad(k_new), pad(v_new), jnp.asarray(rc), pool_kt, pool_vt)


def _l2norm(x):
    return x * lax.rsqrt(jnp.sum(x * x, axis=-1, keepdims=True) + 1e-12)


def _rope_tables(pos):
    half = MLA_ROPE // 2
    inv = ROPE_THETA ** (-jnp.arange(half, dtype=F32) / half)
    ang = pos.astype(F32)[:, None] * inv[None, :]
    return jnp.cos(ang), jnp.sin(ang)


def _rwkv_mix(p, prev, S0, prm, lanes_bh):
    mu, w0, w2, a0, a2, g2, k_k, k_a, r_k, lnx_g, lnx_b = prm
    B, T, _ = p.shape
    H = RW_HEADS
    p_prev = jnp.concatenate([prev[:, None, :], p[:, :-1]], axis=1)
    pm = p + (p_prev - p) * mu
    cuts = [int(c) for c in np.cumsum(RW_SPLITS)[:-1]]
    r, wl, k, v, al, gl = jnp.split(pm, cuts, axis=-1)
    w_log = -jax.nn.softplus(-(w0 + jnp.tanh(wl) @ w2)) - 0.5
    decay = jnp.exp(-jnp.exp(w_log))
    a = jax.nn.sigmoid(a0 + al @ a2)
    g = jax.nn.sigmoid(gl) @ g2
    heads = lambda t: t.reshape(B, T, H, HEAD_DIM)
    kk = _l2norm(heads(k * k_k))
    k = k * (1.0 + (a - 1.0) * k_a)
    r_h, k_h, v_h, w_h, a_h = heads(r), heads(k), heads(v), heads(decay), heads(a)
    aa, bb = -kk, kk * a_h
    ts = lambda t: _to_scan(t, lanes_bh)
    s0 = _state_to_scan(jnp.swapaxes(S0, -1, -2), lanes_bh)
    y, S = scan_call(ts(r_h), ts(w_h), ts(k_h), ts(v_h), ts(aa), ts(bb), s0, "layer")
    y = _from_scan(y, B, H, lanes_bh).reshape(B, T, RW_WIDTH) * lnx_g + lnx_b
    S = jnp.swapaxes(_state_from_scan(S, B, H, lanes_bh), -1, -2)
    bonus = jnp.sum(r_h * k_h * r_k, axis=-1, keepdims=True) * v_h
    y = (y + bonus.reshape(B, T, RW_WIDTH)) * g
    return y, S, p[:, -1]


def _gdn_mix(qkv, z, a_raw, b_raw, conv_buf, S0, prm, lanes_bh):
    conv_w, a_log, dt_bias, norm_g = prm
    B, T, _ = qkv.shape
    H = GD_HEADS
    xx = jnp.concatenate([conv_buf, qkv], axis=1)
    y = sum(xx[:, j:j + T] * conv_w[j] for j in range(GD_CONV))
    y = _silu(y)
    q, k, v = jnp.split(y, [H * GD_DK, 2 * H * GD_DK], axis=-1)
    q = _l2norm(q.reshape(B, T, H, GD_DK)) * (GD_DK ** -0.5)
    k = _l2norm(k.reshape(B, T, H, GD_DK))
    v = v.reshape(B, T, H, GD_DV)
    beta = jax.nn.sigmoid(b_raw)[..., None]
    alpha = jnp.exp(-jnp.exp(a_log) * jax.nn.softplus(a_raw + dt_bias))[..., None]
    ts = lambda t: _to_scan(t, lanes_bh)
    w = jnp.broadcast_to(alpha, k.shape)
    y, S = scan_call(ts(q), ts(w), ts(k), ts(beta * v), ts(k), ts(-beta * alpha * k),
                     _state_to_scan(S0, lanes_bh), "rms")
    o = _from_scan(y, B, H, lanes_bh) * norm_g * _silu(z.reshape(B, T, H, GD_DV))
    S = _state_from_scan(S, B, H, lanes_bh)
    return o.reshape(B, T, H * GD_DV), S, xx[:, -(GD_CONV - 1):]


def _bdiag(w):
    H, a, b = w.shape
    eye = jnp.eye(H, dtype=w.dtype)
    return (eye[:, None, :, None] * w[:, :, None, :]).reshape(H * a, H * b)


def kernel(x_prompt, x_sample, cache_moba_k, cache_moba_v, cache_mla_ckv, cache_mla_krope, state_rwkv_wkv, state_rwkv_shift, state_gdn, state_gdn_conv, page_table, c_prompt, c_sample, w_ada, b_ada, g_mix_pre, g_mix_post, g_ff_pre, g_ff_post, w_in_even, w_out_even, rw_mu, rw_w0, rw_w2, rw_a0, rw_a2, rw_g2, rw_kk, rw_ka, rw_rk, rw_lnx_g, rw_lnx_b, ffn_w_gate, ffn_w_up, ffn_w_down, w_in_odd, w_out_odd, mla_q_norm, mla_w_q_up, mla_kv_norm, mla_w_uk, mla_w_uv, gdn_conv_w, gdn_a_log, gdn_dt_bias, gdn_norm_g, moe_w_router, moe_b_router, moe_w_gate, moe_w_up, moe_w_down):
    Bp, Tp, D = x_prompt.shape
    Bs, Ts, _ = x_sample.shape
    past = page_table.shape[1] * PAGE_SIZE
    Ms = Bs * Ts
    row = lambda v: v.reshape(1, -1)

    xp = x_prompt
    xs = x_sample.reshape(1, Ms, D)
    c_all = jnp.concatenate([c_prompt, c_sample], axis=0)

    outs_p, outs_s = {}, {}
    for layer in range(w_ada.shape[0]):
        i = layer // 2
        mod = adaln_call(c_all, w_ada[layer].astype(BF16), row(b_ada[layer]))
        mp = [m.reshape(Bp, 1, D) for m in jnp.split(mod[:Bp], 6, axis=-1)]
        ms = [jnp.broadcast_to(m[:, None, :], (Bs, Ts, D)).reshape(1, Ms, D)
              for m in jnp.split(mod[Bp:], 6, axis=-1)]
        gpre, gpost = row(g_mix_pre[layer]), row(g_mix_post[layer])
        if layer % 2 == 0:
            w_in = w_in_even[i].astype(BF16)
            splits = (RW_PROJ, MB_HEADS * HEAD_DIM, MB_KV_HEADS * HEAD_DIM, MB_KV_HEADS * HEAD_DIM)
            rw_prm = (rw_mu[i], rw_w0[i], rw_w2[i], rw_a0[i], rw_a2[i], rw_g2[i], rw_kk[i], rw_ka[i],
                      rw_rk[i], rw_lnx_g[i], rw_lnx_b[i])
            w_out = w_out_even[i].astype(BF16)
            p_rw, q, k, v = modmm_call(xp, mp[0], mp[1], gpre, w_in, splits)
            y_rw, S_p, shift_p = _rwkv_mix(p_rw, jnp.zeros((Bp, RW_PROJ), F32),
                                           jnp.zeros((Bp, RW_HEADS, HEAD_DIM, HEAD_DIM), F32), rw_prm, True)
            k4 = k.reshape(Bp, Tp, MB_KV_HEADS, HEAD_DIM)
            v4 = v.reshape(Bp, Tp, MB_KV_HEADS, HEAD_DIM)
            y_mb = moba_prompt_call(q, k, v, _gate_matrix(kmean_call(k)))
            xp = outproj_call(y_rw, y_mb, w_out[:RW_WIDTH], w_out[RW_WIDTH:], xp, mp[2], gpost)
            outs_p.update(mk=k4[None], mv=v4[None], wkv=S_p[None], shift=shift_p[None])
            p_rw, q, k, v = modmm_call(xs, ms[0], ms[1], gpre, w_in, splits)
            y_rw, S_s, shift_s = _rwkv_mix(p_rw.reshape(Bs, Ts, RW_PROJ), state_rwkv_shift[i],
                                           state_rwkv_wkv[i], rw_prm, False)
            k4 = k.reshape(Bs, Ts, MB_KV_HEADS, HEAD_DIM)
            v4 = v.reshape(Bs, Ts, MB_KV_HEADS, HEAD_DIM)
            n_pool = cache_moba_k.shape[1]
            pool_t = lambda c: c[i].transpose(0, 2, 3, 1).reshape(n_pool, -1, PAGE_SIZE)
            y_mb = moba_sample_call(page_table, q.reshape(Bs, Ts, -1), k.reshape(Bs, Ts, -1), v.reshape(Bs, Ts, -1),
                                    pool_t(cache_moba_k), pool_t(cache_moba_v))
            xs = outproj_call(y_rw.reshape(1, Ms, RW_WIDTH), y_mb.reshape(1, Ms, -1),
                              w_out[:RW_WIDTH], w_out[RW_WIDTH:], xs, ms[2], gpost)
            outs_s.update(mk=k4[None], mv=v4[None], wkv=S_s[None], shift=shift_s[None])
        else:
            w_in = w_in_odd[i]
            cuts = [int(c) for c in np.cumsum(ODD_SPLITS)[:-1]]
            w_cq, w_ckv, w_kr, w_qkv, w_z, w_a, w_b = jnp.split(w_in, cuts, axis=1)
            padc = lambda w, n: jnp.pad(w, ((0, 0), (0, n - w.shape[1])))
            w_in_p = jnp.concatenate([w_cq, w_ckv, padc(w_kr, LANES), w_qkv, w_z,
                                      padc(jnp.concatenate([w_a, w_b], axis=1), LANES)], axis=1).astype(BF16)
            splits = (MLA_Q_LORA, MLA_KV_LORA, LANES, GD_QKV, GD_HEADS * GD_DV, LANES)
            w_out = w_out_odd[i].astype(BF16)
            gd_prm = (gdn_conv_w[i], gdn_a_log[i], gdn_dt_bias[i], gdn_norm_g[i])
            w_uv_bd = _bdiag(mla_w_uv[i].transpose(1, 0, 2)).astype(BF16)
            w_uk_bd = _bdiag(mla_w_uk[i].transpose(1, 2, 0)).astype(BF16)
            wq = mla_w_q_up[i].reshape(MLA_Q_LORA, MLA_HEADS, MLA_NOPE + MLA_ROPE)
            w_qn = wq[:, :, :MLA_NOPE].reshape(MLA_Q_LORA, -1).astype(BF16)
            w_qr = jnp.pad(wq[:, :, MLA_NOPE:], ((0, 0), (0, 0), (0, LANES - MLA_ROPE)))
            w_qr = w_qr.reshape(MLA_Q_LORA, -1).astype(BF16)
            w_oa, w_ob = w_out[:MLA_HEADS * MLA_V], w_out[MLA_HEADS * MLA_V:]

            def odd(x, m, B, T, cosf, sinf, conv_buf, S0, lanes_bh):
                cq, ckv, kr, qkv, z, ab = modmm_call(x, m[0], m[1], gpre, w_in_p, splits)
                qcat, kcat, ckv_n, k_rope = mla_prep_call(
                    cq, ckv, kr, cosf, sinf, row(mla_q_norm[i]), row(mla_kv_norm[i]), w_qn, w_qr, w_uk_bd)
                qkv, z = qkv.reshape(B, T, -1), z.reshape(B, T, -1)
                ab = ab.reshape(B, T, LANES)
                a_raw, b_raw = ab[..., :GD_HEADS], ab[..., GD_HEADS:2 * GD_HEADS]
                y_gdn, S, buf = _gdn_mix(qkv, z, a_raw, b_raw, conv_buf, S0, gd_prm, lanes_bh)
                ckv_n = ckv_n.reshape(B, T, MLA_KV_LORA)
                k_rope = k_rope.reshape(B, T, LANES)[..., :MLA_ROPE]
                return qcat, kcat, y_gdn, ckv_n, k_rope, S, buf

            cosf, sinf = _rope_rows(jnp.arange(Tp))
            qcat, kcat, y_gdn, ckv_n, k_rope, S_p, buf_p = odd(
                xp, mp, Bp, Tp, cosf, sinf, jnp.zeros((Bp, GD_CONV - 1, GD_QKV), F32),
                jnp.zeros((Bp, GD_HEADS, GD_DK, GD_DV), F32), True)
            o_lat = mla_prompt_call(qcat.reshape(Bp, Tp * MLA_HEADS, 2 * LANES), kcat)
            xp = outproj_call(o_lat.reshape(Bp, Tp, -1), y_gdn, w_oa, w_ob, xp, mp[2], gpost, wpre=w_uv_bd)
            outs_p.update(ckv=ckv_n[None], krope=k_rope[None], gdn=S_p[None], conv=buf_p[None])
            cosf, sinf = _rope_rows(jnp.tile(past + jnp.arange(Ts), Bs))
            qcat, kcat, y_gdn, ckv_n, k_rope, S_s, buf_s = odd(
                xs, ms, Bs, Ts, cosf, sinf, state_gdn_conv[i], state_gdn[i], False)
            kn = jnp.pad(kcat.reshape(Bs, Ts, 2 * LANES), ((0, 0), (0, LANES - Ts), (0, 0)))
            o_lat = mla_sample_call(page_table, qcat.reshape(Bs, Ts * MLA_HEADS, 2 * LANES), kn,
                                    cache_mla_ckv[i], cache_mla_krope[i].transpose(0, 2, 1))
            xs = outproj_call(o_lat.reshape(1, Ms, -1), y_gdn.reshape(1, Ms, -1), w_oa, w_ob, xs, ms[2], gpost,
                              wpre=w_uv_bd)
            outs_s.update(ckv=ckv_n[None], krope=k_rope[None], gdn=S_s[None], conv=buf_s[None])

        gpre, gpost = row(g_ff_pre[layer]), row(g_ff_post[layer])
        if layer % 2 == 0:
            J = 2
            Fd = ffn_w_gate.shape[2] // J
            wg = ffn_w_gate[i].astype(BF16).reshape(D, J, Fd).transpose(1, 0, 2)
            wu = ffn_w_up[i].astype(BF16).reshape(D, J, Fd).transpose(1, 0, 2)
            wd = ffn_w_down[i].astype(BF16).reshape(J, Fd, D)
            xp = ffn_call(xp, mp[3], mp[4], mp[5], gpre, gpost, wg, wu, wd)
            xs = ffn_call(xs, ms[3], ms[4], ms[5], gpre, gpost, wg, wu, wd)
        else:
            wr = jnp.pad(moe_w_router[i], ((0, 0), (0, LANES - N_EXPERTS)))
            br = jnp.pad(row(moe_b_router[i]), ((0, 0), (0, LANES - N_EXPERTS)))
            wg, wu, wd = (w[i].astype(BF16) for w in (moe_w_gate, moe_w_up, moe_w_down))
            xp = ffn_call(xp, mp[3], mp[4], mp[5], gpre, gpost, wg, wu, wd, wr, br)
            xs = ffn_call(xs, ms[3], ms[4], ms[5], gpre, gpost, wg, wu, wd, wr, br)

    names = ('mk', 'mv', 'wkv', 'shift', 'ckv', 'krope', 'gdn', 'conv')
    return ((xp, xs.reshape(Bs, Ts, D)) + tuple(outs_p[n] for n in names)
            + tuple(outs_s[n] for n in names))
```

```python
import functools
import math

import jax
import jax.numpy as jnp
import numpy as np
from jax import lax
from jax.experimental import pallas as pl
from jax.experimental.pallas import tpu as pltpu

F32 = jnp.float32
BF16 = jnp.bfloat16

NORM_EPS = 1e-6
HEAD_DIM = 64
PAGE_SIZE = 128
RW_HEADS = 8
RW_WIDTH = RW_HEADS * HEAD_DIM
RW_SPLITS = (RW_WIDTH, 64, RW_WIDTH, RW_WIDTH, 64, 128)
RW_PROJ = sum(RW_SPLITS)
RW_LNX_EPS = 64e-5
MB_HEADS = 8
MB_KV_HEADS = 4
MB_GROUP = MB_HEADS // MB_KV_HEADS
MB_BLOCK = 256
MB_TOPK = 3
MLA_HEADS = 8
MLA_NOPE = 64
MLA_ROPE = 32
MLA_V = 64
MLA_Q_LORA = 256
MLA_KV_LORA = 128
ROPE_THETA = 10000.0
GD_HEADS = 8
GD_DK = 64
GD_DV = 64
GD_CONV = 4
GD_QKV = GD_HEADS * (2 * GD_DK + GD_DV)
ODD_SPLITS = (MLA_Q_LORA, MLA_KV_LORA, MLA_ROPE, GD_QKV, GD_HEADS * GD_DV, GD_HEADS, GD_HEADS)
N_EXPERTS = 8

LANES = 128
VMEM_LIMIT = 56 * 1024 * 1024


def _cparams(sem):
    return pltpu.CompilerParams(dimension_semantics=sem, vmem_limit_bytes=VMEM_LIMIT)


def _silu(x):
    return x * jax.nn.sigmoid(x)


def _rms(x, eps=NORM_EPS):
    return x * lax.rsqrt(jnp.mean(x * x, axis=-1, keepdims=True) + eps)


def _adaln_body(c_ref, w_ref, b_ref, o_ref):
    c = c_ref[...]
    o_ref[...] = jnp.dot(_silu(c).astype(BF16), w_ref[...], preferred_element_type=F32) + b_ref[...]


def adaln_call(c, w, b, tn=1536):
    R, D = c.shape
    N = w.shape[1]
    return pl.pallas_call(
        _adaln_body,
        grid=(N // tn,),
        in_specs=[pl.BlockSpec((R, D), lambda j: (0, 0)),
                  pl.BlockSpec((D, tn), lambda j: (0, j)),
                  pl.BlockSpec((1, tn), lambda j: (0, j))],
        out_specs=pl.BlockSpec((R, tn), lambda j: (0, j)),
        out_shape=jax.ShapeDtypeStruct((R, N), F32),
        compiler_params=_cparams(("arbitrary",)),
        name="adaln",
    )(c, w, b)


def _mod_spec(mod, tm):
    if mod.shape[1] == 1:
        return pl.BlockSpec((None, 1, mod.shape[2]), lambda g, i, *_: (g, 0, 0))
    return pl.BlockSpec((None, tm, mod.shape[2]), lambda g, i, *_: (g, i, 0))


def _modmm_body(x_ref, sh_ref, sc_ref, g_ref, w_ref, *o_refs, splits):
    h = _rms(x_ref[...]) * g_ref[...] * (1.0 + sc_ref[...]) + sh_ref[...]
    y = jnp.dot(h.astype(BF16), w_ref[...], preferred_element_type=F32)
    off = 0
    for o_ref, n in zip(o_refs, splits):
        o_ref[...] = y[:, off:off + n].astype(o_ref.dtype)
        off += n


def modmm_call(x, shift, scale, gain, w, splits, tm=512):
    G, R, D = x.shape
    tm = min(tm, R)
    N = w.shape[1]
    assert sum(splits) == N and all(s % LANES == 0 for s in splits)
    return pl.pallas_call(
        functools.partial(_modmm_body, splits=splits),
        grid=(G, R // tm),
        in_specs=[pl.BlockSpec((None, tm, D), lambda g, i: (g, i, 0)),
                  _mod_spec(shift, tm), _mod_spec(scale, tm),
                  pl.BlockSpec((1, D), lambda g, i: (0, 0)),
                  pl.BlockSpec((D, N), lambda g, i: (0, 0))],
        out_specs=[pl.BlockSpec((None, tm, n), lambda g, i: (g, i, 0)) for n in splits],
        out_shape=[jax.ShapeDtypeStruct((G, R, n), F32) for n in splits],
        compiler_params=_cparams(("arbitrary", "arbitrary")),
        name="modmm",
    )(x, shift, scale, gain, w)


def _outproj_body(ya_ref, yb_ref, wpre_ref, wa_ref, wb_ref, x_ref, ga_ref, gp_ref, o_ref, *, pre):
    ya = ya_ref[...].astype(BF16)
    if pre:
        ya = jnp.dot(ya, wpre_ref[...], preferred_element_type=F32).astype(BF16)
    y = jnp.dot(ya, wa_ref[...], preferred_element_type=F32)
    y = y + jnp.dot(yb_ref[...].astype(BF16), wb_ref[...], preferred_element_type=F32)
    o_ref[...] = x_ref[...] + ga_ref[...] * (_rms(y) * gp_ref[...])


def outproj_call(ya, yb, wa, wb, x, gate, gpost, wpre=None, tm=512):
    G, R, D = x.shape
    tm = min(tm, R)
    Ka, Kb = ya.shape[2], yb.shape[2]
    pre = wpre is not None
    if not pre:
        wpre = jnp.zeros((8, LANES), BF16)
    return pl.pallas_call(
        functools.partial(_outproj_body, pre=pre),
        grid=(G, R // tm),
        in_specs=[pl.BlockSpec((None, tm, Ka), lambda g, i: (g, i, 0)),
                  pl.BlockSpec((None, tm, Kb), lambda g, i: (g, i, 0)),
                  pl.BlockSpec(wpre.shape, lambda g, i: (0, 0)),
                  pl.BlockSpec(wa.shape, lambda g, i: (0, 0)),
                  pl.BlockSpec(wb.shape, lambda g, i: (0, 0)),
                  pl.BlockSpec((None, tm, D), lambda g, i: (g, i, 0)),
                  _mod_spec(gate, tm),
                  pl.BlockSpec((1, D), lambda g, i: (0, 0))],
        out_specs=pl.BlockSpec((None, tm, D), lambda g, i: (g, i, 0)),
        out_shape=jax.ShapeDtypeStruct((G, R, D), F32),
        compiler_params=_cparams(("arbitrary", "arbitrary")),
        name="outproj",
    )(ya, yb, wpre, wa, wb, x, gate, gpost)


def _mla_prep_body(cq_ref, ckv_ref, kr_ref, cos_ref, sin_ref, qg_ref, kg_ref, wqn_ref, wqr_ref, wuk_ref,
                   qcat_ref, kcat_ref, ckvn_ref, krope_ref):
    cosf, sinf = cos_ref[...], sin_ref[...]
    lane = lax.broadcasted_iota(jnp.int32, cosf.shape, 1)
    half = MLA_ROPE // 2

    def rope(x):
        rot = jnp.where(lane < half, pltpu.roll(x, LANES - half, 1), pltpu.roll(x, half, 1))
        return x * cosf + rot * sinf

    cqn = (_rms(cq_ref[...]) * qg_ref[...]).astype(BF16)
    qn = jnp.dot(cqn, wqn_ref[...], preferred_element_type=F32).astype(BF16)
    qr = jnp.dot(cqn, wqr_ref[...], preferred_element_type=F32)
    ql = jnp.dot(qn, wuk_ref[...], preferred_element_type=F32)
    for h in range(MLA_HEADS):
        qcat_ref[:, 2 * h * LANES:(2 * h + 1) * LANES] = ql[:, h * LANES:(h + 1) * LANES].astype(BF16)
        qcat_ref[:, (2 * h + 1) * LANES:(2 * h + 2) * LANES] = rope(qr[:, h * LANES:(h + 1) * LANES]).astype(BF16)
    ckvn = _rms(ckv_ref[...]) * kg_ref[...]
    krope = rope(kr_ref[...])
    ckvn_ref[...] = ckvn
    krope_ref[...] = krope
    kcat_ref[:, :LANES] = ckvn.astype(BF16)
    kcat_ref[:, LANES:] = krope.astype(BF16)


def mla_prep_call(cq, ckv, kr, cosf, sinf, q_gain, kv_gain, w_qn, w_qr, w_uk_bd, tm=512):
    G, R, _ = cq.shape
    tm = min(tm, R)
    rows = lambda n: pl.BlockSpec((None, tm, n), lambda g, i: (g, i, 0))
    const = lambda a: pl.BlockSpec(a.shape, lambda g, i: (0, 0))
    W = 2 * LANES * MLA_HEADS
    return pl.pallas_call(
        _mla_prep_body,
        grid=(G, R // tm),
        in_specs=[rows(cq.shape[2]), rows(LANES), rows(LANES),
                  pl.BlockSpec((tm, LANES), lambda g, i: (i, 0)), pl.BlockSpec((tm, LANES), lambda g, i: (i, 0)),
                  const(q_gain), const(kv_gain), const(w_qn), const(w_qr), const(w_uk_bd)],
        out_specs=[rows(W), rows(2 * LANES), rows(LANES), rows(LANES)],
        out_shape=[jax.ShapeDtypeStruct((G, R, W), BF16), jax.ShapeDtypeStruct((G, R, 2 * LANES), BF16),
                   jax.ShapeDtypeStruct((G, R, LANES), F32), jax.ShapeDtypeStruct((G, R, LANES), F32)],
        compiler_params=_cparams(("arbitrary", "arbitrary")),
        name="mla_prep",
    )(cq, ckv, kr, cosf, sinf, q_gain, kv_gain, w_qn, w_qr, w_uk_bd)


def _rope_rows(pos):
    cos, sin = _rope_tables(pos)
    z = jnp.zeros((pos.shape[0], LANES - MLA_ROPE), F32)
    return jnp.concatenate([cos, cos, z], axis=1), jnp.concatenate([-sin, sin, z], axis=1)


def _ffn_body(x_ref, sh_ref, sc_ref, ga_ref, gpre_ref, gpost_ref, wr_ref, br_ref,
              wg_ref, wu_ref, wd_ref, o_ref, h_ref, acc_ref, rw_ref, *, moe):
    j = pl.program_id(2)

    @pl.when(j == 0)
    def _():
        h = _rms(x_ref[...]) * gpre_ref[...] * (1.0 + sc_ref[...]) + sh_ref[...]
        h_ref[...] = h.astype(BF16)
        acc_ref[...] = jnp.zeros_like(acc_ref)
        if moe:
            logits = jnp.dot(h, wr_ref[...], preferred_element_type=F32,
                             precision=lax.Precision.HIGHEST) + br_ref[...]
            lane = lax.broadcasted_iota(jnp.int32, logits.shape, 1)
            neg = jnp.float32(-jnp.inf)
            logits = jnp.where(lane < N_EXPERTS, logits, neg)
            m1 = jnp.max(logits, axis=-1, keepdims=True)
            i1 = jnp.min(jnp.where(logits == m1, lane, LANES), axis=-1, keepdims=True)
            rest = jnp.where(lane == i1, neg, logits)
            m2 = jnp.max(rest, axis=-1, keepdims=True)
            i2 = jnp.min(jnp.where(rest == m2, lane, LANES), axis=-1, keepdims=True)
            e2 = jnp.exp(m2 - m1)
            w1 = 1.0 / (1.0 + e2)
            w2 = e2 / (1.0 + e2)
            rw_ref[...] = jnp.where(lane == i1, w1, 0.0) + jnp.where(lane == i2, w2, 0.0)

    hb = h_ref[...]
    g = jnp.dot(hb, wg_ref[...], preferred_element_type=F32)
    u = jnp.dot(hb, wu_ref[...], preferred_element_type=F32)
    f = jnp.dot((_silu(g) * u).astype(BF16), wd_ref[...], preferred_element_type=F32)
    if moe:
        lane = lax.broadcasted_iota(jnp.int32, rw_ref.shape, 1)
        we = jnp.sum(jnp.where(lane == j, rw_ref[...], 0.0), axis=-1, keepdims=True)
        f = f * we
    acc_ref[...] += f

    @pl.when(j == pl.num_programs(2) - 1)
    def _():
        o_ref[...] = x_ref[...] + ga_ref[...] * (_rms(acc_ref[...]) * gpost_ref[...])


def ffn_call(x, shift, scale, gate, gpre, gpost, wg, wu, wd, w_router=None, b_router=None, tm=512):
    G, R, D = x.shape
    tm = min(tm, R)
    J, _, Fd = wg.shape
    moe = w_router is not None
    if not moe:
        w_router = jnp.zeros((D, LANES), F32)
        b_router = jnp.zeros((1, LANES), F32)
    return pl.pallas_call(
        functools.partial(_ffn_body, moe=moe),
        grid=(G, R // tm, J),
        in_specs=[pl.BlockSpec((None, tm, D), lambda g, i, j: (g, i, 0)),
                  _mod_spec(shift, tm), _mod_spec(scale, tm), _mod_spec(gate, tm),
                  pl.BlockSpec((1, D), lambda g, i, j: (0, 0)),
                  pl.BlockSpec((1, D), lambda g, i, j: (0, 0)),
                  pl.BlockSpec((D, LANES), lambda g, i, j: (0, 0)),
                  pl.BlockSpec((1, LANES), lambda g, i, j: (0, 0)),
                  pl.BlockSpec((None, D, Fd), lambda g, i, j: (j, 0, 0)),
                  pl.BlockSpec((None, D, Fd), lambda g, i, j: (j, 0, 0)),
                  pl.BlockSpec((None, Fd, D), lambda g, i, j: (j, 0, 0))],
        out_specs=pl.BlockSpec((None, tm, D), lambda g, i, j: (g, i, 0)),
        out_shape=jax.ShapeDtypeStruct((G, R, D), F32),
        scratch_shapes=[pltpu.VMEM((tm, D), BF16), pltpu.VMEM((tm, D), F32),
                        pltpu.VMEM((tm, LANES), F32)],
        compiler_params=_cparams(("arbitrary", "arbitrary", "arbitrary")),
        name="moe" if moe else "ffn",
    )(x, shift, scale, gate, gpre, gpost, w_router, b_router, wg, wu, wd)


def _scan_body(r_ref, w_ref, k_ref, v_ref, a_ref, b_ref, an_ref, s0_ref, y_ref, s_ref, sa_ref,
               *, tt, dk, norm):
    i = pl.program_id(1)

    @pl.when(i == 0)
    def _():
        s_ref[...] = s0_ref[...]
        sa = jnp.zeros(sa_ref.shape, F32)
        for k in range(dk):
            sa = sa + s0_ref[k] * a_ref[0, k:k + 1, :]
        sa_ref[...] = sa

    def step(t, sa):
        v_t = v_ref[t]
        tn = jnp.minimum(t + 1, tt - 1)
        last = t == tt - 1
        y = jnp.zeros(sa.shape, F32)
        sa_n = jnp.zeros(sa.shape, F32)
        for k in range(dk):
            new = (s_ref[k] * w_ref[t, k:k + 1, :] + sa * b_ref[t, k:k + 1, :]
                   + v_t * k_ref[t, k:k + 1, :])
            s_ref[k] = new
            y = y + new * r_ref[t, k:k + 1, :]
            a_next = jnp.where(last, an_ref[0, k:k + 1, :], a_ref[tn, k:k + 1, :])
            sa_n = sa_n + new * a_next
        if norm == "layer":
            mu = jnp.mean(y, axis=0, keepdims=True)
            d = y - mu
            y = d * lax.rsqrt(jnp.mean(d * d, axis=0, keepdims=True) + RW_LNX_EPS)
        else:
            y = y * lax.rsqrt(jnp.mean(y * y, axis=0, keepdims=True) + NORM_EPS)
        y_ref[t] = y
        return sa_n

    sa_ref[...] = lax.fori_loop(0, tt, step, sa_ref[...])


def scan_call(r, w, k, v, a, b, s0, norm, tt=32):
    G, T, dk, L = r.shape
    dv = v.shape[2]
    tt = min(tt, T)
    nt = T // tt
    tspec = lambda d: pl.BlockSpec((None, tt, d, L), lambda g, i: (g, i, 0, 0))
    nxt = pl.BlockSpec((None, 1, dk, L), lambda g, i: (g, jnp.minimum((i + 1) * tt, T - 1), 0, 0))
    sspec = pl.BlockSpec((None, dk, dv, L), lambda g, i: (g, 0, 0, 0))
    return pl.pallas_call(
        functools.partial(_scan_body, tt=tt, dk=dk, norm=norm),
        grid=(G, nt),
        in_specs=[tspec(dk), tspec(dk), tspec(dk), tspec(dv), tspec(dk), tspec(dk), nxt, sspec],
        out_specs=[tspec(dv), sspec],
        out_shape=[jax.ShapeDtypeStruct((G, T, dv, L), F32),
                   jax.ShapeDtypeStruct((G, dk, dv, L), F32)],
        scratch_shapes=[pltpu.VMEM((dv, L), F32)],
        compiler_params=_cparams(("arbitrary", "arbitrary")),
        name="scan",
    )(r, w, k, v, a, b, a, s0)


def _to_scan(x, lanes_bh):
    B, T, H, C = x.shape
    if lanes_bh:
        return x.transpose(1, 3, 0, 2).reshape(1, T, C, B * H)
    return x.transpose(2, 1, 3, 0)


def _from_scan(y, B, H, lanes_bh):
    G, T, C, L = y.shape
    if lanes_bh:
        return y.reshape(T, C, B, H).transpose(2, 0, 3, 1)
    return y.transpose(3, 1, 0, 2)


def _state_to_scan(s, lanes_bh):
    B, H, dk, dv = s.shape
    if lanes_bh:
        return s.transpose(2, 3, 0, 1).reshape(1, dk, dv, B * H)
    return s.transpose(1, 2, 3, 0)


def _state_from_scan(s, B, H, lanes_bh):
    G, dk, dv, L = s.shape
    if lanes_bh:
        return s.reshape(dk, dv, B, H).transpose(2, 3, 0, 1)
    return s.transpose(3, 0, 1, 2)


NEG = -1e30


def _online_softmax_update(s, v, m_ref, l_ref, acc_ref):
    m_prev = m_ref[...]
    m_new = jnp.maximum(m_prev, jnp.max(s, axis=-1, keepdims=True))
    alpha = jnp.exp(m_prev - m_new)
    p = jnp.exp(s - m_new)
    l_ref[...] = alpha * l_ref[...] + jnp.sum(p, axis=-1, keepdims=True)
    acc_ref[...] = alpha * acc_ref[...] + jnp.dot(p.astype(BF16), v, preferred_element_type=F32)
    m_ref[...] = m_new


_NT = (((1,), (1,)), ((), ()))


def _mla_prompt_body(qi_ref, kj_ref, q_ref, k_ref, vt_ref, o_ref, m_ref, l_ref, acc_ref, *, tq, tk, scale):
    p = pl.program_id(1)
    qi, kj = qi_ref[p], kj_ref[p]

    @pl.when(kj == 0)
    def _():
        m_ref[...] = jnp.full(m_ref.shape, NEG, F32)
        l_ref[...] = jnp.zeros(l_ref.shape, F32)
        acc_ref[...] = jnp.zeros(acc_ref.shape, F32)

    st = lax.dot_general(k_ref[...], q_ref[...], _NT, preferred_element_type=F32) * scale
    kpos = kj * tk + lax.broadcasted_iota(jnp.int32, st.shape, 0)
    tok = qi * tq + lax.broadcasted_iota(jnp.int32, st.shape, 1) // MLA_HEADS
    st = jnp.where(kpos <= tok, st, NEG)
    m_prev = m_ref[...]
    m_new = jnp.maximum(m_prev, jnp.max(st, axis=0, keepdims=True))
    alpha = jnp.exp(m_prev - m_new)
    pt = jnp.exp(st - m_new)
    l_ref[...] = alpha * l_ref[...] + jnp.sum(pt, axis=0, keepdims=True)
    acc_ref[...] = alpha * acc_ref[...] + jnp.dot(vt_ref[...], pt.astype(BF16), preferred_element_type=F32)
    m_ref[...] = m_new

    @pl.when(kj == (qi * tq + tq - 1) // tk)
    def _():
        o_ref[...] = (acc_ref[...] / l_ref[...]).T.astype(o_ref.dtype)


def mla_prompt_call(qcat, kcat, tq=256, tk=256):
    B, TH, W = qcat.shape
    T = kcat.shape[1]
    vt = kcat[:, :, :MLA_KV_LORA].transpose(0, 2, 1)
    pairs = [(i, j) for i in range(T // tq) for j in range((i * tq + tq - 1) // tk + 1)]
    qi_tab = jnp.asarray([p[0] for p in pairs], jnp.int32)
    kj_tab = jnp.asarray([p[1] for p in pairs], jnp.int32)
    R = tq * MLA_HEADS
    return pl.pallas_call(
        functools.partial(_mla_prompt_body, tq=tq, tk=tk, scale=(MLA_NOPE + MLA_ROPE) ** -0.5),
        grid_spec=pltpu.PrefetchScalarGridSpec(
            num_scalar_prefetch=2, grid=(B, len(pairs)),
            in_specs=[pl.BlockSpec((None, R, W), lambda b, p, qi, kj: (b, qi[p], 0)),
                      pl.BlockSpec((None, tk, W), lambda b, p, qi, kj: (b, kj[p], 0)),
                      pl.BlockSpec((None, MLA_KV_LORA, tk), lambda b, p, qi, kj: (b, 0, kj[p]))],
            out_specs=pl.BlockSpec((None, R, MLA_KV_LORA), lambda b, p, qi, kj: (b, qi[p], 0)),
            scratch_shapes=[pltpu.VMEM((1, R), F32), pltpu.VMEM((1, R), F32),
                            pltpu.VMEM((MLA_KV_LORA, R), F32)]),
        out_shape=jax.ShapeDtypeStruct((B, TH, MLA_KV_LORA), BF16),
        compiler_params=_cparams(("arbitrary", "arbitrary")),
        name="mla_prompt",
    )(qi_tab, kj_tab, qcat, kcat, vt)


def _page_stream(pt_ref, streams, npc, n_pages):
    b, t = pl.program_id(0), pl.program_id(1)
    steps = pl.num_programs(1)
    n = b * steps + t
    slot = n % 2

    def each(bb, tt, sl, act):
        page0 = (tt * npc) % n_pages
        for pred, pool, buf, sem in streams:
            def go(pool=pool, buf=buf, sem=sem):
                for j in range(npc):
                    act(pltpu.make_async_copy(pool.at[pt_ref[bb, page0 + j]], buf.at[sl, j], sem.at[sl]))
            live = pred(tt)
            if live is True:
                go()
            else:
                pl.when(live)(go)

    @pl.when(n == 0)
    def _():
        each(b, t, slot, lambda cp: cp.start())

    @pl.when(n + 1 < pl.num_programs(0) * steps)
    def _():
        wrap = t + 1 == steps
        each(jnp.where(wrap, b + 1, b), jnp.where(wrap, 0, t + 1), 1 - slot, lambda cp: cp.start())

    each(b, t, slot, lambda cp: cp.wait())
    return slot


def _mla_sample_body(pt_ref, q_ref, kn_ref, poolc_ref, poolr_ref, o_ref,
                     cbuf, rbuf, semc, semr, m_ref, l_ref, acc_ref, *, npc, n_pages, scale):
    c = pl.program_id(1)
    nch = n_pages // npc
    always = lambda t: True
    slot = _page_stream(pt_ref, [(always, poolc_ref, cbuf, semc), (always, poolr_ref, rbuf, semr)],
                        npc, n_pages)

    @pl.when(c == 0)
    def _():
        m_ref[...] = jnp.full(m_ref.shape, NEG, F32)
        l_ref[...] = jnp.zeros(l_ref.shape, F32)
        acc_ref[...] = jnp.zeros(acc_ref.shape, F32)

    q = q_ref[...]
    q_lat, q_rope = q[:, :MLA_KV_LORA], q[:, MLA_KV_LORA:]
    ckv = cbuf[slot].reshape(npc * PAGE_SIZE, MLA_KV_LORA).astype(BF16)
    rt = jnp.concatenate([rbuf[slot, j] for j in range(npc)], axis=1)
    rt = jnp.concatenate([rt, jnp.zeros((LANES - MLA_ROPE, rt.shape[1]), F32)], axis=0).astype(BF16)
    s = lax.dot_general(q_lat, ckv, _NT, preferred_element_type=F32)
    s = (s + jnp.dot(q_rope, rt, preferred_element_type=F32)) * scale
    _online_softmax_update(s, ckv, m_ref, l_ref, acc_ref)

    @pl.when(c == nch - 1)
    def _():
        kn = kn_ref[...]
        s = lax.dot_general(q, kn, _NT, preferred_element_type=F32) * scale
        qs = lax.broadcasted_iota(jnp.int32, s.shape, 0) // MLA_HEADS
        col = lax.broadcasted_iota(jnp.int32, s.shape, 1)
        s = jnp.where(col <= qs, s, NEG)
        _online_softmax_update(s, kn[:, :MLA_KV_LORA], m_ref, l_ref, acc_ref)
        o_ref[...] = (acc_ref[...] / l_ref[...]).astype(o_ref.dtype)


def mla_sample_call(page_table, qcat, kcat_new, pool_c, pool_rt, npc=32):
    B, R, W = qcat.shape
    n_pages = page_table.shape[1]
    nch = n_pages // npc
    assert n_pages % npc == 0
    return pl.pallas_call(
        functools.partial(_mla_sample_body, npc=npc, n_pages=n_pages, scale=(MLA_NOPE + MLA_ROPE) ** -0.5),
        grid_spec=pltpu.PrefetchScalarGridSpec(
            num_scalar_prefetch=1, grid=(B, nch),
            in_specs=[pl.BlockSpec((None, R, W), lambda b, c, pt: (b, 0, 0)),
                      pl.BlockSpec((None,) + kcat_new.shape[1:], lambda b, c, pt: (b, 0, 0)),
                      pl.BlockSpec(memory_space=pl.ANY), pl.BlockSpec(memory_space=pl.ANY)],
            out_specs=pl.BlockSpec((None, R, MLA_KV_LORA), lambda b, c, pt: (b, 0, 0)),
            scratch_shapes=[pltpu.VMEM((2, npc, PAGE_SIZE, MLA_KV_LORA), F32),
                            pltpu.VMEM((2, npc, MLA_ROPE, PAGE_SIZE), F32),
                            pltpu.SemaphoreType.DMA((2,)), pltpu.SemaphoreType.DMA((2,)),
                            pltpu.VMEM((R, 1), F32), pltpu.VMEM((R, 1), F32),
                            pltpu.VMEM((R, MLA_KV_LORA), F32)]),
        out_shape=jax.ShapeDtypeStruct((B, R, MLA_KV_LORA), BF16),
        compiler_params=_cparams(("arbitrary", "arbitrary")),
        name="mla_sample",
    )(page_table, qcat, kcat_new, pool_c, pool_rt)


def _kmean_body(k_ref, o_ref):
    k = k_ref[...]
    nb = k.shape[0] // MB_BLOCK
    o_ref[...] = jnp.mean(k.reshape(nb, MB_BLOCK, k.shape[1]), axis=1)


def kmean_call(k):
    B, T, W = k.shape
    nb = T // MB_BLOCK
    return pl.pallas_call(
        _kmean_body, grid=(B,),
        in_specs=[pl.BlockSpec((None, T, W), lambda b: (b, 0, 0))],
        out_specs=pl.BlockSpec((None, nb, W), lambda b: (b, 0, 0)),
        out_shape=jax.ShapeDtypeStruct((B, nb, W), F32),
        compiler_params=_cparams(("arbitrary",)),
        name="kmean",
    )(k)


def _moba_prompt_body(qi_ref, kb_ref, last_ref, q_ref, k_ref, vt_ref, kmt_ref, o_ref,
                      sel_ref, m_ref, l_ref, acc_ref, *, nb):
    p = pl.program_id(1)
    qi, kb = qi_ref[p], kb_ref[p]
    own = kb == qi
    q = q_ref[...]

    @pl.when(own)
    def _():
        gate = lax.dot_general(kmt_ref[...], q, _NT, preferred_element_type=F32,
                               precision=lax.Precision.HIGHEST)
        row = lax.broadcasted_iota(jnp.int32, gate.shape, 0)
        j = row % nb
        valid = (j < qi) & (row < MB_HEADS * nb)
        g = jnp.where(valid, gate, -jnp.inf)
        rank = jnp.zeros(g.shape, jnp.int32)
        for d in range(1, nb):
            up = pltpu.roll(g, LANES - d, 0)
            rank += jnp.where((j + d < nb) & (up > g), 1, 0)
            dn = pltpu.roll(g, d, 0)
            rank += jnp.where((j - d >= 0) & (dn >= g), 1, 0)
        sel_ref[...] = jnp.where(valid & (rank < MB_TOPK), 1.0, 0.0)
        m_ref[...] = jnp.full(m_ref.shape, NEG, F32)
        l_ref[...] = jnp.zeros(l_ref.shape, F32)
        acc_ref[...] = jnp.zeros(acc_ref.shape, F32)

    kb16 = k_ref[...].astype(BF16)
    vt16 = vt_ref[...].astype(BF16)
    dist = ((qi - kb) * MB_BLOCK + lax.broadcasted_iota(jnp.int32, (MB_BLOCK, MB_BLOCK), 1)
            - lax.broadcasted_iota(jnp.int32, (MB_BLOCK, MB_BLOCK), 0))
    distf = dist.astype(F32)
    for h in range(MB_HEADS):
        g = h // MB_GROUP
        hs = slice(h * HEAD_DIM, (h + 1) * HEAD_DIM)
        gs = slice(g * HEAD_DIM, (g + 1) * HEAD_DIM)
        st = lax.dot_general(kb16[:, gs], q[:, hs].astype(BF16), _NT, preferred_element_type=F32)
        st = st * (HEAD_DIM ** -0.5) - (2.0 ** (-8.0 * (h + 1) / MB_HEADS)) * distf
        picked = sel_ref[pl.ds(h * nb + kb, 1), :] > 0.5
        st = jnp.where(own, jnp.where(dist >= 0, st, NEG), jnp.where(picked, st, NEG))
        m_prev = m_ref[h]
        m_new = jnp.maximum(m_prev, jnp.max(st, axis=0, keepdims=True))
        alpha = jnp.exp(m_prev - m_new)
        pt = jnp.exp(st - m_new)
        l_ref[h] = alpha * l_ref[h] + jnp.sum(pt, axis=0, keepdims=True)
        acc_ref[h] = alpha * acc_ref[h] + jnp.dot(vt16[gs, :], pt.astype(BF16), preferred_element_type=F32)
        m_ref[h] = m_new

    @pl.when(last_ref[p] == 1)
    def _():
        out_t = jnp.concatenate([acc_ref[h] / l_ref[h] for h in range(MB_HEADS)], axis=0)
        o_ref[...] = out_t.T


def moba_prompt_call(q, k, v, km):
    B, T, Wq = q.shape
    Wk = k.shape[2]
    nb = T // MB_BLOCK
    vt = v.transpose(0, 2, 1)
    kmt = km.transpose(0, 2, 1)
    steps = [(i, kb) for i in range(nb) for kb in [i] + list(range(i))]
    qi_tab = jnp.asarray([s[0] for s in steps], jnp.int32)
    kb_tab = jnp.asarray([s[1] for s in steps], jnp.int32)
    last_tab = jnp.asarray([1 if (i == 0 or kb == i - 1) else 0 for i, kb in steps], jnp.int32)
    return pl.pallas_call(
        functools.partial(_moba_prompt_body, nb=nb),
        grid_spec=pltpu.PrefetchScalarGridSpec(
            num_scalar_prefetch=3, grid=(B, len(steps)),
            in_specs=[pl.BlockSpec((None, MB_BLOCK, Wq), lambda b, p, qi, kb, la: (b, qi[p], 0)),
                      pl.BlockSpec((None, MB_BLOCK, Wk), lambda b, p, qi, kb, la: (b, kb[p], 0)),
                      pl.BlockSpec((None, Wk, MB_BLOCK), lambda b, p, qi, kb, la: (b, 0, kb[p])),
                      pl.BlockSpec((None, LANES, Wq), lambda b, p, qi, kb, la: (b, 0, 0))],
            out_specs=pl.BlockSpec((None, MB_BLOCK, Wq), lambda b, p, qi, kb, la: (b, qi[p], 0)),
            scratch_shapes=[pltpu.VMEM((LANES, MB_BLOCK), F32),
                            pltpu.VMEM((MB_HEADS, 1, MB_BLOCK), F32),
                            pltpu.VMEM((MB_HEADS, 1, MB_BLOCK), F32),
                            pltpu.VMEM((MB_HEADS, HEAD_DIM, MB_BLOCK), F32)]),
        out_shape=jax.ShapeDtypeStruct((B, T, Wq), F32),
        compiler_params=_cparams(("arbitrary", "arbitrary")),
        name="moba_prompt",
    )(qi_tab, kb_tab, last_tab, q, k, vt, kmt)


def _gate_matrix(kmean):
    B, nb, _ = kmean.shape
    kvh = np.arange(MB_HEADS) // MB_GROUP
    km = kmean.reshape(B, nb, MB_KV_HEADS, HEAD_DIM)[:, :, kvh, :].transpose(0, 2, 3, 1)
    eye = jnp.eye(MB_HEADS, dtype=F32)
    km = (km[:, :, :, None, :] * eye[None, :, None, :, None]).reshape(B, MB_HEADS * HEAD_DIM, MB_HEADS * nb)
    return jnp.pad(km, ((0, 0), (0, 0), (0, LANES - MB_HEADS * nb)))


def _moba_sample_body(pt_ref, qbd_ref, kn_ref, vn_ref, rc_ref, poolk_ref, poolv_ref, o_ref,
                      buf, sem, s_ref, p_ref, gate_ref, l_ref, acc_ref, *, npc, n_pages):
    t = pl.program_id(1)
    nch = n_pages // npc
    past = n_pages * PAGE_SIZE
    ph, c = t // nch, t % nch
    nb = past // MB_BLOCK
    ck = npc * PAGE_SIZE
    scale = HEAD_DIM ** -0.5
    R = qbd_ref.shape[0]
    slope = rc_ref[:, 0:1]
    qs = rc_ref[:, 1:2]
    slot = _page_stream(pt_ref, [(lambda tt: tt < nch, poolk_ref, buf, sem),
                                 (lambda tt: tt >= nch, poolv_ref, buf, sem)], npc, n_pages)

    def chunk_t():
        return jnp.concatenate([buf[slot, j] for j in range(npc)], axis=1).astype(BF16)

    glane = lax.broadcasted_iota(jnp.int32, gate_ref.shape, 1)

    @pl.when(ph == 0)
    def _():
        s = jnp.dot(qbd_ref[...].astype(BF16), chunk_t(), preferred_element_type=F32)
        s_ref[:, pl.ds(pl.multiple_of(c * ck, ck), ck)] = s
        g = jnp.where(c == 0, jnp.full(gate_ref.shape, -jnp.inf, F32), gate_ref[...])
        for i in range(ck // MB_BLOCK):
            bsum = jnp.sum(s[:, i * MB_BLOCK:(i + 1) * MB_BLOCK], axis=-1, keepdims=True)
            g = jnp.where(glane == c * (ck // MB_BLOCK) + i, bsum, g)
        gate_ref[...] = g

    @pl.when(t == nch - 1)
    def _():
        g = gate_ref[...]
        picks = []
        for _ in range(MB_TOPK):
            mx = jnp.max(g, axis=-1, keepdims=True)
            idx = jnp.min(jnp.where(g == mx, glane, LANES), axis=-1, keepdims=True)
            picks.append(idx)
            g = jnp.where(glane == idx, -jnp.inf, g)

        qb = qbd_ref[...].astype(BF16)
        kn = kn_ref[...].astype(BF16)
        s_own = lax.dot_general(qb, kn, _NT, preferred_element_type=F32) * scale
        col = lax.broadcasted_iota(jnp.int32, s_own.shape, 1).astype(F32)
        s_own = jnp.where(col <= qs, s_own - slope * (qs - col), NEG)
        m0 = jnp.max(s_own, axis=-1, keepdims=True)

        kcol = lax.broadcasted_iota(jnp.int32, (R, MB_BLOCK), 1).astype(F32)

        def block_scores(j):
            s = s_ref[:, pl.ds(pl.multiple_of(j * MB_BLOCK, MB_BLOCK), MB_BLOCK)]
            kpos = jnp.asarray(j * MB_BLOCK).astype(F32) + kcol
            s = s * scale - slope * ((past + qs) - kpos)
            picked = (picks[0] == j) | (picks[1] == j) | (picks[2] == j)
            return jnp.where(picked, s, NEG)

        m_acc = lax.fori_loop(0, nb, lambda j, a: jnp.maximum(a, block_scores(j)),
                              jnp.full((R, MB_BLOCK), NEG, F32))
        m = jnp.maximum(m0, jnp.max(m_acc, axis=-1, keepdims=True))

        def fill(j, a):
            p = jnp.exp(block_scores(j) - m)
            p_ref[:, pl.ds(pl.multiple_of(j * MB_BLOCK, MB_BLOCK), MB_BLOCK)] = p.astype(BF16)
            return a + p

        l_acc = lax.fori_loop(0, nb, fill, jnp.zeros((R, MB_BLOCK), F32))
        p_own = jnp.exp(s_own - m)
        l_ref[...] = jnp.sum(p_own, axis=-1, keepdims=True) + jnp.sum(l_acc, axis=-1, keepdims=True)
        acc_ref[...] = jnp.dot(p_own.astype(BF16), vn_ref[...].astype(BF16), preferred_element_type=F32)

    @pl.when(ph == 1)
    def _():
        pb = p_ref[:, pl.ds(pl.multiple_of(c * ck, ck), ck)]
        acc_ref[...] += lax.dot_general(pb, chunk_t(), _NT, preferred_element_type=F32)

    @pl.when(t == 2 * nch - 1)
    def _():
        out = acc_ref[...] / l_ref[...]
        S = R // MB_HEADS
        o_ref[...] = jnp.concatenate(
            [out[h * S:(h + 1) * S, (h // MB_GROUP) * HEAD_DIM:(h // MB_GROUP + 1) * HEAD_DIM]
             for h in range(MB_HEADS)], axis=-1)


def moba_sample_call(page_table, q, k_new, v_new, pool_kt, pool_vt, npc=32):
    B, S, Wq = q.shape
    Wk = k_new.shape[2]
    n_pages = page_table.shape[1]
    past = n_pages * PAGE_SIZE
    nch = n_pages // npc
    assert past % MB_BLOCK == 0 and past // MB_BLOCK >= MB_TOPK and n_pages % npc == 0
    R = MB_HEADS * S
    kvh = np.arange(MB_HEADS) // MB_GROUP
    onehot = jnp.asarray(np.eye(MB_KV_HEADS, dtype=np.float32)[kvh])
    qh = q.reshape(B, S, MB_HEADS, HEAD_DIM).transpose(0, 2, 1, 3)
    qbd = (qh[:, :, :, None, :] * onehot[None, :, None, :, None]).reshape(B, R, Wk)
    pad = lambda t: jnp.pad(t, ((0, 0), (0, LANES - S), (0, 0)))
    rc = np.zeros((R, LANES), np.float32)
    rc[:, 0] = np.repeat([2.0 ** (-8.0 * (h + 1) / MB_HEADS) for h in range(MB_HEADS)], S)
    rc[:, 1] = np.tile(np.arange(S), MB_HEADS)
    full = lambda shp: pl.BlockSpec((None,) + shp, lambda b, t, pt: (b, 0, 0))
    return pl.pallas_call(
        functools.partial(_moba_sample_body, npc=npc, n_pages=n_pages),
        grid_spec=pltpu.PrefetchScalarGridSpec(
            num_scalar_prefetch=1, grid=(B, 2 * nch),
            in_specs=[full((R, Wk)), full((LANES, Wk)), full((LANES, Wk)),
                      pl.BlockSpec((R, LANES), lambda b, t, pt: (0, 0)),
                      pl.BlockSpec(memory_space=pl.ANY), pl.BlockSpec(memory_space=pl.ANY)],
            out_specs=full((S, Wq)),
            scratch_shapes=[pltpu.VMEM((2, npc, Wk, PAGE_SIZE), F32), pltpu.SemaphoreType.DMA((2,)),
                            pltpu.VMEM((R, past), F32), pltpu.VMEM((R, past), BF16),
                            pltpu.VMEM((R, LANES), F32),
                            pltpu.VMEM((R, 1), F32), pltpu.VMEM((R, Wk), F32)]),
        out_shape=jax.ShapeDtypeStruct((B, S, Wq), F32),
        compiler_params=_cparams(("arbitrary", "arbitrary")),
        name="moba_sample",
    )(page_table, qbd, pad(k_new), pad(v_new), jnp.asarray(rc), pool_kt, pool_vt)


def _l2norm(x):
    return x * lax.rsqrt(jnp.sum(x * x, axis=-1, keepdims=True) + 1e-12)


def _rope_tables(pos):
    half = MLA_ROPE // 2
    inv = ROPE_THETA ** (-jnp.arange(half, dtype=F32) / half)
    ang = pos.astype(F32)[:, None] * inv[None, :]
    return jnp.cos(ang), jnp.sin(ang)


def _rwkv_mix(p, prev, S0, prm, lanes_bh):
    mu, w0, w2, a0, a2, g2, k_k, k_a, r_k, lnx_g, lnx_b = prm
    B, T, _ = p.shape
    H = RW_HEADS
    p_prev = jnp.concatenate([prev[:, None, :], p[:, :-1]], axis=1)
    pm = p + (p_prev - p) * mu
    cuts = [int(c) for c in np.cumsum(RW_SPLITS)[:-1]]
    r, wl, k, v, al, gl = jnp.split(pm, cuts, axis=-1)
    w_log = -jax.nn.softplus(-(w0 + jnp.tanh(wl) @ w2)) - 0.5
    decay = jnp.exp(-jnp.exp(w_log))
    a = jax.nn.sigmoid(a0 + al @ a2)
    g = jax.nn.sigmoid(gl) @ g2
    heads = lambda t: t.reshape(B, T, H, HEAD_DIM)
    kk = _l2norm(heads(k * k_k))
    k = k * (1.0 + (a - 1.0) * k_a)
    r_h, k_h, v_h, w_h, a_h = heads(r), heads(k), heads(v), heads(decay), heads(a)
    aa, bb = -kk, kk * a_h
    ts = lambda t: _to_scan(t, lanes_bh)
    s0 = _state_to_scan(jnp.swapaxes(S0, -1, -2), lanes_bh)
    y, S = scan_call(ts(r_h), ts(w_h), ts(k_h), ts(v_h), ts(aa), ts(bb), s0, "layer")
    y = _from_scan(y, B, H, lanes_bh).reshape(B, T, RW_WIDTH) * lnx_g + lnx_b
    S = jnp.swapaxes(_state_from_scan(S, B, H, lanes_bh), -1, -2)
    bonus = jnp.sum(r_h * k_h * r_k, axis=-1, keepdims=True) * v_h
    y = (y + bonus.reshape(B, T, RW_WIDTH)) * g
    return y, S, p[:, -1]


def _gdn_mix(qkv, z, a_raw, b_raw, conv_buf, S0, prm, lanes_bh):
    conv_w, a_log, dt_bias, norm_g = prm
    B, T, _ = qkv.shape
    H = GD_HEADS
    xx = jnp.concatenate([conv_buf, qkv], axis=1)
    y = sum(xx[:, j:j + T] * conv_w[j] for j in range(GD_CONV))
    y = _silu(y)
    q, k, v = jnp.split(y, [H * GD_DK, 2 * H * GD_DK], axis=-1)
    q = _l2norm(q.reshape(B, T, H, GD_DK)) * (GD_DK ** -0.5)
    k = _l2norm(k.reshape(B, T, H, GD_DK))
    v = v.reshape(B, T, H, GD_DV)
    beta = jax.nn.sigmoid(b_raw)[..., None]
    alpha = jnp.exp(-jnp.exp(a_log) * jax.nn.softplus(a_raw + dt_bias))[..., None]
    ts = lambda t: _to_scan(t, lanes_bh)
    w = jnp.broadcast_to(alpha, k.shape)
    y, S = scan_call(ts(q), ts(w), ts(k), ts(beta * v), ts(k), ts(-beta * alpha * k),
                     _state_to_scan(S0, lanes_bh), "rms")
    o = _from_scan(y, B, H, lanes_bh) * norm_g * _silu(z.reshape(B, T, H, GD_DV))
    S = _state_from_scan(S, B, H, lanes_bh)
    return o.reshape(B, T, H * GD_DV), S, xx[:, -(GD_CONV - 1):]


def _bdiag(w):
    H, a, b = w.shape
    eye = jnp.eye(H, dtype=w.dtype)
    return (eye[:, None, :, None] * w[:, :, None, :]).reshape(H * a, H * b)


def kernel(x_prompt, x_sample, cache_moba_k, cache_moba_v, cache_mla_ckv, cache_mla_krope, state_rwkv_wkv, state_rwkv_shift, state_gdn, state_gdn_conv, page_table, c_prompt, c_sample, w_ada, b_ada, g_mix_pre, g_mix_post, g_ff_pre, g_ff_post, w_in_even, w_out_even, rw_mu, rw_w0, rw_w2, rw_a0, rw_a2, rw_g2, rw_kk, rw_ka, rw_rk, rw_lnx_g, rw_lnx_b, ffn_w_gate, ffn_w_up, ffn_w_down, w_in_odd, w_out_odd, mla_q_norm, mla_w_q_up, mla_kv_norm, mla_w_uk, mla_w_uv, gdn_conv_w, gdn_a_log, gdn_dt_bias, gdn_norm_g, moe_w_router, moe_b_router, moe_w_gate, moe_w_up, moe_w_down):
    Bp, Tp, D = x_prompt.shape
    Bs, Ts, _ = x_sample.shape
    past = page_table.shape[1] * PAGE_SIZE
    Ms = Bs * Ts
    row = lambda v: v.reshape(1, -1)

    xp = x_prompt
    xs = x_sample.reshape(1, Ms, D)
    c_all = jnp.concatenate([c_prompt, c_sample], axis=0)

    outs_p, outs_s = {}, {}
    for layer in range(w_ada.shape[0]):
        i = layer // 2
        mod = adaln_call(c_all, w_ada[layer].astype(BF16), row(b_ada[layer]))
        mp = [m.reshape(Bp, 1, D) for m in jnp.split(mod[:Bp], 6, axis=-1)]
        ms = [jnp.broadcast_to(m[:, None, :], (Bs, Ts, D)).reshape(1, Ms, D)
              for m in jnp.split(mod[Bp:], 6, axis=-1)]
        gpre, gpost = row(g_mix_pre[layer]), row(g_mix_post[layer])
        if layer % 2 == 0:
            w_in = w_in_even[i].astype(BF16)
            splits = (RW_PROJ, MB_HEADS * HEAD_DIM, MB_KV_HEADS * HEAD_DIM, MB_KV_HEADS * HEAD_DIM)
            rw_prm = (rw_mu[i], rw_w0[i], rw_w2[i], rw_a0[i], rw_a2[i], rw_g2[i], rw_kk[i], rw_ka[i],
                      rw_rk[i], rw_lnx_g[i], rw_lnx_b[i])
            w_out = w_out_even[i].astype(BF16)
            p_rw, q, k, v = modmm_call(xp, mp[0], mp[1], gpre, w_in, splits)
            y_rw, S_p, shift_p = _rwkv_mix(p_rw, jnp.zeros((Bp, RW_PROJ), F32),
                                           jnp.zeros((Bp, RW_HEADS, HEAD_DIM, HEAD_DIM), F32), rw_prm, True)
            k4 = k.reshape(Bp, Tp, MB_KV_HEADS, HEAD_DIM)
            v4 = v.reshape(Bp, Tp, MB_KV_HEADS, HEAD_DIM)
            y_mb = moba_prompt_call(q, k, v, _gate_matrix(kmean_call(k)))
            xp = outproj_call(y_rw, y_mb, w_out[:RW_WIDTH], w_out[RW_WIDTH:], xp, mp[2], gpost)
            outs_p.update(mk=k4[None], mv=v4[None], wkv=S_p[None], shift=shift_p[None])
            p_rw, q, k, v = modmm_call(xs, ms[0], ms[1], gpre, w_in, splits)
            y_rw, S_s, shift_s = _rwkv_mix(p_rw.reshape(Bs, Ts, RW_PROJ), state_rwkv_shift[i],
                                           state_rwkv_wkv[i], rw_prm, False)
            k4 = k.reshape(Bs, Ts, MB_KV_HEADS, HEAD_DIM)
            v4 = v.reshape(Bs, Ts, MB_KV_HEADS, HEAD_DIM)
            n_pool = cache_moba_k.shape[1]
            pool_t = lambda c: c[i].transpose(0, 2, 3, 1).reshape(n_pool, -1, PAGE_SIZE)
            y_mb = moba_sample_call(page_table, q.reshape(Bs, Ts, -1), k.reshape(Bs, Ts, -1), v.reshape(Bs, Ts, -1),
                                    pool_t(cache_moba_k), pool_t(cache_moba_v))
            xs = outproj_call(y_rw.reshape(1, Ms, RW_WIDTH), y_mb.reshape(1, Ms, -1),
                              w_out[:RW_WIDTH], w_out[RW_WIDTH:], xs, ms[2], gpost)
            outs_s.update(mk=k4[None], mv=v4[None], wkv=S_s[None], shift=shift_s[None])
        else:
            w_in = w_in_odd[i]
            cuts = [int(c) for c in np.cumsum(ODD_SPLITS)[:-1]]
            w_cq, w_ckv, w_kr, w_qkv, w_z, w_a, w_b = jnp.split(w_in, cuts, axis=1)
            padc = lambda w, n: jnp.pad(w, ((0, 0), (0, n - w.shape[1])))
            w_in_p = jnp.concatenate([w_cq, w_ckv, padc(w_kr, LANES), w_qkv, w_z,
                                      padc(jnp.concatenate([w_a, w_b], axis=1), LANES)], axis=1).astype(BF16)
            splits = (MLA_Q_LORA, MLA_KV_LORA, LANES, GD_QKV, GD_HEADS * GD_DV, LANES)
            w_out = w_out_odd[i].astype(BF16)
            gd_prm = (gdn_conv_w[i], gdn_a_log[i], gdn_dt_bias[i], gdn_norm_g[i])
            w_uv_bd = _bdiag(mla_w_uv[i].transpose(1, 0, 2)).astype(BF16)
            w_uk_bd = _bdiag(mla_w_uk[i].transpose(1, 2, 0)).astype(BF16)
            wq = mla_w_q_up[i].reshape(MLA_Q_LORA, MLA_HEADS, MLA_NOPE + MLA_ROPE)
            w_qn = wq[:, :, :MLA_NOPE].reshape(MLA_Q_LORA, -1).astype(BF16)
            w_qr = jnp.pad(wq[:, :, MLA_NOPE:], ((0, 0), (0, 0), (0, LANES - MLA_ROPE)))
            w_qr = w_qr.reshape(MLA_Q_LORA, -1).astype(BF16)
            w_oa, w_ob = w_out[:MLA_HEADS * MLA_V], w_out[MLA_HEADS * MLA_V:]

            def odd(x, m, B, T, cosf, sinf, conv_buf, S0, lanes_bh):
                cq, ckv, kr, qkv, z, ab = modmm_call(x, m[0], m[1], gpre, w_in_p, splits)
                qcat, kcat, ckv_n, k_rope = mla_prep_call(
                    cq, ckv, kr, cosf, sinf, row(mla_q_norm[i]), row(mla_kv_norm[i]), w_qn, w_qr, w_uk_bd)
                qkv, z = qkv.reshape(B, T, -1), z.reshape(B, T, -1)
                ab = ab.reshape(B, T, LANES)
                a_raw, b_raw = ab[..., :GD_HEADS], ab[..., GD_HEADS:2 * GD_HEADS]
                y_gdn, S, buf = _gdn_mix(qkv, z, a_raw, b_raw, conv_buf, S0, gd_prm, lanes_bh)
                ckv_n = ckv_n.reshape(B, T, MLA_KV_LORA)
                k_rope = k_rope.reshape(B, T, LANES)[..., :MLA_ROPE]
                return qcat, kcat, y_gdn, ckv_n, k_rope, S, buf

            cosf, sinf = _rope_rows(jnp.arange(Tp))
            qcat, kcat, y_gdn, ckv_n, k_rope, S_p, buf_p = odd(
                xp, mp, Bp, Tp, cosf, sinf, jnp.zeros((Bp, GD_CONV - 1, GD_QKV), F32),
                jnp.zeros((Bp, GD_HEADS, GD_DK, GD_DV), F32), True)
            o_lat = mla_prompt_call(qcat.reshape(Bp, Tp * MLA_HEADS, 2 * LANES), kcat)
            xp = outproj_call(o_lat.reshape(Bp, Tp, -1), y_gdn, w_oa, w_ob, xp, mp[2], gpost, wpre=w_uv_bd)
            outs_p.update(ckv=ckv_n[None], krope=k_rope[None], gdn=S_p[None], conv=buf_p[None])
            cosf, sinf = _rope_rows(jnp.tile(past + jnp.arange(Ts), Bs))
            qcat, kcat, y_gdn, ckv_n, k_rope, S_s, buf_s = odd(
                xs, ms, Bs, Ts, cosf, sinf, state_gdn_conv[i], state_gdn[i], False)
            kn = jnp.pad(kcat.reshape(Bs, Ts, 2 * LANES), ((0, 0), (0, LANES - Ts), (0, 0)))
            o_lat = mla_sample_call(page_table, qcat.reshape(Bs, Ts * MLA_HEADS, 2 * LANES), kn,
                                    cache_mla_ckv[i], cache_mla_krope[i].transpose(0, 2, 1))
            xs = outproj_call(o_lat.reshape(1, Ms, -1), y_gdn.reshape(1, Ms, -1), w_oa, w_ob, xs, ms[2], gpost,
                              wpre=w_uv_bd)
            outs_s.update(ckv=ckv_n[None], krope=k_rope[None], gdn=S_s[None], conv=buf_s[None])

        gpre, gpost = row(g_ff_pre[layer]), row(g_ff_post[layer])
        if layer % 2 == 0:
            J = 2
            Fd = ffn_w_gate.shape[2] // J
            wg = ffn_w_gate[i].astype(BF16).reshape(D, J, Fd).transpose(1, 0, 2)
            wu = ffn_w_up[i].astype(BF16).reshape(D, J, Fd).transpose(1, 0, 2)
            wd = ffn_w_down[i].astype(BF16).reshape(J, Fd, D)
            xp = ffn_call(xp, mp[3], mp[4], mp[5], gpre, gpost, wg, wu, wd)
            xs = ffn_call(xs, ms[3], ms[4], ms[5], gpre, gpost, wg, wu, wd)
        else:
            wr = jnp.pad(moe_w_router[i], ((0, 0), (0, LANES - N_EXPERTS)))
            br = jnp.pad(row(moe_b_router[i]), ((0, 0), (0, LANES - N_EXPERTS)))
            wg, wu, wd = (w[i].astype(BF16) for w in (moe_w_gate, moe_w_up, moe_w_down))
            xp = ffn_call(xp, mp[3], mp[4], mp[5], gpre, gpost, wg, wu, wd, wr, br)
            xs = ffn_call(xs, ms[3], ms[4], ms[5], gpre, gpost, wg, wu, wd, wr, br)

    names = ('mk', 'mv', 'wkv', 'shift', 'ckv', 'krope', 'gdn', 'conv')
    return ((xp, xs.reshape(Bs, Ts, D)) + tuple(outs_p[n] for n in names)
            + tuple(outs_s[n] for n in names))
```

```python
import functools
import math

import jax
import jax.numpy as jnp
import numpy as np
from jax import lax
from jax.experimental import pallas as pl
from jax.experimental.pallas import tpu as pltpu

F32 = jnp.float32
BF16 = jnp.bfloat16

NORM_EPS = 1e-6
HEAD_DIM = 64
PAGE_SIZE = 128
RW_HEADS = 8
RW_WIDTH = RW_HEADS * HEAD_DIM
RW_SPLITS = (RW_WIDTH, 64, RW_WIDTH, RW_WIDTH, 64, 128)
RW_PROJ = sum(RW_SPLITS)
RW_LNX_EPS = 64e-5
MB_HEADS = 8
MB_KV_HEADS = 4
MB_GROUP = MB_HEADS // MB_KV_HEADS
MB_BLOCK = 256
MB_TOPK = 3
MLA_HEADS = 8
MLA_NOPE = 64
MLA_ROPE = 32
MLA_V = 64
MLA_Q_LORA = 256
MLA_KV_LORA = 128
ROPE_THETA = 10000.0
GD_HEADS = 8
GD_DK = 64
GD_DV = 64
GD_CONV = 4
GD_QKV = GD_HEADS * (2 * GD_DK + GD_DV)
ODD_SPLITS = (MLA_Q_LORA, MLA_KV_LORA, MLA_ROPE, GD_QKV, GD_HEADS * GD_DV, GD_HEADS, GD_HEADS)
N_EXPERTS = 8

LANES = 128
VMEM_LIMIT = 56 * 1024 * 1024


def _cparams(sem):
    return pltpu.CompilerParams(dimension_semantics=sem, vmem_limit_bytes=VMEM_LIMIT)


def _silu(x):
    return x * jax.nn.sigmoid(x)


def _rms(x, eps=NORM_EPS):
    return x * lax.rsqrt(jnp.mean(x * x, axis=-1, keepdims=True) + eps)


def _adaln_body(c_ref, w_ref, b_ref, o_ref):
    c = c_ref[...]
    o_ref[...] = jnp.dot(_silu(c).astype(BF16), w_ref[...], preferred_element_type=F32) + b_ref[...]


def adaln_call(c, w, b, tn=1536):
    R, D = c.shape
    N = w.shape[1]
    return pl.pallas_call(
        _adaln_body,
        grid=(N // tn,),
        in_specs=[pl.BlockSpec((R, D), lambda j: (0, 0)),
                  pl.BlockSpec((D, tn), lambda j: (0, j)),
                  pl.BlockSpec((1, tn), lambda j: (0, j))],
        out_specs=pl.BlockSpec((R, tn), lambda j: (0, j)),
        out_shape=jax.ShapeDtypeStruct((R, N), F32),
        compiler_params=_cparams(("arbitrary",)),
        name="adaln",
    )(c, w, b)


def _mod_spec(mod, tm):
    if mod.shape[1] == 1:
        return pl.BlockSpec((None, 1, mod.shape[2]), lambda g, i, *_: (g, 0, 0))
    return pl.BlockSpec((None, tm, mod.shape[2]), lambda g, i, *_: (g, i, 0))


def _modmm_body(x_ref, sh_ref, sc_ref, g_ref, w_ref, *o_refs, splits):
    h = _rms(x_ref[...]) * g_ref[...] * (1.0 + sc_ref[...]) + sh_ref[...]
    y = jnp.dot(h.astype(BF16), w_ref[...], preferred_element_type=F32)
    off = 0
    for o_ref, n in zip(o_refs, splits):
        o_ref[...] = y[:, off:off + n].astype(o_ref.dtype)
        off += n


def modmm_call(x, shift, scale, gain, w, splits, tm=512):
    G, R, D = x.shape
    tm = min(tm, R)
    N = w.shape[1]
    assert sum(splits) == N and all(s % LANES == 0 for s in splits)
    return pl.pallas_call(
        functools.partial(_modmm_body, splits=splits),
        grid=(G, R // tm),
        in_specs=[pl.BlockSpec((None, tm, D), lambda g, i: (g, i, 0)),
                  _mod_spec(shift, tm), _mod_spec(scale, tm),
                  pl.BlockSpec((1, D), lambda g, i: (0, 0)),
                  pl.BlockSpec((D, N), lambda g, i: (0, 0))],
        out_specs=[pl.BlockSpec((None, tm, n), lambda g, i: (g, i, 0)) for n in splits],
        out_shape=[jax.ShapeDtypeStruct((G, R, n), F32) for n in splits],
        compiler_params=_cparams(("arbitrary", "arbitrary")),
        name="modmm",
    )(x, shift, scale, gain, w)


def _outproj_body(ya_ref, yb_ref, wpre_ref, wa_ref, wb_ref, x_ref, ga_ref, gp_ref, o_ref, *, pre):
    ya = ya_ref[...].astype(BF16)
    if pre:
        ya = jnp.dot(ya, wpre_ref[...], preferred_element_type=F32).astype(BF16)
    y = jnp.dot(ya, wa_ref[...], preferred_element_type=F32)
    y = y + jnp.dot(yb_ref[...].astype(BF16), wb_ref[...], preferred_element_type=F32)
    o_ref[...] = x_ref[...] + ga_ref[...] * (_rms(y) * gp_ref[...])


def outproj_call(ya, yb, wa, wb, x, gate, gpost, wpre=None, tm=512):
    G, R, D = x.shape
    tm = min(tm, R)
    Ka, Kb = ya.shape[2], yb.shape[2]
    pre = wpre is not None
    if not pre:
        wpre = jnp.zeros((8, LANES), BF16)
    return pl.pallas_call(
        functools.partial(_outproj_body, pre=pre),
        grid=(G, R // tm),
        in_specs=[pl.BlockSpec((None, tm, Ka), lambda g, i: (g, i, 0)),
                  pl.BlockSpec((None, tm, Kb), lambda g, i: (g, i, 0)),
                  pl.BlockSpec(wpre.shape, lambda g, i: (0, 0)),
                  pl.BlockSpec(wa.shape, lambda g, i: (0, 0)),
                  pl.BlockSpec(wb.shape, lambda g, i: (0, 0)),
                  pl.BlockSpec((None, tm, D), lambda g, i: (g, i, 0)),
                  _mod_spec(gate, tm),
                  pl.BlockSpec((1, D), lambda g, i: (0, 0))],
        out_specs=pl.BlockSpec((None, tm, D), lambda g, i: (g, i, 0)),
        out_shape=jax.ShapeDtypeStruct((G, R, D), F32),
        compiler_params=_cparams(("arbitrary", "arbitrary")),
        name="outproj",
    )(ya, yb, wpre, wa, wb, x, gate, gpost)


def _mla_prep_body(cq_ref, ckv_ref, kr_ref, cos_ref, sin_ref, qg_ref, kg_ref, wqn_ref, wqr_ref, wuk_ref,
                   qcat_ref, kcat_ref, ckvn_ref, krope_ref):
    cosf, sinf = cos_ref[...], sin_ref[...]
    lane = lax.broadcasted_iota(jnp.int32, cosf.shape, 1)
    half = MLA_ROPE // 2

    def rope(x):
        rot = jnp.where(lane < half, pltpu.roll(x, LANES - half, 1), pltpu.roll(x, half, 1))
        return x * cosf + rot * sinf

    cqn = (_rms(cq_ref[...]) * qg_ref[...]).astype(BF16)
    qn = jnp.dot(cqn, wqn_ref[...], preferred_element_type=F32).astype(BF16)
    qr = jnp.dot(cqn, wqr_ref[...], preferred_element_type=F32)
    ql = jnp.dot(qn, wuk_ref[...], preferred_element_type=F32)
    for h in range(MLA_HEADS):
        qcat_ref[:, 2 * h * LANES:(2 * h + 1) * LANES] = ql[:, h * LANES:(h + 1) * LANES].astype(BF16)
        qcat_ref[:, (2 * h + 1) * LANES:(2 * h + 2) * LANES] = rope(qr[:, h * LANES:(h + 1) * LANES]).astype(BF16)
    ckvn = _rms(ckv_ref[...]) * kg_ref[...]
    krope = rope(kr_ref[...])
    ckvn_ref[...] = ckvn
    krope_ref[...] = krope
    kcat_ref[:, :LANES] = ckvn.astype(BF16)
    kcat_ref[:, LANES:] = krope.astype(BF16)


def mla_prep_call(cq, ckv, kr, cosf, sinf, q_gain, kv_gain, w_qn, w_qr, w_uk_bd, tm=512):
    G, R, _ = cq.shape
    tm = min(tm, R)
    rows = lambda n: pl.BlockSpec((None, tm, n), lambda g, i: (g, i, 0))
    const = lambda a: pl.BlockSpec(a.shape, lambda g, i: (0, 0))
    W = 2 * LANES * MLA_HEADS
    return pl.pallas_call(
        _mla_prep_body,
        grid=(G, R // tm),
        in_specs=[rows(cq.shape[2]), rows(LANES), rows(LANES),
                  pl.BlockSpec((tm, LANES), lambda g, i: (i, 0)), pl.BlockSpec((tm, LANES), lambda g, i: (i, 0)),
                  const(q_gain), const(kv_gain), const(w_qn), const(w_qr), const(w_uk_bd)],
        out_specs=[rows(W), rows(2 * LANES), rows(LANES), rows(LANES)],
        out_shape=[jax.ShapeDtypeStruct((G, R, W), BF16), jax.ShapeDtypeStruct((G, R, 2 * LANES), BF16),
                   jax.ShapeDtypeStruct((G, R, LANES), F32), jax.ShapeDtypeStruct((G, R, LANES), F32)],
        compiler_params=_cparams(("arbitrary", "arbitrary")),
        name="mla_prep",
    )(cq, ckv, kr, cosf, sinf, q_gain, kv_gain, w_qn, w_qr, w_uk_bd)


def _rope_rows(pos):
    cos, sin = _rope_tables(pos)
    z = jnp.zeros((pos.shape[0], LANES - MLA_ROPE), F32)
    return jnp.concatenate([cos, cos, z], axis=1), jnp.concatenate([-sin, sin, z], axis=1)


def _ffn_body(x_ref, sh_ref, sc_ref, ga_ref, gpre_ref, gpost_ref, wr_ref, br_ref,
              wg_ref, wu_ref, wd_ref, o_ref, h_ref, acc_ref, rw_ref, *, moe):
    j = pl.program_id(2)

    @pl.when(j == 0)
    def _():
        h = _rms(x_ref[...]) * gpre_ref[...] * (1.0 + sc_ref[...]) + sh_ref[...]
        h_ref[...] = h.astype(BF16)
        acc_ref[...] = jnp.zeros_like(acc_ref)
        if moe:
            logits = jnp.dot(h, wr_ref[...], preferred_element_type=F32,
                             precision=lax.Precision.HIGHEST) + br_ref[...]
            lane = lax.broadcasted_iota(jnp.int32, logits.shape, 1)
            neg = jnp.float32(-jnp.inf)
            logits = jnp.where(lane < N_EXPERTS, logits, neg)
            m1 = jnp.max(logits, axis=-1, keepdims=True)
            i1 = jnp.min(jnp.where(logits == m1, lane, LANES), axis=-1, keepdims=True)
            rest = jnp.where(lane == i1, neg, logits)
            m2 = jnp.max(rest, axis=-1, keepdims=True)
            i2 = jnp.min(jnp.where(rest == m2, lane, LANES), axis=-1, keepdims=True)
            e2 = jnp.exp(m2 - m1)
            w1 = 1.0 / (1.0 + e2)
            w2 = e2 / (1.0 + e2)
            rw_ref[...] = jnp.where(lane == i1, w1, 0.0) + jnp.where(lane == i2, w2, 0.0)

    hb = h_ref[...]
    g = jnp.dot(hb, wg_ref[...], preferred_element_type=F32)
    u = jnp.dot(hb, wu_ref[...], preferred_element_type=F32)
    f = jnp.dot((_silu(g) * u).astype(BF16), wd_ref[...], preferred_element_type=F32)
    if moe:
        lane = lax.broadcasted_iota(jnp.int32, rw_ref.shape, 1)
        we = jnp.sum(jnp.where(lane == j, rw_ref[...], 0.0), axis=-1, keepdims=True)
        f = f * we
    acc_ref[...] += f

    @pl.when(j == pl.num_programs(2) - 1)
    def _():
        o_ref[...] = x_ref[...] + ga_ref[...] * (_rms(acc_ref[...]) * gpost_ref[...])


def ffn_call(x, shift, scale, gate, gpre, gpost, wg, wu, wd, w_router=None, b_router=None, tm=512):
    G, R, D = x.shape
    tm = min(tm, R)
    J, _, Fd = wg.shape
    moe = w_router is not None
    if not moe:
        w_router = jnp.zeros((D, LANES), F32)
        b_router = jnp.zeros((1, LANES), F32)
    return pl.pallas_call(
        functools.partial(_ffn_body, moe=moe),
        grid=(G, R // tm, J),
        in_specs=[pl.BlockSpec((None, tm, D), lambda g, i, j: (g, i, 0)),
                  _mod_spec(shift, tm), _mod_spec(scale, tm), _mod_spec(gate, tm),
                  pl.BlockSpec((1, D), lambda g, i, j: (0, 0)),
                  pl.BlockSpec((1, D), lambda g, i, j: (0, 0)),
                  pl.BlockSpec((D, LANES), lambda g, i, j: (0, 0)),
                  pl.BlockSpec((1, LANES), lambda g, i, j: (0, 0)),
                  pl.BlockSpec((None, D, Fd), lambda g, i, j: (j, 0, 0)),
                  pl.BlockSpec((None, D, Fd), lambda g, i, j: (j, 0, 0)),
                  pl.BlockSpec((None, Fd, D), lambda g, i, j: (j, 0, 0))],
        out_specs=pl.BlockSpec((None, tm, D), lambda g, i, j: (g, i, 0)),
        out_shape=jax.ShapeDtypeStruct((G, R, D), F32),
        scratch_shapes=[pltpu.VMEM((tm, D), BF16), pltpu.VMEM((tm, D), F32),
                        pltpu.VMEM((tm, LANES), F32)],
        compiler_params=_cparams(("arbitrary", "arbitrary", "arbitrary")),
        name="moe" if moe else "ffn",
    )(x, shift, scale, gate, gpre, gpost, w_router, b_router, wg, wu, wd)


def _scan_body(r_ref, w_ref, k_ref, v_ref, a_ref, b_ref, an_ref, s0_ref, y_ref, s_ref, sa_ref,
               *, tt, dk, norm):
    i = pl.program_id(1)

    @pl.when(i == 0)
    def _():
        s_ref[...] = s0_ref[...]
        sa = jnp.zeros(sa_ref.shape, F32)
        for k in range(dk):
            sa = sa + s0_ref[k] * a_ref[0, k:k + 1, :]
        sa_ref[...] = sa

    def step(t, sa):
        v_t = v_ref[t]
        tn = jnp.minimum(t + 1, tt - 1)
        last = t == tt - 1
        y = jnp.zeros(sa.shape, F32)
        sa_n = jnp.zeros(sa.shape, F32)
        for k in range(dk):
            new = (s_ref[k] * w_ref[t, k:k + 1, :] + sa * b_ref[t, k:k + 1, :]
                   + v_t * k_ref[t, k:k + 1, :])
            s_ref[k] = new
            y = y + new * r_ref[t, k:k + 1, :]
            a_next = jnp.where(last, an_ref[0, k:k + 1, :], a_ref[tn, k:k + 1, :])
            sa_n = sa_n + new * a_next
        if norm == "layer":
            mu = jnp.mean(y, axis=0, keepdims=True)
            d = y - mu
            y = d * lax.rsqrt(jnp.mean(d * d, axis=0, keepdims=True) + RW_LNX_EPS)
        else:
            y = y * lax.rsqrt(jnp.mean(y * y, axis=0, keepdims=True) + NORM_EPS)
        y_ref[t] = y
        return sa_n

    sa_ref[...] = lax.fori_loop(0, tt, step, sa_ref[...])


def scan_call(r, w, k, v, a, b, s0, norm, tt=32):
    G, T, dk, L = r.shape
    dv = v.shape[2]
    tt = min(tt, T)
    nt = T // tt
    tspec = lambda d: pl.BlockSpec((None, tt, d, L), lambda g, i: (g, i, 0, 0))
    nxt = pl.BlockSpec((None, 1, dk, L), lambda g, i: (g, jnp.minimum((i + 1) * tt, T - 1), 0, 0))
    sspec = pl.BlockSpec((None, dk, dv, L), lambda g, i: (g, 0, 0, 0))
    return pl.pallas_call(
        functools.partial(_scan_body, tt=tt, dk=dk, norm=norm),
        grid=(G, nt),
        in_specs=[tspec(dk), tspec(dk), tspec(dk), tspec(dv), tspec(dk), tspec(dk), nxt, sspec],
        out_specs=[tspec(dv), sspec],
        out_shape=[jax.ShapeDtypeStruct((G, T, dv, L), F32),
                   jax.ShapeDtypeStruct((G, dk, dv, L), F32)],
        scratch_shapes=[pltpu.VMEM((dv, L), F32)],
        compiler_params=_cparams(("arbitrary", "arbitrary")),
        name="scan",
    )(r, w, k, v, a, b, a, s0)


def _to_scan(x, lanes_bh):
    B, T, H, C = x.shape
    if lanes_bh:
        return x.transpose(1, 3, 0, 2).reshape(1, T, C, B * H)
    return x.transpose(2, 1, 3, 0)


def _from_scan(y, B, H, lanes_bh):
    G, T, C, L = y.shape
    if lanes_bh:
        return y.reshape(T, C, B, H).transpose(2, 0, 3, 1)
    return y.transpose(3, 1, 0, 2)


def _state_to_scan(s, lanes_bh):
    B, H, dk, dv = s.shape
    if lanes_bh:
        return s.transpose(2, 3, 0, 1).reshape(1, dk, dv, B * H)
    return s.transpose(1, 2, 3, 0)


def _state_from_scan(s, B, H, lanes_bh):
    G, dk, dv, L = s.shape
    if lanes_bh:
        return s.reshape(dk, dv, B, H).transpose(2, 3, 0, 1)
    return s.transpose(3, 0, 1, 2)


NEG = -1e30


def _online_softmax_update(s, v, m_ref, l_ref, acc_ref):
    m_prev = m_ref[...]
    m_new = jnp.maximum(m_prev, jnp.max(s, axis=-1, keepdims=True))
    alpha = jnp.exp(m_prev - m_new)
    p = jnp.exp(s - m_new)
    l_ref[...] = alpha * l_ref[...] + jnp.sum(p, axis=-1, keepdims=True)
    acc_ref[...] = alpha * acc_ref[...] + jnp.dot(p.astype(BF16), v, preferred_element_type=F32)
    m_ref[...] = m_new


_NT = (((1,), (1,)), ((), ()))


def _mla_prompt_body(qi_ref, kj_ref, q_ref, k_ref, vt_ref, o_ref, m_ref, l_ref, acc_ref, *, tq, tk, scale):
    p = pl.program_id(1)
    qi, kj = qi_ref[p], kj_ref[p]

    @pl.when(kj == 0)
    def _():
        m_ref[...] = jnp.full(m_ref.shape, NEG, F32)
        l_ref[...] = jnp.zeros(l_ref.shape, F32)
        acc_ref[...] = jnp.zeros(acc_ref.shape, F32)

    st = lax.dot_general(k_ref[...], q_ref[...], _NT, preferred_element_type=F32) * scale
    kpos = kj * tk + lax.broadcasted_iota(jnp.int32, st.shape, 0)
    tok = qi * tq + lax.broadcasted_iota(jnp.int32, st.shape, 1) // MLA_HEADS
    st = jnp.where(kpos <= tok, st, NEG)
    m_prev = m_ref[...]
    m_new = jnp.maximum(m_prev, jnp.max(st, axis=0, keepdims=True))
    alpha = jnp.exp(m_prev - m_new)
    pt = jnp.exp(st - m_new)
    l_ref[...] = alpha * l_ref[...] + jnp.sum(pt, axis=0, keepdims=True)
    acc_ref[...] = alpha * acc_ref[...] + jnp.dot(vt_ref[...], pt.astype(BF16), preferred_element_type=F32)
    m_ref[...] = m_new

    @pl.when(kj == (qi * tq + tq - 1) // tk)
    def _():
        o_ref[...] = (acc_ref[...] / l_ref[...]).T.astype(o_ref.dtype)


def mla_prompt_call(qcat, kcat, tq=256, tk=256):
    B, TH, W = qcat.shape
    T = kcat.shape[1]
    vt = kcat[:, :, :MLA_KV_LORA].transpose(0, 2, 1)
    pairs = [(i, j) for i in range(T // tq) for j in range((i * tq + tq - 1) // tk + 1)]
    qi_tab = jnp.asarray([p[0] for p in pairs], jnp.int32)
    kj_tab = jnp.asarray([p[1] for p in pairs], jnp.int32)
    R = tq * MLA_HEADS
    return pl.pallas_call(
        functools.partial(_mla_prompt_body, tq=tq, tk=tk, scale=(MLA_NOPE + MLA_ROPE) ** -0.5),
        grid_spec=pltpu.PrefetchScalarGridSpec(
            num_scalar_prefetch=2, grid=(B, len(pairs)),
            in_specs=[pl.BlockSpec((None, R, W), lambda b, p, qi, kj: (b, qi[p], 0)),
                      pl.BlockSpec((None, tk, W), lambda b, p, qi, kj: (b, kj[p], 0)),
                      pl.BlockSpec((None, MLA_KV_LORA, tk), lambda b, p, qi, kj: (b, 0, kj[p]))],
            out_specs=pl.BlockSpec((None, R, MLA_KV_LORA), lambda b, p, qi, kj: (b, qi[p], 0)),
            scratch_shapes=[pltpu.VMEM((1, R), F32), pltpu.VMEM((1, R), F32),
                            pltpu.VMEM((MLA_KV_LORA, R), F32)]),
        out_shape=jax.ShapeDtypeStruct((B, TH, MLA_KV_LORA), BF16),
        compiler_params=_cparams(("arbitrary", "arbitrary")),
        name="mla_prompt",
    )(qi_tab, kj_tab, qcat, kcat, vt)


def _page_stream(pt_ref, streams, npc, n_pages):
    b, t = pl.program_id(0), pl.program_id(1)
    steps = pl.num_programs(1)
    n = b * steps + t
    slot = n % 2

    def each(bb, tt, sl, act):
        page0 = (tt * npc) % n_pages
        for pred, pool, buf, sem in streams:
            def go(pool=pool, buf=buf, sem=sem):
                for j in range(npc):
                    act(pltpu.make_async_copy(pool.at[pt_ref[bb, page0 + j]], buf.at[sl, j], sem.at[sl]))
            live = pred(tt)
            if live is True:
                go()
            else:
                pl.when(live)(go)

    @pl.when(n == 0)
    def _():
        each(b, t, slot, lambda cp: cp.start())

    @pl.when(n + 1 < pl.num_programs(0) * steps)
    def _():
        wrap = t + 1 == steps
        each(jnp.where(wrap, b + 1, b), jnp.where(wrap, 0, t + 1), 1 - slot, lambda cp: cp.start())

    each(b, t, slot, lambda cp: cp.wait())
    return slot


def _mla_sample_body(pt_ref, q_ref, kn_ref, poolc_ref, poolr_ref, o_ref,
                     cbuf, rbuf, semc, semr, m_ref, l_ref, acc_ref, *, npc, n_pages, scale):
    c = pl.program_id(1)
    nch = n_pages // npc
    always = lambda t: True
    slot = _page_stream(pt_ref, [(always, poolc_ref, cbuf, semc), (always, poolr_ref, rbuf, semr)],
                        npc, n_pages)

    @pl.when(c == 0)
    def _():
        m_ref[...] = jnp.full(m_ref.shape, NEG, F32)
        l_ref[...] = jnp.zeros(l_ref.shape, F32)
        acc_ref[...] = jnp.zeros(acc_ref.shape, F32)

    q = q_ref[...]
    q_lat, q_rope = q[:, :MLA_KV_LORA], q[:, MLA_KV_LORA:]
    ckv = cbuf[slot].reshape(npc * PAGE_SIZE, MLA_KV_LORA).astype(BF16)
    rt = jnp.concatenate([rbuf[slot, j] for j in range(npc)], axis=1)
    rt = jnp.concatenate([rt, jnp.zeros((LANES - MLA_ROPE, rt.shape[1]), F32)], axis=0).astype(BF16)
    s = lax.dot_general(q_lat, ckv, _NT, preferred_element_type=F32)
    s = (s + jnp.dot(q_rope, rt, preferred_element_type=F32)) * scale
    _online_softmax_update(s, ckv, m_ref, l_ref, acc_ref)

    @pl.when(c == nch - 1)
    def _():
        kn = kn_ref[...]
        s = lax.dot_general(q, kn, _NT, preferred_element_type=F32) * scale
        qs = lax.broadcasted_iota(jnp.int32, s.shape, 0) // MLA_HEADS
        col = lax.broadcasted_iota(jnp.int32, s.shape, 1)
        s = jnp.where(col <= qs, s, NEG)
        _online_softmax_update(s, kn[:, :MLA_KV_LORA], m_ref, l_ref, acc_ref)
        o_ref[...] = (acc_ref[...] / l_ref[...]).astype(o_ref.dtype)


def mla_sample_call(page_table, qcat, kcat_new, pool_c, pool_rt, npc=32):
    B, R, W = qcat.shape
    n_pages = page_table.shape[1]
    nch = n_pages // npc
    assert n_pages % npc == 0
    return pl.pallas_call(
        functools.partial(_mla_sample_body, npc=npc, n_pages=n_pages, scale=(MLA_NOPE + MLA_ROPE) ** -0.5),
        grid_spec=pltpu.PrefetchScalarGridSpec(
            num_scalar_prefetch=1, grid=(B, nch),
            in_specs=[pl.BlockSpec((None, R, W), lambda b, c, pt: (b, 0, 0)),
                      pl.BlockSpec((None,) + kcat_new.shape[1:], lambda b, c, pt: (b, 0, 0)),
                      pl.BlockSpec(memory_space=pl.ANY), pl.BlockSpec(memory_space=pl.ANY)],
            out_specs=pl.BlockSpec((None, R, MLA_KV_LORA), lambda b, c, pt: (b, 0, 0)),
            scratch_shapes=[pltpu.VMEM((2, npc, PAGE_SIZE, MLA_KV_LORA), F32),
                            pltpu.VMEM((2, npc, MLA_ROPE, PAGE_SIZE), F32),
                            pltpu.SemaphoreType.DMA((2,)), pltpu.SemaphoreType.DMA((2,)),
                            pltpu.VMEM((R, 1), F32), pltpu.VMEM((R, 1), F32),
                            pltpu.VMEM((R, MLA_KV_LORA), F32)]),
        out_shape=jax.ShapeDtypeStruct((B, R, MLA_KV_LORA), BF16),
        compiler_params=_cparams(("arbitrary", "arbitrary")),
        name="mla_sample",
    )(page_table, qcat, kcat_new, pool_c, pool_rt)


def _kmean_body(k_ref, o_ref):
    k = k_ref[...]
    nb = k.shape[0] // MB_BLOCK
    o_ref[...] = jnp.mean(k.reshape(nb, MB_BLOCK, k.shape[1]), axis=1)


def kmean_call(k):
    B, T, W = k.shape
    nb = T // MB_BLOCK
    return pl.pallas_call(
        _kmean_body, grid=(B,),
        in_specs=[pl.BlockSpec((None, T, W), lambda b: (b, 0, 0))],
        out_specs=pl.BlockSpec((None, nb, W), lambda b: (b, 0, 0)),
        out_shape=jax.ShapeDtypeStruct((B, nb, W), F32),
        compiler_params=_cparams(("arbitrary",)),
        name="kmean",
    )(k)


def _moba_prompt_body(qi_ref, kb_ref, last_ref, q_ref, k_ref, vt_ref, kmt_ref, o_ref,
                      sel_ref, m_ref, l_ref, acc_ref, *, nb):
    p = pl.program_id(1)
    qi, kb = qi_ref[p], kb_ref[p]
    own = kb == qi
    q = q_ref[...]

    @pl.when(own)
    def _():
        gate = lax.dot_general(kmt_ref[...], q, _NT, preferred_element_type=F32,
                               precision=lax.Precision.HIGHEST)
        row = lax.broadcasted_iota(jnp.int32, gate.shape, 0)
        j = row % nb
        valid = (j < qi) & (row < MB_HEADS * nb)
        g = jnp.where(valid, gate, -jnp.inf)
        rank = jnp.zeros(g.shape, jnp.int32)
        for d in range(1, nb):
            up = pltpu.roll(g, LANES - d, 0)
            rank += jnp.where((j + d < nb) & (up > g), 1, 0)
            dn = pltpu.roll(g, d, 0)
            rank += jnp.where((j - d >= 0) & (dn >= g), 1, 0)
        sel_ref[...] = jnp.where(valid & (rank < MB_TOPK), 1.0, 0.0)
        m_ref[...] = jnp.full(m_ref.shape, NEG, F32)
        l_ref[...] = jnp.zeros(l_ref.shape, F32)
        acc_ref[...] = jnp.zeros(acc_ref.shape, F32)

    kb16 = k_ref[...].astype(BF16)
    vt16 = vt_ref[...].astype(BF16)
    dist = ((qi - kb) * MB_BLOCK + lax.broadcasted_iota(jnp.int32, (MB_BLOCK, MB_BLOCK), 1)
            - lax.broadcasted_iota(jnp.int32, (MB_BLOCK, MB_BLOCK), 0))
    distf = dist.astype(F32)
    for h in range(MB_HEADS):
        g = h // MB_GROUP
        hs = slice(h * HEAD_DIM, (h + 1) * HEAD_DIM)
        gs = slice(g * HEAD_DIM, (g + 1) * HEAD_DIM)
        st = lax.dot_general(kb16[:, gs], q[:, hs].astype(BF16), _NT, preferred_element_type=F32)
        st = st * (HEAD_DIM ** -0.5) - (2.0 ** (-8.0 * (h + 1) / MB_HEADS)) * distf
        picked = sel_ref[pl.ds(h * nb + kb, 1), :] > 0.5
        st = jnp.where(own, jnp.where(dist >= 0, st, NEG), jnp.where(picked, st, NEG))
        m_prev = m_ref[h]
        m_new = jnp.maximum(m_prev, jnp.max(st, axis=0, keepdims=True))
        alpha = jnp.exp(m_prev - m_new)
        pt = jnp.exp(st - m_new)
        l_ref[h] = alpha * l_ref[h] + jnp.sum(pt, axis=0, keepdims=True)
        acc_ref[h] = alpha * acc_ref[h] + jnp.dot(vt16[gs, :], pt.astype(BF16), preferred_element_type=F32)
        m_ref[h] = m_new

    @pl.when(last_ref[p] == 1)
    def _():
        out_t = jnp.concatenate([acc_ref[h] / l_ref[h] for h in range(MB_HEADS)], axis=0)
        o_ref[...] = out_t.T


def moba_prompt_call(q, k, v, km):
    B, T, Wq = q.shape
    Wk = k.shape[2]
    nb = T // MB_BLOCK
    vt = v.transpose(0, 2, 1)
    kmt = km.transpose(0, 2, 1)
    steps = [(i, kb) for i in range(nb) for kb in [i] + list(range(i))]
    qi_tab = jnp.asarray([s[0] for s in steps], jnp.int32)
    kb_tab = jnp.asarray([s[1] for s in steps], jnp.int32)
    last_tab = jnp.asarray([1 if (i == 0 or kb == i - 1) else 0 for i, kb in steps], jnp.int32)
    return pl.pallas_call(
        functools.partial(_moba_prompt_body, nb=nb),
        grid_spec=pltpu.PrefetchScalarGridSpec(
            num_scalar_prefetch=3, grid=(B, len(steps)),
            in_specs=[pl.BlockSpec((None, MB_BLOCK, Wq), lambda b, p, qi, kb, la: (b, qi[p], 0)),
                      pl.BlockSpec((None, MB_BLOCK, Wk), lambda b, p, qi, kb, la: (b, kb[p], 0)),
                      pl.BlockSpec((None, Wk, MB_BLOCK), lambda b, p, qi, kb, la: (b, 0, kb[p])),
                      pl.BlockSpec((None, LANES, Wq), lambda b, p, qi, kb, la: (b, 0, 0))],
            out_specs=pl.BlockSpec((None, MB_BLOCK, Wq), lambda b, p, qi, kb, la: (b, qi[p], 0)),
            scratch_shapes=[pltpu.VMEM((LANES, MB_BLOCK), F32),
                            pltpu.VMEM((MB_HEADS, 1, MB_BLOCK), F32),
                            pltpu.VMEM((MB_HEADS, 1, MB_BLOCK), F32),
                            pltpu.VMEM((MB_HEADS, HEAD_DIM, MB_BLOCK), F32)]),
        out_shape=jax.ShapeDtypeStruct((B, T, Wq), F32),
        compiler_params=_cparams(("arbitrary", "arbitrary")),
        name="moba_prompt",
    )(qi_tab, kb_tab, last_tab, q, k, vt, kmt)


def _gate_matrix(kmean):
    B, nb, _ = kmean.shape
    kvh = np.arange(MB_HEADS) // MB_GROUP
    km = kmean.reshape(B, nb, MB_KV_HEADS, HEAD_DIM)[:, :, kvh, :].transpose(0, 2, 3, 1)
    eye = jnp.eye(MB_HEADS, dtype=F32)
    km = (km[:, :, :, None, :] * eye[None, :, None, :, None]).reshape(B, MB_HEADS * HEAD_DIM, MB_HEADS * nb)
    return jnp.pad(km, ((0, 0), (0, 0), (0, LANES - MB_HEADS * nb)))


def _moba_sample_body(pt_ref, qbd_ref, kn_ref, vn_ref, rc_ref, poolk_ref, poolv_ref, o_ref,
                      buf, sem, s_ref, p_ref, gate_ref, l_ref, acc_ref, *, npc, n_pages):
    t = pl.program_id(1)
    nch = n_pages // npc
    past = n_pages * PAGE_SIZE
    ph, c = t // nch, t % nch
    nb = past // MB_BLOCK
    ck = npc * PAGE_SIZE
    scale = HEAD_DIM ** -0.5
    R = qbd_ref.shape[0]
    slope = rc_ref[:, 0:1]
    qs = rc_ref[:, 1:2]
    slot = _page_stream(pt_ref, [(lambda tt: tt < nch, poolk_ref, buf, sem),
                                 (lambda tt: tt >= nch, poolv_ref, buf, sem)], npc, n_pages)

    def chunk_t():
        return jnp.concatenate([buf[slot, j] for j in range(npc)], axis=1).astype(BF16)

    glane = lax.broadcasted_iota(jnp.int32, gate_ref.shape, 1)

    @pl.when(ph == 0)
    def _():
        s = jnp.dot(qbd_ref[...].astype(BF16), chunk_t(), preferred_element_type=F32)
        s_ref[:, pl.ds(pl.multiple_of(c * ck, ck), ck)] = s
        g = jnp.where(c == 0, jnp.full(gate_ref.shape, -jnp.inf, F32), gate_ref[...])
        for i in range(ck // MB_BLOCK):
            bsum = jnp.sum(s[:, i * MB_BLOCK:(i + 1) * MB_BLOCK], axis=-1, keepdims=True)
            g = jnp.where(glane == c * (ck // MB_BLOCK) + i, bsum, g)
        gate_ref[...] = g

    @pl.when(t == nch - 1)
    def _():
        g = gate_ref[...]
        picks = []
        for _ in range(MB_TOPK):
            mx = jnp.max(g, axis=-1, keepdims=True)
            idx = jnp.min(jnp.where(g == mx, glane, LANES), axis=-1, keepdims=True)
            picks.append(idx)
            g = jnp.where(glane == idx, -jnp.inf, g)

        qb = qbd_ref[...].astype(BF16)
        kn = kn_ref[...].astype(BF16)
        s_own = lax.dot_general(qb, kn, _NT, preferred_element_type=F32) * scale
        col = lax.broadcasted_iota(jnp.int32, s_own.shape, 1).astype(F32)
        s_own = jnp.where(col <= qs, s_own - slope * (qs - col), NEG)
        m0 = jnp.max(s_own, axis=-1, keepdims=True)

        wide = lambda c: jnp.broadcast_to(c, (R, MB_BLOCK))
        kcol = lax.broadcasted_iota(jnp.int32, (R, MB_BLOCK), 1).astype(F32)
        slope_w = wide(slope)
        bias0 = slope_w * kcol - wide(slope * (past + qs))
        picks_w = [wide(pk) for pk in picks]

        def block_scores(j):
            s = s_ref[:, pl.ds(pl.multiple_of(j * MB_BLOCK, MB_BLOCK), MB_BLOCK)]
            s = s * scale + (bias0 + slope_w * jnp.asarray(j * MB_BLOCK).astype(F32))
            picked = (picks_w[0] == j) | (picks_w[1] == j) | (picks_w[2] == j)
            return jnp.where(picked, s, NEG)

        m_acc = lax.fori_loop(0, nb, lambda j, a: jnp.maximum(a, block_scores(j)),
                              jnp.full((R, MB_BLOCK), NEG, F32))
        m = jnp.maximum(m0, jnp.max(m_acc, axis=-1, keepdims=True))

        def fill(j, a):
            p = jnp.exp(block_scores(j) - m)
            p_ref[:, pl.ds(pl.multiple_of(j * MB_BLOCK, MB_BLOCK), MB_BLOCK)] = p.astype(BF16)
            return a + p

        l_acc = lax.fori_loop(0, nb, fill, jnp.zeros((R, MB_BLOCK), F32))
        p_own = jnp.exp(s_own - m)
        l_ref[...] = jnp.sum(p_own, axis=-1, keepdims=True) + jnp.sum(l_acc, axis=-1, keepdims=True)
        acc_ref[...] = jnp.dot(p_own.astype(BF16), vn_ref[...].astype(BF16), preferred_element_type=F32)

    @pl.when(ph == 1)
    def _():
        pb = p_ref[:, pl.ds(pl.multiple_of(c * ck, ck), ck)]
        acc_ref[...] += lax.dot_general(pb, chunk_t(), _NT, preferred_element_type=F32)

    @pl.when(t == 2 * nch - 1)
    def _():
        out = acc_ref[...] / l_ref[...]
        S = R // MB_HEADS
        o_ref[...] = jnp.concatenate(
            [out[h * S:(h + 1) * S, (h // MB_GROUP) * HEAD_DIM:(h // MB_GROUP + 1) * HEAD_DIM]
             for h in range(MB_HEADS)], axis=-1)


def moba_sample_call(page_table, q, k_new, v_new, pool_kt, pool_vt, npc=32):
    B, S, Wq = q.shape
    Wk = k_new.shape[2]
    n_pages = page_table.shape[1]
    past = n_pages * PAGE_SIZE
    nch = n_pages // npc
    assert past % MB_BLOCK == 0 and past // MB_BLOCK >= MB_TOPK and n_pages % npc == 0
    R = MB_HEADS * S
    kvh = np.arange(MB_HEADS) // MB_GROUP
    onehot = jnp.asarray(np.eye(MB_KV_HEADS, dtype=np.float32)[kvh])
    qh = q.reshape(B, S, MB_HEADS, HEAD_DIM).transpose(0, 2, 1, 3)
    qbd = (qh[:, :, :, None, :] * onehot[None, :, None, :, None]).reshape(B, R, Wk)
    pad = lambda t: jnp.pad(t, ((0, 0), (0, LANES - S), (0, 0)))
    rc = np.zeros((R, LANES), np.float32)
    rc[:, 0] = np.repeat([2.0 ** (-8.0 * (h + 1) / MB_HEADS) for h in range(MB_HEADS)], S)
    rc[:, 1] = np.tile(np.arange(S), MB_HEADS)
    full = lambda shp: pl.BlockSpec((None,) + shp, lambda b, t, pt: (b, 0, 0))
    return pl.pallas_call(
        functools.partial(_moba_sample_body, npc=npc, n_pages=n_pages),
        grid_spec=pltpu.PrefetchScalarGridSpec(
            num_scalar_prefetch=1, grid=(B, 2 * nch),
            in_specs=[full((R, Wk)), full((LANES, Wk)), full((LANES, Wk)),
                      pl.BlockSpec((R, LANES), lambda b, t, pt: (0, 0)),
                      pl.BlockSpec(memory_space=pl.ANY), pl.BlockSpec(memory_space=pl.ANY)],
            out_specs=full((S, Wq)),
            scratch_shapes=[pltpu.VMEM((2, npc, Wk, PAGE_SIZE), F32), pltpu.SemaphoreType.DMA((2,)),
                            pltpu.VMEM((R, past), F32), pltpu.VMEM((R, past), BF16),
                            pltpu.VMEM((R, LANES), F32),
                            pltpu.VMEM((R, 1), F32), pltpu.VMEM((R, Wk), F32)]),
        out_shape=jax.ShapeDtypeStruct((B, S, Wq), F32),
        compiler_params=_cparams(("arbitrary", "arbitrary")),
        name="moba_sample",
    )(page_table, qbd, pad(k_new), pad(v_new), jnp.asarray(rc), pool_kt, pool_vt)


def _l2norm(x):
    return x * lax.rsqrt(jnp.sum(x * x, axis=-1, keepdims=True) + 1e-12)


def _rope_tables(pos):
    half = MLA_ROPE // 2
    inv = ROPE_THETA ** (-jnp.arange(half, dtype=F32) / half)
    ang = pos.astype(F32)[:, None] * inv[None, :]
    return jnp.cos(ang), jnp.sin(ang)


def _rwkv_mix(p, prev, S0, prm, lanes_bh):
    mu, w0, w2, a0, a2, g2, k_k, k_a, r_k, lnx_g, lnx_b = prm
    B, T, _ = p.shape
    H = RW_HEADS
    p_prev = jnp.concatenate([prev[:, None, :], p[:, :-1]], axis=1)
    pm = p + (p_prev - p) * mu
    cuts = [int(c) for c in np.cumsum(RW_SPLITS)[:-1]]
    r, wl, k, v, al, gl = jnp.split(pm, cuts, axis=-1)
    w_log = -jax.nn.softplus(-(w0 + jnp.tanh(wl) @ w2)) - 0.5
    decay = jnp.exp(-jnp.exp(w_log))
    a = jax.nn.sigmoid(a0 + al @ a2)
    g = jax.nn.sigmoid(gl) @ g2
    heads = lambda t: t.reshape(B, T, H, HEAD_DIM)
    kk = _l2norm(heads(k * k_k))
    k = k * (1.0 + (a - 1.0) * k_a)
    r_h, k_h, v_h, w_h, a_h = heads(r), heads(k), heads(v), heads(decay), heads(a)
    aa, bb = -kk, kk * a_h
    ts = lambda t: _to_scan(t, lanes_bh)
    s0 = _state_to_scan(jnp.swapaxes(S0, -1, -2), lanes_bh)
    y, S = scan_call(ts(r_h), ts(w_h), ts(k_h), ts(v_h), ts(aa), ts(bb), s0, "layer")
    y = _from_scan(y, B, H, lanes_bh).reshape(B, T, RW_WIDTH) * lnx_g + lnx_b
    S = jnp.swapaxes(_state_from_scan(S, B, H, lanes_bh), -1, -2)
    bonus = jnp.sum(r_h * k_h * r_k, axis=-1, keepdims=True) * v_h
    y = (y + bonus.reshape(B, T, RW_WIDTH)) * g
    return y, S, p[:, -1]


def _gdn_mix(qkv, z, a_raw, b_raw, conv_buf, S0, prm, lanes_bh):
    conv_w, a_log, dt_bias, norm_g = prm
    B, T, _ = qkv.shape
    H = GD_HEADS
    xx = jnp.concatenate([conv_buf, qkv], axis=1)
    y = sum(xx[:, j:j + T] * conv_w[j] for j in range(GD_CONV))
    y = _silu(y)
    q, k, v = jnp.split(y, [H * GD_DK, 2 * H * GD_DK], axis=-1)
    q = _l2norm(q.reshape(B, T, H, GD_DK)) * (GD_DK ** -0.5)
    k = _l2norm(k.reshape(B, T, H, GD_DK))
    v = v.reshape(B, T, H, GD_DV)
    beta = jax.nn.sigmoid(b_raw)[..., None]
    alpha = jnp.exp(-jnp.exp(a_log) * jax.nn.softplus(a_raw + dt_bias))[..., None]
    ts = lambda t: _to_scan(t, lanes_bh)
    w = jnp.broadcast_to(alpha, k.shape)
    y, S = scan_call(ts(q), ts(w), ts(k), ts(beta * v), ts(k), ts(-beta * alpha * k),
                     _state_to_scan(S0, lanes_bh), "rms")
    o = _from_scan(y, B, H, lanes_bh) * norm_g * _silu(z.reshape(B, T, H, GD_DV))
    S = _state_from_scan(S, B, H, lanes_bh)
    return o.reshape(B, T, H * GD_DV), S, xx[:, -(GD_CONV - 1):]


def _bdiag(w):
    H, a, b = w.shape
    eye = jnp.eye(H, dtype=w.dtype)
    return (eye[:, None, :, None] * w[:, :, None, :]).reshape(H * a, H * b)


def kernel(x_prompt, x_sample, cache_moba_k, cache_moba_v, cache_mla_ckv, cache_mla_krope, state_rwkv_wkv, state_rwkv_shift, state_gdn, state_gdn_conv, page_table, c_prompt, c_sample, w_ada, b_ada, g_mix_pre, g_mix_post, g_ff_pre, g_ff_post, w_in_even, w_out_even, rw_mu, rw_w0, rw_w2, rw_a0, rw_a2, rw_g2, rw_kk, rw_ka, rw_rk, rw_lnx_g, rw_lnx_b, ffn_w_gate, ffn_w_up, ffn_w_down, w_in_odd, w_out_odd, mla_q_norm, mla_w_q_up, mla_kv_norm, mla_w_uk, mla_w_uv, gdn_conv_w, gdn_a_log, gdn_dt_bias, gdn_norm_g, moe_w_router, moe_b_router, moe_w_gate, moe_w_up, moe_w_down):
    Bp, Tp, D = x_prompt.shape
    Bs, Ts, _ = x_sample.shape
    past = page_table.shape[1] * PAGE_SIZE
    Ms = Bs * Ts
    row = lambda v: v.reshape(1, -1)

    xp = x_prompt
    xs = x_sample.reshape(1, Ms, D)
    c_all = jnp.concatenate([c_prompt, c_sample], axis=0)

    outs_p, outs_s = {}, {}
    for layer in range(w_ada.shape[0]):
        i = layer // 2
        mod = adaln_call(c_all, w_ada[layer].astype(BF16), row(b_ada[layer]))
        mp = [m.reshape(Bp, 1, D) for m in jnp.split(mod[:Bp], 6, axis=-1)]
        ms = [jnp.broadcast_to(m[:, None, :], (Bs, Ts, D)).reshape(1, Ms, D)
              for m in jnp.split(mod[Bp:], 6, axis=-1)]
        gpre, gpost = row(g_mix_pre[layer]), row(g_mix_post[layer])
        if layer % 2 == 0:
            w_in = w_in_even[i].astype(BF16)
            splits = (RW_PROJ, MB_HEADS * HEAD_DIM, MB_KV_HEADS * HEAD_DIM, MB_KV_HEADS * HEAD_DIM)
            rw_prm = (rw_mu[i], rw_w0[i], rw_w2[i], rw_a0[i], rw_a2[i], rw_g2[i], rw_kk[i], rw_ka[i],
                      rw_rk[i], rw_lnx_g[i], rw_lnx_b[i])
            w_out = w_out_even[i].astype(BF16)
            p_rw, q, k, v = modmm_call(xp, mp[0], mp[1], gpre, w_in, splits)
            y_rw, S_p, shift_p = _rwkv_mix(p_rw, jnp.zeros((Bp, RW_PROJ), F32),
                                           jnp.zeros((Bp, RW_HEADS, HEAD_DIM, HEAD_DIM), F32), rw_prm, True)
            k4 = k.reshape(Bp, Tp, MB_KV_HEADS, HEAD_DIM)
            v4 = v.reshape(Bp, Tp, MB_KV_HEADS, HEAD_DIM)
            y_mb = moba_prompt_call(q, k, v, _gate_matrix(kmean_call(k)))
            xp = outproj_call(y_rw, y_mb, w_out[:RW_WIDTH], w_out[RW_WIDTH:], xp, mp[2], gpost)
            outs_p.update(mk=k4[None], mv=v4[None], wkv=S_p[None], shift=shift_p[None])
            p_rw, q, k, v = modmm_call(xs, ms[0], ms[1], gpre, w_in, splits)
            y_rw, S_s, shift_s = _rwkv_mix(p_rw.reshape(Bs, Ts, RW_PROJ), state_rwkv_shift[i],
                                           state_rwkv_wkv[i], rw_prm, False)
            k4 = k.reshape(Bs, Ts, MB_KV_HEADS, HEAD_DIM)
            v4 = v.reshape(Bs, Ts, MB_KV_HEADS, HEAD_DIM)
            n_pool = cache_moba_k.shape[1]
            pool_t = lambda c: c[i].transpose(0, 2, 3, 1).reshape(n_pool, -1, PAGE_SIZE)
            y_mb = moba_sample_call(page_table, q.reshape(Bs, Ts, -1), k.reshape(Bs, Ts, -1), v.reshape(Bs, Ts, -1),
                                    pool_t(cache_moba_k), pool_t(cache_moba_v))
            xs = outproj_call(y_rw.reshape(1, Ms, RW_WIDTH), y_mb.reshape(1, Ms, -1),
                              w_out[:RW_WIDTH], w_out[RW_WIDTH:], xs, ms[2], gpost)
            outs_s.update(mk=k4[None], mv=v4[None], wkv=S_s[None], shift=shift_s[None])
        else:
            w_in = w_in_odd[i]
            cuts = [int(c) for c in np.cumsum(ODD_SPLITS)[:-1]]
            w_cq, w_ckv, w_kr, w_qkv, w_z, w_a, w_b = jnp.split(w_in, cuts, axis=1)
            padc = lambda w, n: jnp.pad(w, ((0, 0), (0, n - w.shape[1])))
            w_in_p = jnp.concatenate([w_cq, w_ckv, padc(w_kr, LANES), w_qkv, w_z,
                                      padc(jnp.concatenate([w_a, w_b], axis=1), LANES)], axis=1).astype(BF16)
            splits = (MLA_Q_LORA, MLA_KV_LORA, LANES, GD_QKV, GD_HEADS * GD_DV, LANES)
            w_out = w_out_odd[i].astype(BF16)
            gd_prm = (gdn_conv_w[i], gdn_a_log[i], gdn_dt_bias[i], gdn_norm_g[i])
            w_uv_bd = _bdiag(mla_w_uv[i].transpose(1, 0, 2)).astype(BF16)
            w_uk_bd = _bdiag(mla_w_uk[i].transpose(1, 2, 0)).astype(BF16)
            wq = mla_w_q_up[i].reshape(MLA_Q_LORA, MLA_HEADS, MLA_NOPE + MLA_ROPE)
            w_qn = wq[:, :, :MLA_NOPE].reshape(MLA_Q_LORA, -1).astype(BF16)
            w_qr = jnp.pad(wq[:, :, MLA_NOPE:], ((0, 0), (0, 0), (0, LANES - MLA_ROPE)))
            w_qr = w_qr.reshape(MLA_Q_LORA, -1).astype(BF16)
            w_oa, w_ob = w_out[:MLA_HEADS * MLA_V], w_out[MLA_HEADS * MLA_V:]

            def odd(x, m, B, T, cosf, sinf, conv_buf, S0, lanes_bh):
                cq, ckv, kr, qkv, z, ab = modmm_call(x, m[0], m[1], gpre, w_in_p, splits)
                qcat, kcat, ckv_n, k_rope = mla_prep_call(
                    cq, ckv, kr, cosf, sinf, row(mla_q_norm[i]), row(mla_kv_norm[i]), w_qn, w_qr, w_uk_bd)
                qkv, z = qkv.reshape(B, T, -1), z.reshape(B, T, -1)
                ab = ab.reshape(B, T, LANES)
                a_raw, b_raw = ab[..., :GD_HEADS], ab[..., GD_HEADS:2 * GD_HEADS]
                y_gdn, S, buf = _gdn_mix(qkv, z, a_raw, b_raw, conv_buf, S0, gd_prm, lanes_bh)
                ckv_n = ckv_n.reshape(B, T, MLA_KV_LORA)
                k_rope = k_rope.reshape(B, T, LANES)[..., :MLA_ROPE]
                return qcat, kcat, y_gdn, ckv_n, k_rope, S, buf

            cosf, sinf = _rope_rows(jnp.arange(Tp))
            qcat, kcat, y_gdn, ckv_n, k_rope, S_p, buf_p = odd(
                xp, mp, Bp, Tp, cosf, sinf, jnp.zeros((Bp, GD_CONV - 1, GD_QKV), F32),
                jnp.zeros((Bp, GD_HEADS, GD_DK, GD_DV), F32), True)
            o_lat = mla_prompt_call(qcat.reshape(Bp, Tp * MLA_HEADS, 2 * LANES), kcat)
            xp = outproj_call(o_lat.reshape(Bp, Tp, -1), y_gdn, w_oa, w_ob, xp, mp[2], gpost, wpre=w_uv_bd)
            outs_p.update(ckv=ckv_n[None], krope=k_rope[None], gdn=S_p[None], conv=buf_p[None])
            cosf, sinf = _rope_rows(jnp.tile(past + jnp.arange(Ts), Bs))
            qcat, kcat, y_gdn, ckv_n, k_rope, S_s, buf_s = odd(
                xs, ms, Bs, Ts, cosf, sinf, state_gdn_conv[i], state_gdn[i], False)
            kn = jnp.pad(kcat.reshape(Bs, Ts, 2 * LANES), ((0, 0), (0, LANES - Ts), (0, 0)))
            o_lat = mla_sample_call(page_table, qcat.reshape(Bs, Ts * MLA_HEADS, 2 * LANES), kn,
                                    cache_mla_ckv[i], cache_mla_krope[i].transpose(0, 2, 1))
            xs = outproj_call(o_lat.reshape(1, Ms, -1), y_gdn.reshape(1, Ms, -1), w_oa, w_ob, xs, ms[2], gpost,
                              wpre=w_uv_bd)
            outs_s.update(ckv=ckv_n[None], krope=k_rope[None], gdn=S_s[None], conv=buf_s[None])

        gpre, gpost = row(g_ff_pre[layer]), row(g_ff_post[layer])
        if layer % 2 == 0:
            J = 2
            Fd = ffn_w_gate.shape[2] // J
            wg = ffn_w_gate[i].astype(BF16).reshape(D, J, Fd).transpose(1, 0, 2)
            wu = ffn_w_up[i].astype(BF16).reshape(D, J, Fd).transpose(1, 0, 2)
            wd = ffn_w_down[i].astype(BF16).reshape(J, Fd, D)
            xp = ffn_call(xp, mp[3], mp[4], mp[5], gpre, gpost, wg, wu, wd)
            xs = ffn_call(xs, ms[3], ms[4], ms[5], gpre, gpost, wg, wu, wd)
        else:
            wr = jnp.pad(moe_w_router[i], ((0, 0), (0, LANES - N_EXPERTS)))
            br = jnp.pad(row(moe_b_router[i]), ((0, 0), (0, LANES - N_EXPERTS)))
            wg, wu, wd = (w[i].astype(BF16) for w in (moe_w_gate, moe_w_up, moe_w_down))
            xp = ffn_call(xp, mp[3], mp[4], mp[5], gpre, gpost, wg, wu, wd, wr, br)
            xs = ffn_call(xs, ms[3], ms[4], ms[5], gpre, gpost, wg, wu, wd, wr, br)

    names = ('mk', 'mv', 'wkv', 'shift', 'ckv', 'krope', 'gdn', 'conv')
    return ((xp, xs.reshape(Bs, Ts, D)) + tuple(outs_p[n] for n in names)
            + tuple(outs_s[n] for n in names))
```
